```python
import jax, jax.numpy as jnp
from jax import lax
import numpy as np

D_MODEL = 4096
BATCH = 1
SEQ = 8192
DEPTH = 1

GRID_W = 64
CTX_LEN = 256
HEAD_DIM = 128
MIX_WIDTH = D_MODEL
W_SGU = MIX_WIDTH // 2
W_NA = MIX_WIDTH - W_SGU
N_SGU_HEADS = W_SGU // HEAD_DIM
N_NA_HEADS = W_NA // HEAD_DIM
CHUNK = 128
WIN_H = 8
WIN_W = 16
D_FF = ((8 * D_MODEL + 3 * 256 - 1) // (3 * 256)) * 256
IN_COLS = 2 * W_SGU + 3 * W_NA
SPLIT_POINTS = (W_SGU, 2 * W_SGU, 2 * W_SGU + W_NA, 2 * W_SGU + 2 * W_NA)
DEEPNORM_ALPHA = (2.0 * DEPTH) ** 0.25
DEEPNORM_BETA = (8.0 * DEPTH) ** -0.25
LN_EPS = 1e-5
ADA_INIT = 0.5

kernel_name = "hybrid_sgu_natten_dit_block"


def _layer_norm(x, gain, bias):
    xf = x.astype(jnp.float32)
    mu = jnp.mean(xf, axis=-1, keepdims=True)
    var = jnp.mean(jnp.square(xf - mu), axis=-1, keepdims=True)
    y = (xf - mu) * lax.rsqrt(var + LN_EPS)
    if gain is not None:
        y = y * gain.astype(jnp.float32) + bias.astype(jnp.float32)
    return y.astype(x.dtype)


def _rms_norm(x):
    xf = x.astype(jnp.float32)
    return (xf * lax.rsqrt(jnp.mean(jnp.square(xf), axis=-1, keepdims=True) + LN_EPS)).astype(x.dtype)


def _modulation(cond, w_ada, b_ada):
    return jnp.split(jax.nn.silu(cond) @ w_ada + b_ada, 6, axis=-1)


def _heads(t, n_heads):
    return t.reshape(t.shape[:-1] + (n_heads, HEAD_DIM))


def _chunk_spatial_gating(u, v, w_s, b_s, ln_g, ln_b):
    b, length, _ = u.shape
    v = _layer_norm(v, ln_g, ln_b).reshape(b, length // CHUNK, CHUNK, N_SGU_HEADS, HEAD_DIM)
    mixed = jnp.einsum("hpq,bnqhd->bnphd", w_s, v) + b_s.T[None, None, :, :, None]
    return u * mixed.reshape(b, length, W_SGU)


def _neighbourhood_attention(q, k, v, k_ctx, v_ctx, rpb):
    b, s, h, dh = q.shape
    rows = s // GRID_W
    wh = min(WIN_H, rows)
    r = jnp.arange(rows)
    key_rows = jnp.clip(r - wh // 2, 0, rows - wh)[:, None] + jnp.arange(wh)[None, :]
    col = jnp.arange(GRID_W)
    col_start = jnp.clip(col - WIN_W // 2, 0, GRID_W - WIN_W)
    col_mask = (col[None, :] >= col_start[:, None]) & (col[None, :] < col_start[:, None] + WIN_W)
    q = q.reshape(b, rows, GRID_W, h, dh) * (dh ** -0.5)
    k_blk = jnp.take(k.reshape(b, rows, GRID_W, h, dh), key_rows, axis=1)
    v_blk = jnp.take(v.reshape(b, rows, GRID_W, h, dh), key_rows, axis=1)
    s_lat = jnp.einsum("brqhd,brikhd->bhrqik", q, k_blk).astype(jnp.float32)
    d_row = key_rows - r[:, None] + (WIN_H - 1)
    d_col = jnp.clip(col[None, :] - col[:, None], -(WIN_W - 1), WIN_W - 1) + (WIN_W - 1)
    bias = rpb[:, d_row[:, None, :, None], d_col[None, :, None, :]]
    s_lat = jnp.where(col_mask[:, None, :], s_lat + bias.astype(jnp.float32), -jnp.inf)
    s_ctx = jnp.einsum("brqhd,bchd->bhrqc", q, k_ctx).astype(jnp.float32)
    n_lat = wh * GRID_W
    p = jax.nn.softmax(jnp.concatenate([s_lat.reshape(b, h, rows, GRID_W, n_lat), s_ctx], axis=-1), axis=-1).astype(v.dtype)
    p_lat = p[..., :n_lat].reshape(b, h, rows, GRID_W, wh, GRID_W)
    out = jnp.einsum("bhrqik,brikhd->brqhd", p_lat, v_blk) + jnp.einsum("bhrqc,bchd->brqhd", p[..., n_lat:], v_ctx)
    return out.reshape(b, s, h * dh)


def _context_attention(q, k, v):
    s = jnp.einsum("bqhd,bkhd->bhqk", q * (HEAD_DIM ** -0.5), k).astype(jnp.float32)
    p = jax.nn.softmax(s, axis=-1).astype(v.dtype)
    out = jnp.einsum("bhqk,bkhd->bqhd", p, v)
    return out.reshape(out.shape[:2] + (W_NA,))


def _merge(out_a, out_b, gn_g, w_o):
    return (jnp.concatenate([_rms_norm(out_a), _rms_norm(out_b)], axis=-1) * gn_g) @ w_o


def _mixing(h, hc, w_in, w_s, b_s, sgu_g, sgu_b, rpb, gn_g, w_o, update_ctx):
    u, vs, q, k, v = jnp.split(h @ w_in, SPLIT_POINTS, axis=-1)
    if update_ctx:
        uc, vsc, qc, kc, vc = jnp.split(hc @ w_in, SPLIT_POINTS, axis=-1)
    else:
        kc, vc = jnp.split(hc @ w_in[:, SPLIT_POINTS[2]:], 2, axis=-1)
    kc, vc = _heads(kc, N_NA_HEADS), _heads(vc, N_NA_HEADS)
    out_a = _chunk_spatial_gating(jax.nn.gelu(u, approximate=False), jax.nn.gelu(vs, approximate=False),
                                  w_s, b_s, sgu_g, sgu_b)
    out_b = _neighbourhood_attention(_heads(q, N_NA_HEADS), _heads(k, N_NA_HEADS), _heads(v, N_NA_HEADS), kc, vc, rpb)
    y = _merge(out_a, out_b, gn_g, w_o)
    yc = None
    if update_ctx:
        out_ac = _chunk_spatial_gating(jax.nn.gelu(uc, approximate=False), jax.nn.gelu(vsc, approximate=False),
                                       w_s, b_s, sgu_g, sgu_b)
        out_bc = _context_attention(_heads(qc, N_NA_HEADS), kc, vc)
        yc = _merge(out_ac, out_bc, gn_g, w_o)
    return y, yc


def _swiglu(h, w_gate, w_up, w_down):
    return (jax.nn.silu(h @ w_gate) * (h @ w_up)) @ w_down


def setup_inputs(seed: int = 0) -> dict:
    key = jax.random.key(seed)
    ks = jax.random.split(key, 20)
    n = jax.random.normal
    f32 = jnp.float32
    return {
        "x": n(ks[0], (BATCH, SEQ, D_MODEL), f32),
        "c": n(ks[1], (BATCH, D_MODEL), f32),
        "ctx": n(ks[2], (BATCH, CTX_LEN, D_MODEL), f32),
        "c_ctx": n(ks[3], (D_MODEL,), f32),
        "w_ada": n(ks[4], (DEPTH, D_MODEL, 6 * D_MODEL), f32) * (ADA_INIT * D_MODEL ** -0.5),
        "b_ada": n(ks[5], (DEPTH, 6 * D_MODEL), f32) * 0.01,
        "w_in": n(ks[6], (DEPTH, D_MODEL, IN_COLS), f32) * D_MODEL ** -0.5,
        "w_s": n(ks[7], (DEPTH, N_SGU_HEADS, CHUNK, CHUNK), f32) * CHUNK ** -0.5,
        "b_s": 1.0 + 0.01 * n(ks[8], (DEPTH, N_SGU_HEADS, CHUNK), f32),
        "sgu_g": 1.0 + 0.01 * n(ks[9], (DEPTH, W_SGU), f32),
        "sgu_b": 0.01 * n(ks[10], (DEPTH, W_SGU), f32),
        "rpb": 0.1 * n(ks[11], (DEPTH, N_NA_HEADS, 2 * WIN_H - 1, 2 * WIN_W - 1), f32),
        "gn_g": 1.0 + 0.01 * n(ks[12], (DEPTH, MIX_WIDTH), f32),
        "w_o": n(ks[13], (DEPTH, MIX_WIDTH, D_MODEL), f32) * (DEEPNORM_BETA * MIX_WIDTH ** -0.5),
        "ln_g": 1.0 + 0.01 * n(ks[14], (DEPTH, 2, D_MODEL), f32),
        "ln_b": 0.01 * n(ks[15], (DEPTH, 2, D_MODEL), f32),
        "w_gate": n(ks[16], (DEPTH, D_MODEL, D_FF), f32) * D_MODEL ** -0.5,
        "w_up": n(ks[17], (DEPTH, D_MODEL, D_FF), f32) * D_MODEL ** -0.5,
        "w_down": n(ks[18], (DEPTH, D_FF, D_MODEL), f32) * (DEEPNORM_BETA * D_FF ** -0.5),
    }


def reference(x, c, ctx, c_ctx, w_ada, b_ada, w_in, w_s, b_s, sgu_g, sgu_b, rpb, gn_g, w_o, ln_g, ln_b,
              w_gate, w_up, w_down):
    x = _layer_norm(x, None, None)
    ctx = _layer_norm(ctx, None, None)
    alpha = DEEPNORM_ALPHA
    for layer in range(DEPTH):
        update_ctx = layer < DEPTH - 1
        sh1, sc1, g1, sh2, sc2, g2 = _modulation(c[:, None, :], w_ada[layer], b_ada[layer])
        csh1, csc1, cg1, csh2, csc2, cg2 = _modulation(c_ctx[None, None, :], w_ada[layer], b_ada[layer])
        y, yc = _mixing(x * (1 + sc1) + sh1, ctx * (1 + csc1) + csh1, w_in[layer], w_s[layer], b_s[layer],
                        sgu_g[layer], sgu_b[layer], rpb[layer], gn_g[layer], w_o[layer], update_ctx)
        x = _layer_norm(alpha * x + g1 * y, ln_g[layer, 0], ln_b[layer, 0])
        f = _swiglu(x * (1 + sc2) + sh2, w_gate[layer], w_up[layer], w_down[layer])
        x = _layer_norm(alpha * x + g2 * f, ln_g[layer, 1], ln_b[layer, 1])
        if update_ctx:
            ctx = _layer_norm(alpha * ctx + cg1 * yc, ln_g[layer, 0], ln_b[layer, 0])
            fc = _swiglu(ctx * (1 + csc2) + csh2, w_gate[layer], w_up[layer], w_down[layer])
            ctx = _layer_norm(alpha * ctx + cg2 * fc, ln_g[layer, 1], ln_b[layer, 1])
    return x
```

```python
import functools
import math

import jax
import jax.numpy as jnp
from jax import lax
from jax.experimental import pallas as pl
from jax.experimental.pallas import tpu as pltpu

F32 = jnp.float32
BF16 = jnp.bfloat16

GRID_W = 64
HEAD_DIM = 128
CHUNK = 128
WIN_H = 8
WIN_W = 16
LN_EPS = 1e-5
DEPTH = 1
DEEPNORM_ALPHA = (2.0 * DEPTH) ** 0.25

LANES = 128
SUBLANES = 8
VMEM_LIMIT = 56 * 1024 * 1024


def _cparams(sem):
    return pltpu.CompilerParams(dimension_semantics=sem, vmem_limit_bytes=VMEM_LIMIT)


def _gelu(x):
    return 0.5 * x * (1.0 + lax.erf(x * (1.0 / math.sqrt(2.0))))


def _silu(x):
    return x * jax.nn.sigmoid(x)


def _ln_rows(x):
    mu = jnp.mean(x, axis=-1, keepdims=True)
    xc = x - mu
    var = jnp.mean(xc * xc, axis=-1, keepdims=True)
    return xc * lax.rsqrt(var + LN_EPS)


def _rms_rows(x):
    return x * lax.rsqrt(jnp.mean(x * x, axis=-1, keepdims=True) + LN_EPS)


def _mod_kernel(ccol_ref, w_ref, b_ref, o_ref, sb_ref, *, tn):
    d = w_ref.shape[0]

    @pl.when(pl.program_id(0) == 0)
    def _():
        s = _silu(ccol_ref[...])
        sb_ref[0] = jnp.broadcast_to(s[:, 0:1], (d, LANES))
        sb_ref[1] = jnp.broadcast_to(s[:, 1:2], (d, LANES))

    nl = tn // LANES

    def body(kc, accs):
        k0 = pl.multiple_of(kc * SUBLANES, SUBLANES)
        s0 = sb_ref[0, pl.ds(k0, SUBLANES), :]
        s1 = sb_ref[1, pl.ds(k0, SUBLANES), :]
        w = w_ref[pl.ds(k0, SUBLANES), :]
        out = []
        for j in range(nl):
            wj = w[:, j * LANES:(j + 1) * LANES]
            out.append(accs[2 * j] + wj * s0)
            out.append(accs[2 * j + 1] + wj * s1)
        return tuple(out)

    zero = jnp.zeros((SUBLANES, LANES), F32)
    accs = lax.fori_loop(0, d // SUBLANES, body, (zero,) * (2 * nl), unroll=8)
    r0 = jnp.concatenate([jnp.sum(accs[2 * j], axis=0, keepdims=True) for j in range(nl)], axis=1)
    r1 = jnp.concatenate([jnp.sum(accs[2 * j + 1], axis=0, keepdims=True) for j in range(nl)], axis=1)
    row = lax.broadcasted_iota(jnp.int32, (SUBLANES, tn), 0)
    b = b_ref[...]
    o_ref[...] = jnp.where(row == 0, r0 + b, jnp.where(row == 1, r1 + b, 0.0))


def _modulation(ccol, w_ada, b_ada):
    d, n = w_ada.shape
    tn = 512
    return pl.pallas_call(
        functools.partial(_mod_kernel, tn=tn),
        grid=(n // tn,),
        in_specs=[
            pl.BlockSpec((d, 2), lambda j: (0, 0)),
            pl.BlockSpec((d, tn), lambda j: (0, j)),
            pl.BlockSpec((1, tn), lambda j: (0, j)),
        ],
        out_specs=pl.BlockSpec((SUBLANES, tn), lambda j: (0, j)),
        out_shape=jax.ShapeDtypeStruct((SUBLANES, n), F32),
        scratch_shapes=[pltpu.VMEM((2, d, LANES), F32)],
        compiler_params=_cparams(("arbitrary",)),
        name="modulation",
    )(ccol, w_ada, b_ada)


def _ln_mod_kernel(x_ref, sc_ref, sh_ref, o_ref):
    xn = _ln_rows(x_ref[...])
    o_ref[...] = (xn * (1.0 + sc_ref[...]) + sh_ref[...]).astype(o_ref.dtype)


def _ln_mod(x, sc, sh, tr):
    m, d = x.shape
    return pl.pallas_call(
        _ln_mod_kernel,
        grid=(m // tr,),
        in_specs=[
            pl.BlockSpec((tr, d), lambda i: (i, 0)),
            pl.BlockSpec((1, d), lambda i: (0, 0)),
            pl.BlockSpec((1, d), lambda i: (0, 0)),
        ],
        out_specs=pl.BlockSpec((tr, d), lambda i: (i, 0)),
        out_shape=jax.ShapeDtypeStruct((m, d), BF16),
        compiler_params=_cparams(("parallel",)),
        name="ln_mod",
    )(x, sc, sh)


def _mm_kernel(x_ref, w_ref, o_ref, *, n_gelu, n_scale, scale):
    j = pl.program_id(0)
    acc = jnp.dot(x_ref[...], w_ref[...], preferred_element_type=F32)
    if n_gelu == 0 and n_scale == 0:
        o_ref[...] = acc.astype(o_ref.dtype)
        return

    @pl.when(j < n_gelu)
    def _():
        o_ref[...] = _gelu(acc).astype(o_ref.dtype)

    @pl.when((j >= n_gelu) & (j < n_gelu + n_scale))
    def _():
        o_ref[...] = (acc * scale).astype(o_ref.dtype)

    @pl.when(j >= n_gelu + n_scale)
    def _():
        o_ref[...] = acc.astype(o_ref.dtype)


def _matmul(x, w, *, bm, bn, n_cols, col_off=0, n_gelu=0, n_scale=0, scale=1.0, out_dtype=BF16, name):
    m, k = x.shape
    return pl.pallas_call(
        functools.partial(_mm_kernel, n_gelu=n_gelu, n_scale=n_scale, scale=scale),
        grid=(n_cols // bn, m // bm),
        in_specs=[
            pl.BlockSpec((bm, k), lambda j, i: (i, 0)),
            pl.BlockSpec((k, bn), lambda j, i: (0, j + col_off)),
        ],
        out_specs=pl.BlockSpec((bm, bn), lambda j, i: (i, j)),
        out_shape=jax.ShapeDtypeStruct((m, n_cols), out_dtype),
        compiler_params=_cparams(("arbitrary", "arbitrary")),
        name=name,
    )(x, w)


def _swiglu_up_kernel(x_ref, wg_ref, wu_ref, o_ref, *, n_valid, bn):
    x = x_ref[...]
    a = jnp.dot(x, wg_ref[...], preferred_element_type=F32)
    b = jnp.dot(x, wu_ref[...], preferred_element_type=F32)
    g = _silu(a) * b
    col = pl.program_id(0) * bn + lax.broadcasted_iota(jnp.int32, g.shape, 1)
    o_ref[...] = jnp.where(col < n_valid, g, 0.0).astype(o_ref.dtype)


def _swiglu_up(x, wg, wu, *, bm, bn, n_valid):
    m, k = x.shape
    n = wg.shape[1]
    return pl.pallas_call(
        functools.partial(_swiglu_up_kernel, n_valid=n_valid, bn=bn),
        grid=(n // bn, m // bm),
        in_specs=[
            pl.BlockSpec((bm, k), lambda j, i: (i, 0)),
            pl.BlockSpec((k, bn), lambda j, i: (0, j)),
            pl.BlockSpec((k, bn), lambda j, i: (0, j)),
        ],
        out_specs=pl.BlockSpec((bm, bn), lambda j, i: (i, j)),
        out_shape=jax.ShapeDtypeStruct((m, n), BF16),
        compiler_params=_cparams(("arbitrary", "arbitrary")),
        name="swiglu_up",
    )(x, wg, wu)


def _down_kernel(g_ref, w_ref, o_ref):
    @pl.when(pl.program_id(1) == 0)
    def _():
        o_ref[...] = jnp.zeros_like(o_ref)

    o_ref[...] += jnp.dot(g_ref[...], w_ref[...], preferred_element_type=F32)


def _down(g, w, *, bm, bk):
    m, k = g.shape
    n = w.shape[1]
    return pl.pallas_call(
        _down_kernel,
        grid=(m // bm, k // bk),
        in_specs=[
            pl.BlockSpec((bm, bk), lambda i, kk: (i, kk)),
            pl.BlockSpec((bk, n), lambda i, kk: (kk, 0)),
        ],
        out_specs=pl.BlockSpec((bm, n), lambda i, kk: (i, 0)),
        out_shape=jax.ShapeDtypeStruct((m, n), F32),
        compiler_params=_cparams(("parallel", "arbitrary")),
        name="swiglu_down",
    )(g, w)


N_PAIR_ROWS = 2 * WIN_H - 2


def _bias_kernel(rpb_ref, o_ref):
    h = pl.program_id(0)
    n_dcol = 2 * WIN_W - 1
    q = lax.broadcasted_iota(jnp.int32, (GRID_W, 2 * GRID_W), 0)
    l = lax.broadcasted_iota(jnp.int32, (GRID_W, 2 * GRID_W), 1)
    kc = l & (GRID_W - 1)
    hi = l >= GRID_W
    dcol = jnp.clip(kc - q, -(WIN_W - 1), WIN_W - 1) + (WIN_W - 1)
    cs = jnp.clip(q - WIN_W // 2, 0, GRID_W - WIN_W)
    inside = (kc >= cs) & (kc < cs + WIN_W)
    base = h * ((2 * WIN_H - 1) * n_dcol)

    def body(dra, carry):
        acc = jnp.zeros((GRID_W, 2 * GRID_W), F32)
        for d in range(n_dcol):
            lo_v = rpb_ref[base + dra * n_dcol + d]
            hi_v = rpb_ref[base + (dra + 1) * n_dcol + d]
            acc = jnp.where(dcol == d, jnp.where(hi, hi_v, lo_v), acc)
        o_ref[dra] = jnp.where(inside, acc, -jnp.inf)
        return carry

    lax.fori_loop(0, N_PAIR_ROWS, body, 0)


def _bias_table(rpb_flat, n_heads):
    return pl.pallas_call(
        _bias_kernel,
        grid=(n_heads,),
        in_specs=[pl.BlockSpec(memory_space=pltpu.SMEM)],
        out_specs=pl.BlockSpec((None, N_PAIR_ROWS, GRID_W, 2 * GRID_W), lambda h: (h, 0, 0, 0)),
        out_shape=jax.ShapeDtypeStruct((n_heads, N_PAIR_ROWS, GRID_W, 2 * GRID_W), F32),
        compiler_params=_cparams(("arbitrary",)),
        name="bias_table",
    )(rpb_flat)


ROWS_PER_STEP = 8


def _attn_kernel(q_ref, k_ref, v_ref, kc_ref, vc_ref, tp_ref, o_ref, *, n_rows):
    jb = pl.program_id(1)
    n_lat = WIN_H * GRID_W
    q = q_ref[...]
    kc = kc_ref[...]
    vc = vc_ref[...]
    dn_t = (((1,), (1,)), ((), ()))
    s_ctx = lax.dot_general(q, kc, dn_t, preferred_element_type=F32)
    for t in range(ROWS_PER_STEP):
        r = jb * ROWS_PER_STEP + t
        ks = jnp.clip(r - WIN_H // 2, 0, n_rows - WIN_H)
        shift = r - ks
        start = pl.multiple_of(ks * GRID_W, GRID_W)
        kw = k_ref[pl.ds(start, n_lat), :]
        vw = v_ref[pl.ds(start, n_lat), :]
        qt = q[t * GRID_W:(t + 1) * GRID_W]
        s = lax.dot_general(qt, kw, dn_t, preferred_element_type=F32)
        bias = jnp.concatenate(
            [tp_ref[2 * p - shift + (WIN_H - 1)] for p in range(WIN_H // 2)], axis=1)
        s = s + bias
        sc = s_ctx[t * GRID_W:(t + 1) * GRID_W]
        m = jnp.maximum(jnp.max(s, axis=1, keepdims=True), jnp.max(sc, axis=1, keepdims=True))
        e = jnp.exp(s - m)
        ec = jnp.exp(sc - m)
        den = jnp.sum(e, axis=1, keepdims=True) + jnp.sum(ec, axis=1, keepdims=True)
        o = jnp.dot(e.astype(BF16), vw, preferred_element_type=F32)
        o = o + jnp.dot(ec.astype(BF16), vc, preferred_element_type=F32)
        o_ref[t * GRID_W:(t + 1) * GRID_W, :] = (o / den).astype(o_ref.dtype)


def _attention(p, kvc, tp, *, n_heads, q_blk0, k_blk0, v_blk0):
    s = p.shape[0]
    n_ctx = kvc.shape[0]
    n_rows = s // GRID_W
    tq = ROWS_PER_STEP * GRID_W
    return pl.pallas_call(
        functools.partial(_attn_kernel, n_rows=n_rows),
        grid=(n_heads, n_rows // ROWS_PER_STEP),
        in_specs=[
            pl.BlockSpec((tq, HEAD_DIM), lambda h, j: (j, q_blk0 + h)),
            pl.BlockSpec((s, HEAD_DIM), lambda h, j: (0, k_blk0 + h)),
            pl.BlockSpec((s, HEAD_DIM), lambda h, j: (0, v_blk0 + h)),
            pl.BlockSpec((n_ctx, HEAD_DIM), lambda h, j: (0, h)),
            pl.BlockSpec((n_ctx, HEAD_DIM), lambda h, j: (0, n_heads + h)),
            pl.BlockSpec((None, N_PAIR_ROWS, GRID_W, 2 * GRID_W), lambda h, j: (h, 0, 0, 0)),
        ],
        out_specs=pl.BlockSpec((tq, HEAD_DIM), lambda h, j: (j, h)),
        out_shape=jax.ShapeDtypeStruct((s, n_heads * HEAD_DIM), BF16),
        compiler_params=_cparams(("arbitrary", "arbitrary")),
        name="nbr_attention",
    )(p, p, p, kvc, kvc, tp)


SGU_CHUNKS_PER_STEP = 4


def _sgu_merge_kernel(gu_ref, gv_ref, ob_ref, ws_ref, bst_ref, lg_ref, lb_ref, gn_ref, o_ref, oa_ref,
                      *, n_heads, w_sgu):
    v = _ln_rows(gv_ref[...].astype(F32)) * lg_ref[...] + lb_ref[...]
    v = v.astype(BF16)
    bst = bst_ref[...]
    for h in range(n_heads):
        cols = slice(h * HEAD_DIM, (h + 1) * HEAD_DIM)
        vh = jnp.concatenate(
            [v[c * CHUNK:(c + 1) * CHUNK, cols] for c in range(SGU_CHUNKS_PER_STEP)], axis=1)
        mixed = jnp.dot(ws_ref[h], vh, preferred_element_type=F32) + bst[:, h:h + 1]
        for c in range(SGU_CHUNKS_PER_STEP):
            rows = slice(c * CHUNK, (c + 1) * CHUNK)
            oa_ref[rows, cols] = gu_ref[rows, cols].astype(F32) * mixed[:, c * HEAD_DIM:(c + 1) * HEAD_DIM]
    gn = gn_ref[...]
    o_ref[:, :w_sgu] = (_rms_rows(oa_ref[...]) * gn[:, :w_sgu]).astype(o_ref.dtype)
    o_ref[:, w_sgu:] = (_rms_rows(ob_ref[...].astype(F32)) * gn[:, w_sgu:]).astype(o_ref.dtype)


def _sgu_merge(p, ob, ws, bst, lg, lb, gn, *, n_heads):
    s = p.shape[0]
    w_sgu = n_heads * HEAD_DIM
    w_na = ob.shape[1]
    tr = SGU_CHUNKS_PER_STEP * CHUNK
    return pl.pallas_call(
        functools.partial(_sgu_merge_kernel, n_heads=n_heads, w_sgu=w_sgu),
        grid=(s // tr,),
        in_specs=[
            pl.BlockSpec((tr, w_sgu), lambda i: (i, 0)),
            pl.BlockSpec((tr, w_sgu), lambda i: (i, 1)),
            pl.BlockSpec((tr, w_na), lambda i: (i, 0)),
            pl.BlockSpec((n_heads, CHUNK, CHUNK), lambda i: (0, 0, 0)),
            pl.BlockSpec((CHUNK, n_heads), lambda i: (0, 0)),
            pl.BlockSpec((1, w_sgu), lambda i: (0, 0)),
            pl.BlockSpec((1, w_sgu), lambda i: (0, 0)),
            pl.BlockSpec((1, w_sgu + w_na), lambda i: (0, 0)),
        ],
        out_specs=pl.BlockSpec((tr, w_sgu + w_na), lambda i: (i, 0)),
        out_shape=jax.ShapeDtypeStruct((s, w_sgu + w_na), BF16),
        scratch_shapes=[pltpu.VMEM((tr, w_sgu), F32)],
        compiler_params=_cparams(("parallel",)),
        name="sgu_merge",
    )(p, p, ob, ws, bst, lg, lb, gn)


def _res_ln_mod_kernel(x_ref, y_ref, g_ref, lg_ref, lb_ref, sc_ref, sh_ref, x1_ref, h_ref):
    xn = _ln_rows(x_ref[...])
    z = DEEPNORM_ALPHA * xn + g_ref[...] * y_ref[...].astype(F32)
    x1 = _ln_rows(z) * lg_ref[...] + lb_ref[...]
    x1_ref[...] = x1
    h_ref[...] = (x1 * (1.0 + sc_ref[...]) + sh_ref[...]).astype(h_ref.dtype)


def _res_ln_mod(x, y, gate, lg, lb, sc, sh, tr):
    m, d = x.shape
    row = pl.BlockSpec((tr, d), lambda i: (i, 0))
    vec = pl.BlockSpec((1, d), lambda i: (0, 0))
    return pl.pallas_call(
        _res_ln_mod_kernel,
        grid=(m // tr,),
        in_specs=[row, row, vec, vec, vec, vec, vec],
        out_specs=[row, row],
        out_shape=[jax.ShapeDtypeStruct((m, d), F32), jax.ShapeDtypeStruct((m, d), BF16)],
        compiler_params=_cparams(("parallel",)),
        name="res_ln_mod",
    )(x, y, gate, lg, lb, sc, sh)


def _res_ln_kernel(x_ref, f_ref, g_ref, lg_ref, lb_ref, o_ref):
    z = DEEPNORM_ALPHA * x_ref[...] + g_ref[...] * f_ref[...].astype(F32)
    o_ref[...] = _ln_rows(z) * lg_ref[...] + lb_ref[...]


def _res_ln(x, f, gate, lg, lb, tr):
    m, d = x.shape
    row = pl.BlockSpec((tr, d), lambda i: (i, 0))
    vec = pl.BlockSpec((1, d), lambda i: (0, 0))
    return pl.pallas_call(
        _res_ln_kernel,
        grid=(m // tr,),
        in_specs=[row, row, vec, vec, vec],
        out_specs=row,
        out_shape=jax.ShapeDtypeStruct((m, d), F32),
        compiler_params=_cparams(("parallel",)),
        name="res_ln",
    )(x, f, gate, lg, lb)


def kernel(x, c, ctx, c_ctx, w_ada, b_ada, w_in, w_s, b_s, sgu_g, sgu_b, rpb, gn_g, w_o, ln_g, ln_b,
           w_gate, w_up, w_down):
    batch, seq, d = x.shape
    assert batch == 1 and w_ada.shape[0] == DEPTH
    n_sgu_heads = w_s.shape[1]
    n_na_heads = rpb.shape[1]
    w_sgu = n_sgu_heads * HEAD_DIM
    w_na = n_na_heads * HEAD_DIM
    d_ff = w_gate.shape[2]
    assert w_in.shape[2] == 2 * w_sgu + 3 * w_na and w_sgu == w_na

    x2 = x.reshape(seq, d)
    ctx2 = ctx.reshape(ctx.shape[1], d)

    ccol = jnp.stack([c.reshape(d), c_ctx], axis=1)
    mod = _modulation(ccol, w_ada[0], b_ada[0].reshape(1, 6 * d))
    sh1, sc1, g1, sh2, sc2, g2 = [mod[0:1, i * d:(i + 1) * d] for i in range(6)]
    csh1, csc1 = mod[1:2, 0:d], mod[1:2, d:2 * d]

    h = _ln_mod(x2, sc1, sh1, 256)
    hc = _ln_mod(ctx2, csc1, csh1, 256)

    bn = 1024
    w_in_b = w_in[0].astype(BF16)
    p = _matmul(h, w_in_b, bm=1024, bn=bn, n_cols=w_in.shape[2], n_gelu=2 * w_sgu // bn,
                n_scale=w_na // bn, scale=HEAD_DIM ** -0.5, name="in_proj")
    kv_off = 2 * w_sgu + w_na
    kvc = _matmul(hc, w_in_b, bm=ctx2.shape[0], bn=bn, n_cols=2 * w_na, col_off=kv_off // bn,
                  name="ctx_kv_proj")

    tp = _bias_table(rpb[0].reshape(-1), n_na_heads)
    ob = _attention(p, kvc, tp, n_heads=n_na_heads, q_blk0=2 * w_sgu // HEAD_DIM,
                    k_blk0=kv_off // HEAD_DIM, v_blk0=(kv_off + w_na) // HEAD_DIM)

    merged = _sgu_merge(p, ob, w_s[0].astype(BF16), b_s[0].T, sgu_g, sgu_b, gn_g, n_heads=n_sgu_heads)

    y = _matmul(merged, w_o[0].astype(BF16), bm=1024, bn=1024, n_cols=d, name="out_proj")
    x1, h2 = _res_ln_mod(x2, y, g1, ln_g[0, 0:1], ln_b[0, 0:1], sc2, sh2, 256)

    bn_ff = 512
    d_ffp = -(-d_ff // 1024) * 1024
    pad = d_ffp - d_ff
    wg = jnp.pad(w_gate[0].astype(BF16), ((0, 0), (0, pad)))
    wu = jnp.pad(w_up[0].astype(BF16), ((0, 0), (0, pad)))
    wd = jnp.pad(w_down[0].astype(BF16), ((0, pad), (0, 0)))
    g = _swiglu_up(h2, wg, wu, bm=1024, bn=bn_ff, n_valid=d_ff)
    f = _down(g, wd, bm=512, bk=1024)

    out = _res_ln(x1, f, g2, ln_g[0, 1:2], ln_b[0, 1:2], 256)
    return out.reshape(batch, seq, d)
```

```python
import functools
import math

import jax
import jax.numpy as jnp
from jax import lax
from jax.experimental import pallas as pl
from jax.experimental.pallas import tpu as pltpu

F32 = jnp.float32
BF16 = jnp.bfloat16

GRID_W = 64
HEAD_DIM = 128
CHUNK = 128
WIN_H = 8
WIN_W = 16
LN_EPS = 1e-5
DEPTH = 1
DEEPNORM_ALPHA = (2.0 * DEPTH) ** 0.25

LANES = 128
SUBLANES = 8
VMEM_LIMIT = 56 * 1024 * 1024


def _cparams(sem):
    return pltpu.CompilerParams(dimension_semantics=sem, vmem_limit_bytes=VMEM_LIMIT)


def _gelu(x):
    return 0.5 * x * (1.0 + lax.erf(x * (1.0 / math.sqrt(2.0))))


def _silu(x):
    return x * jax.nn.sigmoid(x)


def _ln_rows(x):
    mu = jnp.mean(x, axis=-1, keepdims=True)
    xc = x - mu
    var = jnp.mean(xc * xc, axis=-1, keepdims=True)
    return xc * lax.rsqrt(var + LN_EPS)


def _rms_rows(x):
    return x * lax.rsqrt(jnp.mean(x * x, axis=-1, keepdims=True) + LN_EPS)


def _mod_kernel(ccol_ref, w_ref, b_ref, o_ref, sb_ref, *, tn):
    d = w_ref.shape[0]

    @pl.when(pl.program_id(0) == 0)
    def _():
        s = _silu(ccol_ref[...])
        sb_ref[0] = jnp.broadcast_to(s[:, 0:1], (d, LANES))
        sb_ref[1] = jnp.broadcast_to(s[:, 1:2], (d, LANES))

    nl = tn // LANES

    def body(kc, accs):
        k0 = pl.multiple_of(kc * SUBLANES, SUBLANES)
        s0 = sb_ref[0, pl.ds(k0, SUBLANES), :]
        s1 = sb_ref[1, pl.ds(k0, SUBLANES), :]
        w = w_ref[pl.ds(k0, SUBLANES), :]
        out = []
        for j in range(nl):
            wj = w[:, j * LANES:(j + 1) * LANES]
            out.append(accs[2 * j] + wj * s0)
            out.append(accs[2 * j + 1] + wj * s1)
        return tuple(out)

    zero = jnp.zeros((SUBLANES, LANES), F32)
    accs = lax.fori_loop(0, d // SUBLANES, body, (zero,) * (2 * nl), unroll=8)
    r0 = jnp.concatenate([jnp.sum(accs[2 * j], axis=0, keepdims=True) for j in range(nl)], axis=1)
    r1 = jnp.concatenate([jnp.sum(accs[2 * j + 1], axis=0, keepdims=True) for j in range(nl)], axis=1)
    row = lax.broadcasted_iota(jnp.int32, (SUBLANES, tn), 0)
    b = b_ref[...]
    o_ref[...] = jnp.where(row == 0, r0 + b, jnp.where(row == 1, r1 + b, 0.0))


def _modulation(ccol, w_ada, b_ada):
    d, n = w_ada.shape
    tn = 512
    return pl.pallas_call(
        functools.partial(_mod_kernel, tn=tn),
        grid=(n // tn,),
        in_specs=[
            pl.BlockSpec((d, 2), lambda j: (0, 0)),
            pl.BlockSpec((d, tn), lambda j: (0, j)),
            pl.BlockSpec((1, tn), lambda j: (0, j)),
        ],
        out_specs=pl.BlockSpec((SUBLANES, tn), lambda j: (0, j)),
        out_shape=jax.ShapeDtypeStruct((SUBLANES, n), F32),
        scratch_shapes=[pltpu.VMEM((2, d, LANES), F32)],
        compiler_params=_cparams(("arbitrary",)),
        name="modulation",
    )(ccol, w_ada, b_ada)


def _ln_mod_kernel(x_ref, sc_ref, sh_ref, o_ref):
    xn = _ln_rows(x_ref[...])
    o_ref[...] = (xn * (1.0 + sc_ref[...]) + sh_ref[...]).astype(o_ref.dtype)


def _ln_mod(x, sc, sh, tr):
    m, d = x.shape
    return pl.pallas_call(
        _ln_mod_kernel,
        grid=(m // tr,),
        in_specs=[
            pl.BlockSpec((tr, d), lambda i: (i, 0)),
            pl.BlockSpec((1, d), lambda i: (0, 0)),
            pl.BlockSpec((1, d), lambda i: (0, 0)),
        ],
        out_specs=pl.BlockSpec((tr, d), lambda i: (i, 0)),
        out_shape=jax.ShapeDtypeStruct((m, d), BF16),
        compiler_params=_cparams(("parallel",)),
        name="ln_mod",
    )(x, sc, sh)


CAST_ROWS = 256
SIDE_ROWS = 64


def _mm_cast_kernel(*refs, n_w, bn, n_blocks, m_blocks, col0, n_valid, epilogue, n_gelu, n_scale, scale,
                    side_blocks):
    it = iter(refs)
    x_ref = next(it)
    w_hbm = [next(it) for _ in range(n_w)]
    side_in = next(it) if side_blocks else None
    o_ref = next(it)
    side_out = next(it) if side_blocks else None
    stage = [next(it) for _ in range(n_w)]
    wb = [next(it) for _ in range(n_w)]
    sem = next(it)

    j = pl.program_id(0)
    i = pl.program_id(1)
    k = stage[0].shape[0]
    w_last = n_valid - (n_blocks - 1) * bn

    def slab_copies(jj, width):
        return [pltpu.make_async_copy(w_hbm[a].at[:, pl.ds(col0 + jj * bn, width)],
                                      stage[a].at[:, pl.ds(0, width)], sem.at[a]) for a in range(n_w)]

    def for_slab(jj, fn):
        if w_last == bn:
            for cp in slab_copies(jj, bn):
                fn(cp)
            return

        @pl.when(jj < n_blocks - 1)
        def _():
            for cp in slab_copies(jj, bn):
                fn(cp)

        @pl.when(jj == n_blocks - 1)
        def _():
            for cp in slab_copies(n_blocks - 1, w_last):
                fn(cp)

    @pl.when((j == 0) & (i == 0))
    def _():
        for_slab(j, lambda cp: cp.start())

    @pl.when(i == 0)
    def _():
        for_slab(j, lambda cp: cp.wait())

        def cast_rows(r, carry):
            r0 = pl.multiple_of(r * CAST_ROWS, CAST_ROWS)
            for a in range(n_w):
                wb[a][pl.ds(r0, CAST_ROWS), :] = stage[a][pl.ds(r0, CAST_ROWS), :].astype(BF16)
            return carry

        lax.fori_loop(0, k // CAST_ROWS, cast_rows, 0)

    @pl.when((i == min(1, m_blocks - 1)) & (j + 1 < n_blocks))
    def _():
        for_slab(j + 1, lambda cp: cp.start())

    x = x_ref[...]
    accs = [jnp.dot(x, wb[a][...], preferred_element_type=F32) for a in range(n_w)]

    if epilogue == "plain":
        o_ref[...] = accs[0].astype(o_ref.dtype)
    elif epilogue == "swiglu":
        g = _silu(accs[0]) * accs[1]
        col = j * bn + lax.broadcasted_iota(jnp.int32, g.shape, 1)
        o_ref[...] = jnp.where(col < n_valid, g, 0.0).astype(o_ref.dtype)
    else:
        acc = accs[0]

        @pl.when(j < n_gelu)
        def _():
            o_ref[...] = _gelu(acc).astype(o_ref.dtype)

        @pl.when((j >= n_gelu) & (j < n_gelu + n_scale))
        def _():
            o_ref[...] = (acc * scale).astype(o_ref.dtype)

        @pl.when(j >= n_gelu + n_scale)
        def _():
            o_ref[...] = acc.astype(o_ref.dtype)

    if side_blocks:
        t = j * m_blocks + i
        side_out[...] = jnp.where(t < side_blocks, side_in[...], 0.0).astype(side_out.dtype)


def _mm_cast(x, ws, *, bm, bn, n_cols, col0=0, n_valid=None, epilogue="plain", n_gelu=0, n_scale=0,
             scale=1.0, side=None, name):
    m, k = x.shape
    n_valid = n_cols if n_valid is None else n_valid
    n_w = len(ws)
    n_blocks = n_cols // bn
    m_blocks = m // bm
    in_specs = [pl.BlockSpec((bm, k), lambda j, i: (i, 0))]
    in_specs += [pl.BlockSpec(memory_space=pl.ANY)] * n_w
    out_specs = [pl.BlockSpec((bm, bn), lambda j, i: (i, j))]
    out_shape = [jax.ShapeDtypeStruct((m, n_cols), BF16)]
    args = [x, *ws]
    side_blocks = 0
    if side is not None:
        w_side, padded_rows = side
        side_blocks = w_side.shape[0] // SIDE_ROWS
        assert padded_rows == n_blocks * m_blocks * SIDE_ROWS and side_blocks * SIDE_ROWS == w_side.shape[0]
        c = w_side.shape[1]
        in_specs.append(pl.BlockSpec(
            (SIDE_ROWS, c), lambda j, i: (jnp.minimum(j * m_blocks + i, side_blocks - 1), 0)))
        out_specs.append(pl.BlockSpec((SIDE_ROWS, c), lambda j, i: (j * m_blocks + i, 0)))
        out_shape.append(jax.ShapeDtypeStruct((padded_rows, c), BF16))
        args.append(w_side)
    kern = functools.partial(
        _mm_cast_kernel, n_w=n_w, bn=bn, n_blocks=n_blocks, m_blocks=m_blocks, col0=col0, n_valid=n_valid,
        epilogue=epilogue, n_gelu=n_gelu, n_scale=n_scale, scale=scale, side_blocks=side_blocks)
    out = pl.pallas_call(
        kern,
        grid=(n_blocks, m_blocks),
        in_specs=in_specs,
        out_specs=out_specs,
        out_shape=out_shape,
        scratch_shapes=([pltpu.VMEM((k, bn), F32)] * n_w + [pltpu.VMEM((k, bn), BF16)] * n_w
                        + [pltpu.SemaphoreType.DMA((n_w,))]),
        compiler_params=_cparams(("arbitrary", "arbitrary")),
        name=name,
    )(*args)
    return out if side is not None else out[0]


def _down_kernel(g_ref, w_ref, o_ref):
    @pl.when(pl.program_id(1) == 0)
    def _():
        o_ref[...] = jnp.zeros_like(o_ref)

    o_ref[...] += jnp.dot(g_ref[...], w_ref[...], preferred_element_type=F32)


def _down(g, w, *, bm, bk):
    m, k = g.shape
    n = w.shape[1]
    return pl.pallas_call(
        _down_kernel,
        grid=(m // bm, k // bk),
        in_specs=[
            pl.BlockSpec((bm, bk), lambda i, kk: (i, kk)),
            pl.BlockSpec((bk, n), lambda i, kk: (kk, 0)),
        ],
        out_specs=pl.BlockSpec((bm, n), lambda i, kk: (i, 0)),
        out_shape=jax.ShapeDtypeStruct((m, n), F32),
        compiler_params=_cparams(("parallel", "arbitrary")),
        name="swiglu_down",
    )(g, w)


N_PAIR_ROWS = 2 * WIN_H - 2


def _bias_kernel(rpb_ref, o_ref):
    h = pl.program_id(0)
    n_dcol = 2 * WIN_W - 1
    q = lax.broadcasted_iota(jnp.int32, (GRID_W, 2 * GRID_W), 0)
    l = lax.broadcasted_iota(jnp.int32, (GRID_W, 2 * GRID_W), 1)
    kc = l & (GRID_W - 1)
    hi = l >= GRID_W
    dcol = jnp.clip(kc - q, -(WIN_W - 1), WIN_W - 1) + (WIN_W - 1)
    cs = jnp.clip(q - WIN_W // 2, 0, GRID_W - WIN_W)
    inside = (kc >= cs) & (kc < cs + WIN_W)
    base = h * ((2 * WIN_H - 1) * n_dcol)

    def body(dra, carry):
        acc = jnp.zeros((GRID_W, 2 * GRID_W), F32)
        for d in range(n_dcol):
            lo_v = rpb_ref[base + dra * n_dcol + d]
            hi_v = rpb_ref[base + (dra + 1) * n_dcol + d]
            acc = jnp.where(dcol == d, jnp.where(hi, hi_v, lo_v), acc)
        o_ref[dra] = jnp.where(inside, acc, -jnp.inf)
        return carry

    lax.fori_loop(0, N_PAIR_ROWS, body, 0)


def _bias_table(rpb_flat, n_heads):
    return pl.pallas_call(
        _bias_kernel,
        grid=(n_heads,),
        in_specs=[pl.BlockSpec(memory_space=pltpu.SMEM)],
        out_specs=pl.BlockSpec((None, N_PAIR_ROWS, GRID_W, 2 * GRID_W), lambda h: (h, 0, 0, 0)),
        out_shape=jax.ShapeDtypeStruct((n_heads, N_PAIR_ROWS, GRID_W, 2 * GRID_W), F32),
        compiler_params=_cparams(("arbitrary",)),
        name="bias_table",
    )(rpb_flat)


ROWS_PER_STEP = 32


def _attn_kernel(q_ref, k_ref, v_ref, kc_ref, vc_ref, tp_ref, o_ref, *, n_rows):
    jb = pl.program_id(1)
    n_lat = WIN_H * GRID_W
    q = q_ref[...]
    kc = kc_ref[...]
    vc = vc_ref[...]
    dn_t = (((1,), (1,)), ((), ()))
    s_ctx = lax.dot_general(q, kc, dn_t, preferred_element_type=F32)
    starts, s_rows = [], []
    for t in range(ROWS_PER_STEP):
        r = jb * ROWS_PER_STEP + t
        ks = jnp.clip(r - WIN_H // 2, 0, n_rows - WIN_H)
        shift = r - ks
        start = pl.multiple_of(ks * GRID_W, GRID_W)
        starts.append(start)
        kw = k_ref[pl.ds(start, n_lat), :]
        qt = q[t * GRID_W:(t + 1) * GRID_W]
        s = lax.dot_general(qt, kw, dn_t, preferred_element_type=F32)
        bias = jnp.concatenate(
            [tp_ref[2 * p - shift + (WIN_H - 1)] for p in range(WIN_H // 2)], axis=1)
        s_rows.append(s + bias)
    s_lat = jnp.concatenate(s_rows, axis=0)
    m = jnp.maximum(jnp.max(s_lat, axis=1, keepdims=True), jnp.max(s_ctx, axis=1, keepdims=True))
    e = jnp.exp(s_lat - m).astype(BF16)
    ec = jnp.exp(s_ctx - m).astype(BF16)
    ones = jnp.ones((n_lat, HEAD_DIM), BF16)
    vc_aug = jnp.concatenate([vc, ones[:vc.shape[0]]], axis=1)
    oc = jnp.dot(ec, vc_aug, preferred_element_type=F32)
    for t in range(ROWS_PER_STEP):
        rows = slice(t * GRID_W, (t + 1) * GRID_W)
        vw_aug = jnp.concatenate([v_ref[pl.ds(starts[t], n_lat), :], ones], axis=1)
        o = jnp.dot(e[rows], vw_aug, preferred_element_type=F32) + oc[rows]
        o_ref[rows, :] = (o[:, :HEAD_DIM] / o[:, HEAD_DIM:]).astype(o_ref.dtype)


def _attention(p, kvc, tp, *, n_heads, q_blk0, k_blk0, v_blk0):
    s = p.shape[0]
    n_ctx = kvc.shape[0]
    n_rows = s // GRID_W
    tq = ROWS_PER_STEP * GRID_W
    return pl.pallas_call(
        functools.partial(_attn_kernel, n_rows=n_rows),
        grid=(n_heads, n_rows // ROWS_PER_STEP),
        in_specs=[
            pl.BlockSpec((tq, HEAD_DIM), lambda h, j: (j, q_blk0 + h)),
            pl.BlockSpec((s, HEAD_DIM), lambda h, j: (0, k_blk0 + h)),
            pl.BlockSpec((s, HEAD_DIM), lambda h, j: (0, v_blk0 + h)),
            pl.BlockSpec((n_ctx, HEAD_DIM), lambda h, j: (0, h)),
            pl.BlockSpec((n_ctx, HEAD_DIM), lambda h, j: (0, n_heads + h)),
            pl.BlockSpec((None, N_PAIR_ROWS, GRID_W, 2 * GRID_W), lambda h, j: (h, 0, 0, 0)),
        ],
        out_specs=pl.BlockSpec((tq, HEAD_DIM), lambda h, j: (j, h)),
        out_shape=jax.ShapeDtypeStruct((s, n_heads * HEAD_DIM), BF16),
        compiler_params=_cparams(("arbitrary", "arbitrary")),
        name="nbr_attention",
    )(p, p, p, kvc, kvc, tp)


SGU_CHUNKS_PER_STEP = 4


def _sgu_merge_kernel(gu_ref, gv_ref, ob_ref, ws_ref, bst_ref, lg_ref, lb_ref, gn_ref, o_ref, oa_ref,
                      *, n_heads, w_sgu):
    v = _ln_rows(gv_ref[...].astype(F32)) * lg_ref[...] + lb_ref[...]
    v = v.astype(BF16)
    bst = bst_ref[...]
    for h in range(n_heads):
        cols = slice(h * HEAD_DIM, (h + 1) * HEAD_DIM)
        vh = jnp.concatenate(
            [v[c * CHUNK:(c + 1) * CHUNK, cols] for c in range(SGU_CHUNKS_PER_STEP)], axis=1)
        mixed = jnp.dot(ws_ref[h], vh, preferred_element_type=F32) + bst[:, h:h + 1]
        for c in range(SGU_CHUNKS_PER_STEP):
            rows = slice(c * CHUNK, (c + 1) * CHUNK)
            oa_ref[rows, cols] = gu_ref[rows, cols].astype(F32) * mixed[:, c * HEAD_DIM:(c + 1) * HEAD_DIM]
    gn = gn_ref[...]
    o_ref[:, :w_sgu] = (_rms_rows(oa_ref[...]) * gn[:, :w_sgu]).astype(o_ref.dtype)
    o_ref[:, w_sgu:] = (_rms_rows(ob_ref[...].astype(F32)) * gn[:, w_sgu:]).astype(o_ref.dtype)


def _sgu_merge(p, ob, ws, bst, lg, lb, gn, *, n_heads):
    s = p.shape[0]
    w_sgu = n_heads * HEAD_DIM
    w_na = ob.shape[1]
    tr = SGU_CHUNKS_PER_STEP * CHUNK
    return pl.pallas_call(
        functools.partial(_sgu_merge_kernel, n_heads=n_heads, w_sgu=w_sgu),
        grid=(s // tr,),
        in_specs=[
            pl.BlockSpec((tr, w_sgu), lambda i: (i, 0)),
            pl.BlockSpec((tr, w_sgu), lambda i: (i, 1)),
            pl.BlockSpec((tr, w_na), lambda i: (i, 0)),
            pl.BlockSpec((n_heads, CHUNK, CHUNK), lambda i: (0, 0, 0)),
            pl.BlockSpec((CHUNK, n_heads), lambda i: (0, 0)),
            pl.BlockSpec((1, w_sgu), lambda i: (0, 0)),
            pl.BlockSpec((1, w_sgu), lambda i: (0, 0)),
            pl.BlockSpec((1, w_sgu + w_na), lambda i: (0, 0)),
        ],
        out_specs=pl.BlockSpec((tr, w_sgu + w_na), lambda i: (i, 0)),
        out_shape=jax.ShapeDtypeStruct((s, w_sgu + w_na), BF16),
        scratch_shapes=[pltpu.VMEM((tr, w_sgu), F32)],
        compiler_params=_cparams(("parallel",)),
        name="sgu_merge",
    )(p, p, ob, ws, bst, lg, lb, gn)


def _res_ln_mod_kernel(x_ref, y_ref, g_ref, lg_ref, lb_ref, sc_ref, sh_ref, x1_ref, h_ref):
    xn = _ln_rows(x_ref[...])
    z = DEEPNORM_ALPHA * xn + g_ref[...] * y_ref[...].astype(F32)
    x1 = _ln_rows(z) * lg_ref[...] + lb_ref[...]
    x1_ref[...] = x1
    h_ref[...] = (x1 * (1.0 + sc_ref[...]) + sh_ref[...]).astype(h_ref.dtype)


def _res_ln_mod(x, y, gate, lg, lb, sc, sh, tr):
    m, d = x.shape
    row = pl.BlockSpec((tr, d), lambda i: (i, 0))
    vec = pl.BlockSpec((1, d), lambda i: (0, 0))
    return pl.pallas_call(
        _res_ln_mod_kernel,
        grid=(m // tr,),
        in_specs=[row, row, vec, vec, vec, vec, vec],
        out_specs=[row, row],
        out_shape=[jax.ShapeDtypeStruct((m, d), F32), jax.ShapeDtypeStruct((m, d), BF16)],
        compiler_params=_cparams(("parallel",)),
        name="res_ln_mod",
    )(x, y, gate, lg, lb, sc, sh)


def _res_ln_kernel(x_ref, f_ref, g_ref, lg_ref, lb_ref, o_ref):
    z = DEEPNORM_ALPHA * x_ref[...] + g_ref[...] * f_ref[...].astype(F32)
    o_ref[...] = _ln_rows(z) * lg_ref[...] + lb_ref[...]


def _res_ln(x, f, gate, lg, lb, tr):
    m, d = x.shape
    row = pl.BlockSpec((tr, d), lambda i: (i, 0))
    vec = pl.BlockSpec((1, d), lambda i: (0, 0))
    return pl.pallas_call(
        _res_ln_kernel,
        grid=(m // tr,),
        in_specs=[row, row, vec, vec, vec],
        out_specs=row,
        out_shape=jax.ShapeDtypeStruct((m, d), F32),
        compiler_params=_cparams(("parallel",)),
        name="res_ln",
    )(x, f, gate, lg, lb)


def kernel(x, c, ctx, c_ctx, w_ada, b_ada, w_in, w_s, b_s, sgu_g, sgu_b, rpb, gn_g, w_o, ln_g, ln_b,
           w_gate, w_up, w_down):
    batch, seq, d = x.shape
    assert batch == 1 and w_ada.shape[0] == DEPTH
    n_sgu_heads = w_s.shape[1]
    n_na_heads = rpb.shape[1]
    w_sgu = n_sgu_heads * HEAD_DIM
    w_na = n_na_heads * HEAD_DIM
    d_ff = w_gate.shape[2]
    assert w_in.shape[2] == 2 * w_sgu + 3 * w_na and w_sgu == w_na

    x2 = x.reshape(seq, d)
    ctx2 = ctx.reshape(ctx.shape[1], d)

    ccol = jnp.stack([c.reshape(d), c_ctx], axis=1)
    mod = _modulation(ccol, w_ada[0], b_ada[0].reshape(1, 6 * d))
    sh1, sc1, g1, sh2, sc2, g2 = [mod[0:1, i * d:(i + 1) * d] for i in range(6)]
    csh1, csc1 = mod[1:2, 0:d], mod[1:2, d:2 * d]

    h = _ln_mod(x2, sc1, sh1, 256)
    hc = _ln_mod(ctx2, csc1, csh1, 256)

    bn = 1024
    p = _mm_cast(h, [w_in[0]], bm=1024, bn=bn, n_cols=w_in.shape[2], epilogue="in_proj",
                 n_gelu=2 * w_sgu // bn, n_scale=w_na // bn, scale=HEAD_DIM ** -0.5, name="in_proj")
    kv_off = 2 * w_sgu + w_na
    kvc = _mm_cast(hc, [w_in[0]], bm=ctx2.shape[0], bn=bn, n_cols=2 * w_na, col0=kv_off,
                   name="ctx_kv_proj")

    tp = _bias_table(rpb[0].reshape(-1), n_na_heads)
    ob = _attention(p, kvc, tp, n_heads=n_na_heads, q_blk0=2 * w_sgu // HEAD_DIM,
                    k_blk0=kv_off // HEAD_DIM, v_blk0=(kv_off + w_na) // HEAD_DIM)

    merged = _sgu_merge(p, ob, w_s[0].astype(BF16), b_s[0].T, sgu_g, sgu_b, gn_g, n_heads=n_sgu_heads)

    y = _mm_cast(merged, [w_o[0]], bm=1024, bn=1024, n_cols=d, name="out_proj")
    x1, h2 = _res_ln_mod(x2, y, g1, ln_g[0, 0:1], ln_b[0, 0:1], sc2, sh2, 256)

    bn_ff = 512
    d_ffp = -(-d_ff // 1024) * 1024
    g, wd = _mm_cast(h2, [w_gate[0], w_up[0]], bm=1024, bn=bn_ff, n_cols=d_ffp, n_valid=d_ff,
                     epilogue="swiglu", side=(w_down[0], d_ffp), name="swiglu_up")
    f = _down(g, wd, bm=512, bk=1024)

    out = _res_ln(x1, f, g2, ln_g[0, 1:2], ln_b[0, 1:2], 256)
    return out.reshape(batch, seq, d)
```

```python
import functools
import math

import jax
import jax.numpy as jnp
from jax import lax
from jax.experimental import pallas as pl
from jax.experimental.pallas import tpu as pltpu

F32 = jnp.float32
BF16 = jnp.bfloat16

GRID_W = 64
HEAD_DIM = 128
CHUNK = 128
WIN_H = 8
WIN_W = 16
LN_EPS = 1e-5
DEPTH = 1
DEEPNORM_ALPHA = (2.0 * DEPTH) ** 0.25

LANES = 128
SUBLANES = 8
VMEM_LIMIT = 56 * 1024 * 1024


def _cparams(sem):
    return pltpu.CompilerParams(dimension_semantics=sem, vmem_limit_bytes=VMEM_LIMIT)


def _gelu(x):
    return 0.5 * x * (1.0 + lax.erf(x * (1.0 / math.sqrt(2.0))))


def _silu(x):
    return x * jax.nn.sigmoid(x)


def _ln_rows(x):
    mu = jnp.mean(x, axis=-1, keepdims=True)
    xc = x - mu
    var = jnp.mean(xc * xc, axis=-1, keepdims=True)
    return xc * lax.rsqrt(var + LN_EPS)


def _rms_rows(x):
    return x * lax.rsqrt(jnp.mean(x * x, axis=-1, keepdims=True) + LN_EPS)


def _mod_kernel(ccol_ref, w_ref, b_ref, o_ref, sb_ref, *, tn):
    d = w_ref.shape[0]

    @pl.when(pl.program_id(0) == 0)
    def _():
        s = _silu(ccol_ref[...])
        sb_ref[0] = jnp.broadcast_to(s[:, 0:1], (d, LANES))
        sb_ref[1] = jnp.broadcast_to(s[:, 1:2], (d, LANES))

    nl = tn // LANES

    def body(kc, accs):
        k0 = pl.multiple_of(kc * SUBLANES, SUBLANES)
        s0 = sb_ref[0, pl.ds(k0, SUBLANES), :]
        s1 = sb_ref[1, pl.ds(k0, SUBLANES), :]
        w = w_ref[pl.ds(k0, SUBLANES), :]
        out = []
        for j in range(nl):
            wj = w[:, j * LANES:(j + 1) * LANES]
            out.append(accs[2 * j] + wj * s0)
            out.append(accs[2 * j + 1] + wj * s1)
        return tuple(out)

    zero = jnp.zeros((SUBLANES, LANES), F32)
    accs = lax.fori_loop(0, d // SUBLANES, body, (zero,) * (2 * nl), unroll=8)
    r0 = jnp.concatenate([jnp.sum(accs[2 * j], axis=0, keepdims=True) for j in range(nl)], axis=1)
    r1 = jnp.concatenate([jnp.sum(accs[2 * j + 1], axis=0, keepdims=True) for j in range(nl)], axis=1)
    row = lax.broadcasted_iota(jnp.int32, (SUBLANES, tn), 0)
    b = b_ref[...]
    o_ref[...] = jnp.where(row == 0, r0 + b, jnp.where(row == 1, r1 + b, 0.0))


def _modulation(ccol, w_ada, b_ada):
    d, n = w_ada.shape
    tn = 512
    return pl.pallas_call(
        functools.partial(_mod_kernel, tn=tn),
        grid=(n // tn,),
        in_specs=[
            pl.BlockSpec((d, 2), lambda j: (0, 0)),
            pl.BlockSpec((d, tn), lambda j: (0, j)),
            pl.BlockSpec((1, tn), lambda j: (0, j)),
        ],
        out_specs=pl.BlockSpec((SUBLANES, tn), lambda j: (0, j)),
        out_shape=jax.ShapeDtypeStruct((SUBLANES, n), F32),
        scratch_shapes=[pltpu.VMEM((2, d, LANES), F32)],
        compiler_params=_cparams(("arbitrary",)),
        name="modulation",
    )(ccol, w_ada, b_ada)


def _ln_mod_kernel(x_ref, c_ref, mod_ref, o_ref, *, n_x_blocks):
    d = x_ref.shape[1]
    i = pl.program_id(0)

    def emit(src_ref, row):
        sh = mod_ref[row:row + 1, 0:d]
        sc = mod_ref[row:row + 1, d:2 * d]
        o_ref[...] = (_ln_rows(src_ref[...]) * (1.0 + sc) + sh).astype(o_ref.dtype)

    @pl.when(i < n_x_blocks)
    def _():
        emit(x_ref, 0)

    @pl.when(i >= n_x_blocks)
    def _():
        emit(c_ref, 1)


def _ln_mod(x, ctx, mod, tr):
    m, d = x.shape
    n_x_blocks = m // tr
    n_c_blocks = ctx.shape[0] // tr
    return pl.pallas_call(
        functools.partial(_ln_mod_kernel, n_x_blocks=n_x_blocks),
        grid=(n_x_blocks + n_c_blocks,),
        in_specs=[
            pl.BlockSpec((tr, d), lambda i: (jnp.minimum(i, n_x_blocks - 1), 0)),
            pl.BlockSpec((tr, d), lambda i: (jnp.maximum(i - n_x_blocks, 0), 0)),
            pl.BlockSpec((SUBLANES, 2 * d), lambda i: (0, 0)),
        ],
        out_specs=pl.BlockSpec((tr, d), lambda i: (i, 0)),
        out_shape=jax.ShapeDtypeStruct((m + ctx.shape[0], d), BF16),
        compiler_params=_cparams(("arbitrary",)),
        name="ln_mod",
    )(x, ctx, mod)


CAST_ROWS = 256
SIDE_ROWS = 64
MM_CHUNK = 256


def _mm_cast_kernel(*refs, n_w, bn, n_blocks, m_blocks, col0, n_valid, epilogue, scale, side_blocks):
    it = iter(refs)
    x_ref = next(it)
    w_hbm = [next(it) for _ in range(n_w)]
    side_in = next(it) if side_blocks else None
    o_ref = next(it)
    side_out = next(it) if side_blocks else None
    stage = [next(it) for _ in range(n_w)]
    wb = [next(it) for _ in range(n_w)]
    sem = next(it)

    j = pl.program_id(0)
    i = pl.program_id(1)
    k = stage[0].shape[0]
    w_last = n_valid - (n_blocks - 1) * bn

    def slab_copies(jj, width):
        return [pltpu.make_async_copy(w_hbm[a].at[:, pl.ds(col0 + jj * bn, width)],
                                      stage[a].at[:, pl.ds(0, width)], sem.at[a]) for a in range(n_w)]

    def for_slab(jj, fn):
        if w_last == bn:
            for cp in slab_copies(jj, bn):
                fn(cp)
            return

        @pl.when(jj < n_blocks - 1)
        def _():
            for cp in slab_copies(jj, bn):
                fn(cp)

        @pl.when(jj == n_blocks - 1)
        def _():
            for cp in slab_copies(n_blocks - 1, w_last):
                fn(cp)

    @pl.when((j == 0) & (i == 0))
    def _():
        for_slab(j, lambda cp: cp.start())

    @pl.when(i == 0)
    def _():
        for_slab(j, lambda cp: cp.wait())

        def cast_rows(r, carry):
            r0 = pl.multiple_of(r * CAST_ROWS, CAST_ROWS)
            for a in range(n_w):
                wb[a][pl.ds(r0, CAST_ROWS), :] = stage[a][pl.ds(r0, CAST_ROWS), :].astype(BF16)
            return carry

        lax.fori_loop(0, k // CAST_ROWS, cast_rows, 0)

    @pl.when((i == min(1, m_blocks - 1)) & (j + 1 < n_blocks))
    def _():
        for_slab(j + 1, lambda cp: cp.start())

    x = x_ref[...]
    for c in range(bn // MM_CHUNK):
        cols = slice(c * MM_CHUNK, (c + 1) * MM_CHUNK)
        accs = [jnp.dot(x, wb[a][:, cols], preferred_element_type=F32) for a in range(n_w)]
        if epilogue == "plain":
            y = accs[0]
        elif epilogue == "gelu":
            y = _gelu(accs[0])
        elif epilogue == "scale":
            y = accs[0] * scale
        else:
            g = _silu(accs[0]) * accs[1]
            col = j * bn + c * MM_CHUNK + lax.broadcasted_iota(jnp.int32, g.shape, 1)
            y = jnp.where(col < n_valid, g, 0.0)
        o_ref[:, cols] = y.astype(o_ref.dtype)

    if side_blocks:
        t = j * m_blocks + i
        side_out[...] = jnp.where(t < side_blocks, side_in[...], 0.0).astype(side_out.dtype)


def _mm_cast(x, ws, *, bm, bn, n_cols, m_rows=None, col0=0, n_valid=None, epilogue="plain", scale=1.0,
             side=None, name):
    m, k = x.shape
    m = m if m_rows is None else m_rows
    n_valid = n_cols if n_valid is None else n_valid
    n_w = len(ws)
    assert m % bm == 0 and n_cols % bn == 0 and bn % MM_CHUNK == 0 and k % CAST_ROWS == 0
    n_blocks = n_cols // bn
    m_blocks = m // bm
    in_specs = [pl.BlockSpec((bm, k), lambda j, i: (i, 0))]
    in_specs += [pl.BlockSpec(memory_space=pl.ANY)] * n_w
    out_specs = [pl.BlockSpec((bm, bn), lambda j, i: (i, j))]
    out_shape = [jax.ShapeDtypeStruct((m, n_cols), BF16)]
    args = [x, *ws]
    side_blocks = 0
    if side is not None:
        w_side, padded_rows = side
        side_blocks = w_side.shape[0] // SIDE_ROWS
        assert padded_rows == n_blocks * m_blocks * SIDE_ROWS and side_blocks * SIDE_ROWS == w_side.shape[0]
        c = w_side.shape[1]
        in_specs.append(pl.BlockSpec(
            (SIDE_ROWS, c), lambda j, i: (jnp.minimum(j * m_blocks + i, side_blocks - 1), 0)))
        out_specs.append(pl.BlockSpec((SIDE_ROWS, c), lambda j, i: (j * m_blocks + i, 0)))
        out_shape.append(jax.ShapeDtypeStruct((padded_rows, c), BF16))
        args.append(w_side)
    kern = functools.partial(
        _mm_cast_kernel, n_w=n_w, bn=bn, n_blocks=n_blocks, m_blocks=m_blocks, col0=col0, n_valid=n_valid,
        epilogue=epilogue, scale=scale, side_blocks=side_blocks)
    out = pl.pallas_call(
        kern,
        grid=(n_blocks, m_blocks),
        in_specs=in_specs,
        out_specs=out_specs,
        out_shape=out_shape,
        scratch_shapes=([pltpu.VMEM((k, bn), F32)] * n_w + [pltpu.VMEM((k, bn), BF16)] * n_w
                        + [pltpu.SemaphoreType.DMA((n_w,))]),
        compiler_params=_cparams(("arbitrary", "arbitrary")),
        name=name,
    )(*args)
    return out if side is not None else out[0]


LN_ROWS = 64


def _down_ln_kernel(g_ref, w_ref, x_ref, gate_ref, lg_ref, lb_ref, o_ref):
    kk = pl.program_id(1)

    @pl.when(kk == 0)
    def _():
        o_ref[...] = jnp.dot(g_ref[...], w_ref[...], preferred_element_type=F32)

    @pl.when(kk > 0)
    def _():
        o_ref[...] += jnp.dot(g_ref[...], w_ref[...], preferred_element_type=F32)

    @pl.when(kk == pl.num_programs(1) - 1)
    def _():
        gate = gate_ref[...]
        lg = lg_ref[...]
        lb = lb_ref[...]

        def ln_chunk(r, carry):
            rows = pl.ds(pl.multiple_of(r * LN_ROWS, LN_ROWS), LN_ROWS)
            z = DEEPNORM_ALPHA * x_ref[rows, :] + gate * o_ref[rows, :]
            o_ref[rows, :] = _ln_rows(z) * lg + lb
            return carry

        lax.fori_loop(0, o_ref.shape[0] // LN_ROWS, ln_chunk, 0)


def _down_ln(g, w, x, gate, lg, lb, *, bm, bk):
    m, k = g.shape
    n = w.shape[1]
    assert m % bm == 0 and k % bk == 0
    row = pl.BlockSpec((bm, n), lambda i, kk: (i, 0))
    vec = pl.BlockSpec((1, n), lambda i, kk: (0, 0))
    return pl.pallas_call(
        _down_ln_kernel,
        grid=(m // bm, k // bk),
        in_specs=[
            pl.BlockSpec((bm, bk), lambda i, kk: (i, kk)),
            pl.BlockSpec((bk, n), lambda i, kk: (kk, 0)),
            row, vec, vec, vec,
        ],
        out_specs=row,
        out_shape=jax.ShapeDtypeStruct((m, n), F32),
        compiler_params=_cparams(("arbitrary", "arbitrary")),
        name="swiglu_down_ln",
    )(g, w, x, gate, lg, lb)


N_PAIR_ROWS = 2 * WIN_H - 2


def _bias_kernel(rpb_ref, o_ref):
    h = pl.program_id(0)
    n_dcol = 2 * WIN_W - 1
    q = lax.broadcasted_iota(jnp.int32, (GRID_W, 2 * GRID_W), 0)
    l = lax.broadcasted_iota(jnp.int32, (GRID_W, 2 * GRID_W), 1)
    kc = l & (GRID_W - 1)
    hi = l >= GRID_W
    dcol = jnp.clip(kc - q, -(WIN_W - 1), WIN_W - 1) + (WIN_W - 1)
    cs = jnp.clip(q - WIN_W // 2, 0, GRID_W - WIN_W)
    inside = (kc >= cs) & (kc < cs + WIN_W)
    base = h * ((2 * WIN_H - 1) * n_dcol)

    def body(dra, carry):
        acc = jnp.zeros((GRID_W, 2 * GRID_W), F32)
        for d in range(n_dcol):
            lo_v = rpb_ref[base + dra * n_dcol + d]
            hi_v = rpb_ref[base + (dra + 1) * n_dcol + d]
            acc = jnp.where(dcol == d, jnp.where(hi, hi_v, lo_v), acc)
        o_ref[dra] = jnp.where(inside, acc, -jnp.inf)
        return carry

    lax.fori_loop(0, N_PAIR_ROWS, body, 0)


def _bias_table(rpb_flat, n_heads):
    return pl.pallas_call(
        _bias_kernel,
        grid=(n_heads,),
        in_specs=[pl.BlockSpec(memory_space=pltpu.SMEM)],
        out_specs=pl.BlockSpec((None, N_PAIR_ROWS, GRID_W, 2 * GRID_W), lambda h: (h, 0, 0, 0)),
        out_shape=jax.ShapeDtypeStruct((n_heads, N_PAIR_ROWS, GRID_W, 2 * GRID_W), F32),
        compiler_params=_cparams(("arbitrary",)),
        name="bias_table",
    )(rpb_flat)


ROWS_PER_STEP = 32


def _attn_kernel(q_ref, k_ref, v_ref, kc_ref, vc_ref, tp_ref, o_ref, *, n_rows):
    jb = pl.program_id(1)
    n_lat = WIN_H * GRID_W
    q = q_ref[...]
    kc = kc_ref[...]
    vc = vc_ref[...]
    dn_t = (((1,), (1,)), ((), ()))
    s_ctx = lax.dot_general(q, kc, dn_t, preferred_element_type=F32)
    starts, s_rows = [], []
    for t in range(ROWS_PER_STEP):
        r = jb * ROWS_PER_STEP + t
        ks = jnp.clip(r - WIN_H // 2, 0, n_rows - WIN_H)
        shift = r - ks
        start = pl.multiple_of(ks * GRID_W, GRID_W)
        starts.append(start)
        kw = k_ref[pl.ds(start, n_lat), :]
        qt = q[t * GRID_W:(t + 1) * GRID_W]
        s = lax.dot_general(qt, kw, dn_t, preferred_element_type=F32)
        bias = jnp.concatenate(
            [tp_ref[2 * p - shift + (WIN_H - 1)] for p in range(WIN_H // 2)], axis=1)
        s_rows.append(s + bias)
    s_lat = jnp.concatenate(s_rows, axis=0)
    m = jnp.maximum(jnp.max(s_lat, axis=1, keepdims=True), jnp.max(s_ctx, axis=1, keepdims=True))
    e = jnp.exp(s_lat - m).astype(BF16)
    ec = jnp.exp(s_ctx - m).astype(BF16)
    ones = jnp.ones((n_lat, HEAD_DIM), BF16)
    vc_aug = jnp.concatenate([vc, ones[:vc.shape[0]]], axis=1)
    oc = jnp.dot(ec, vc_aug, preferred_element_type=F32)
    for t in range(ROWS_PER_STEP):
        rows = slice(t * GRID_W, (t + 1) * GRID_W)
        vw_aug = jnp.concatenate([v_ref[pl.ds(starts[t], n_lat), :], ones], axis=1)
        o = jnp.dot(e[rows], vw_aug, preferred_element_type=F32) + oc[rows]
        o_ref[rows, :] = (o[:, :HEAD_DIM] / o[:, HEAD_DIM:]).astype(o_ref.dtype)


def _attention(q, kv, tp, *, n_heads):
    s = q.shape[0]
    n_ctx = kv.shape[0] - s
    assert s % n_ctx == 0
    ctx_blk = s // n_ctx
    n_rows = s // GRID_W
    tq = ROWS_PER_STEP * GRID_W
    return pl.pallas_call(
        functools.partial(_attn_kernel, n_rows=n_rows),
        grid=(n_heads, n_rows // ROWS_PER_STEP),
        in_specs=[
            pl.BlockSpec((tq, HEAD_DIM), lambda h, j: (j, h)),
            pl.BlockSpec((s, HEAD_DIM), lambda h, j: (0, h)),
            pl.BlockSpec((s, HEAD_DIM), lambda h, j: (0, n_heads + h)),
            pl.BlockSpec((n_ctx, HEAD_DIM), lambda h, j: (ctx_blk, h)),
            pl.BlockSpec((n_ctx, HEAD_DIM), lambda h, j: (ctx_blk, n_heads + h)),
            pl.BlockSpec((None, N_PAIR_ROWS, GRID_W, 2 * GRID_W), lambda h, j: (h, 0, 0, 0)),
        ],
        out_specs=pl.BlockSpec((tq, HEAD_DIM), lambda h, j: (j, h)),
        out_shape=jax.ShapeDtypeStruct((s, n_heads * HEAD_DIM), BF16),
        compiler_params=_cparams(("arbitrary", "arbitrary")),
        name="nbr_attention",
    )(q, kv, kv, kv, kv, tp)


SGU_CHUNKS_PER_STEP = 4


def _sgu_merge_kernel(gu_ref, gv_ref, ob_ref, ws_ref, bst_ref, lg_ref, lb_ref, gn_ref, o_ref, oa_ref,
                      *, n_heads, w_sgu):
    v = _ln_rows(gv_ref[...].astype(F32)) * lg_ref[...] + lb_ref[...]
    v = v.astype(BF16)
    bst = bst_ref[...]
    for h in range(n_heads):
        cols = slice(h * HEAD_DIM, (h + 1) * HEAD_DIM)
        vh = jnp.concatenate(
            [v[c * CHUNK:(c + 1) * CHUNK, cols] for c in range(SGU_CHUNKS_PER_STEP)], axis=1)
        mixed = jnp.dot(ws_ref[h], vh, preferred_element_type=F32) + bst[:, h:h + 1]
        for c in range(SGU_CHUNKS_PER_STEP):
            rows = slice(c * CHUNK, (c + 1) * CHUNK)
            oa_ref[rows, cols] = gu_ref[rows, cols].astype(F32) * mixed[:, c * HEAD_DIM:(c + 1) * HEAD_DIM]
    gn = gn_ref[...]
    o_ref[:, :w_sgu] = (_rms_rows(oa_ref[...]) * gn[:, :w_sgu]).astype(o_ref.dtype)
    o_ref[:, w_sgu:] = (_rms_rows(ob_ref[...].astype(F32)) * gn[:, w_sgu:]).astype(o_ref.dtype)


def _sgu_merge(p, ob, ws, bst, lg, lb, gn, *, n_heads):
    s = p.shape[0]
    w_sgu = n_heads * HEAD_DIM
    w_na = ob.shape[1]
    tr = SGU_CHUNKS_PER_STEP * CHUNK
    return pl.pallas_call(
        functools.partial(_sgu_merge_kernel, n_heads=n_heads, w_sgu=w_sgu),
        grid=(s // tr,),
        in_specs=[
            pl.BlockSpec((tr, w_sgu), lambda i: (i, 0)),
            pl.BlockSpec((tr, w_sgu), lambda i: (i, 1)),
            pl.BlockSpec((tr, w_na), lambda i: (i, 0)),
            pl.BlockSpec((n_heads, CHUNK, CHUNK), lambda i: (0, 0, 0)),
            pl.BlockSpec((CHUNK, n_heads), lambda i: (0, 0)),
            pl.BlockSpec((1, w_sgu), lambda i: (0, 0)),
            pl.BlockSpec((1, w_sgu), lambda i: (0, 0)),
            pl.BlockSpec((1, w_sgu + w_na), lambda i: (0, 0)),
        ],
        out_specs=pl.BlockSpec((tr, w_sgu + w_na), lambda i: (i, 0)),
        out_shape=jax.ShapeDtypeStruct((s, w_sgu + w_na), BF16),
        scratch_shapes=[pltpu.VMEM((tr, w_sgu), F32)],
        compiler_params=_cparams(("parallel",)),
        name="sgu_merge",
    )(p, p, ob, ws, bst, lg, lb, gn)


def _res_ln_mod_kernel(x_ref, y_ref, g_ref, lg_ref, lb_ref, sc_ref, sh_ref, x1_ref, h_ref):
    xn = _ln_rows(x_ref[...])
    z = DEEPNORM_ALPHA * xn + g_ref[...] * y_ref[...].astype(F32)
    x1 = _ln_rows(z) * lg_ref[...] + lb_ref[...]
    x1_ref[...] = x1
    h_ref[...] = (x1 * (1.0 + sc_ref[...]) + sh_ref[...]).astype(h_ref.dtype)


def _res_ln_mod(x, y, gate, lg, lb, sc, sh, tr):
    m, d = x.shape
    row = pl.BlockSpec((tr, d), lambda i: (i, 0))
    vec = pl.BlockSpec((1, d), lambda i: (0, 0))
    return pl.pallas_call(
        _res_ln_mod_kernel,
        grid=(m // tr,),
        in_specs=[row, row, vec, vec, vec, vec, vec],
        out_specs=[row, row],
        out_shape=[jax.ShapeDtypeStruct((m, d), F32), jax.ShapeDtypeStruct((m, d), BF16)],
        compiler_params=_cparams(("parallel",)),
        name="res_ln_mod",
    )(x, y, gate, lg, lb, sc, sh)


def kernel(x, c, ctx, c_ctx, w_ada, b_ada, w_in, w_s, b_s, sgu_g, sgu_b, rpb, gn_g, w_o, ln_g, ln_b,
           w_gate, w_up, w_down):
    batch, seq, d = x.shape
    assert batch == 1 and w_ada.shape[0] == DEPTH
    n_sgu_heads = w_s.shape[1]
    n_na_heads = rpb.shape[1]
    w_sgu = n_sgu_heads * HEAD_DIM
    w_na = n_na_heads * HEAD_DIM
    d_ff = w_gate.shape[2]
    assert w_in.shape[2] == 2 * w_sgu + 3 * w_na and w_sgu == w_na

    x2 = x.reshape(seq, d)
    ctx2 = ctx.reshape(ctx.shape[1], d)

    ccol = jnp.stack([c.reshape(d), c_ctx], axis=1)
    mod = _modulation(ccol, w_ada[0], b_ada[0].reshape(1, 6 * d))
    g1, sh2, sc2, g2 = [mod[0:1, i * d:(i + 1) * d] for i in range(2, 6)]

    h = _ln_mod(x2, ctx2, mod, 256)

    guv = _mm_cast(h, [w_in[0]], m_rows=seq, bm=1024, bn=1024, n_cols=2 * w_sgu, epilogue="gelu",
                   name="in_proj_uv")
    q = _mm_cast(h, [w_in[0]], m_rows=seq, bm=1024, bn=1024, n_cols=w_na, col0=2 * w_sgu,
                 epilogue="scale", scale=HEAD_DIM ** -0.5, name="in_proj_q")
    kv = _mm_cast(h, [w_in[0]], bm=768, bn=1024, n_cols=2 * w_na, col0=2 * w_sgu + w_na,
                  name="in_proj_kv")

    tp = _bias_table(rpb[0].reshape(-1), n_na_heads)
    ob = _attention(q, kv, tp, n_heads=n_na_heads)

    merged = _sgu_merge(guv, ob, w_s[0].astype(BF16), b_s[0].T, sgu_g, sgu_b, gn_g, n_heads=n_sgu_heads)

    y = _mm_cast(merged, [w_o[0]], bm=1024, bn=1024, n_cols=d, name="out_proj")
    x1, h2 = _res_ln_mod(x2, y, g1, ln_g[0, 0:1], ln_b[0, 0:1], sc2, sh2, 256)

    bn_ff = 512
    d_ffp = -(-d_ff // 1024) * 1024
    g, wd = _mm_cast(h2, [w_gate[0], w_up[0]], bm=1024, bn=bn_ff, n_cols=d_ffp, n_valid=d_ff,
                     epilogue="swiglu", side=(w_down[0], d_ffp), name="swiglu_up")
    out = _down_ln(g, wd, x1, g2, ln_g[0, 1:2], ln_b[0, 1:2], bm=512, bk=1024)
    return out.reshape(batch, seq, d)
```

```python
import functools
import math

import jax
import jax.numpy as jnp
from jax import lax
from jax.experimental import pallas as pl
from jax.experimental.pallas import tpu as pltpu

F32 = jnp.float32
BF16 = jnp.bfloat16

GRID_W = 64
HEAD_DIM = 128
CHUNK = 128
WIN_H = 8
WIN_W = 16
LN_EPS = 1e-5
DEPTH = 1
DEEPNORM_ALPHA = (2.0 * DEPTH) ** 0.25

LANES = 128
SUBLANES = 8
VMEM_LIMIT = 56 * 1024 * 1024


def _cparams(sem):
    return pltpu.CompilerParams(dimension_semantics=sem, vmem_limit_bytes=VMEM_LIMIT)


def _gelu(x):
    return 0.5 * x * (1.0 + lax.erf(x * (1.0 / math.sqrt(2.0))))


def _silu(x):
    return x * jax.nn.sigmoid(x)


def _ln_rows(x):
    mu = jnp.mean(x, axis=-1, keepdims=True)
    xc = x - mu
    var = jnp.mean(xc * xc, axis=-1, keepdims=True)
    return xc * lax.rsqrt(var + LN_EPS)


def _rms_rows(x):
    return x * lax.rsqrt(jnp.mean(x * x, axis=-1, keepdims=True) + LN_EPS)


def _mod_kernel(ccol_ref, w_ref, b_ref, o_ref, sb_ref, *, tn):
    d = w_ref.shape[0]

    @pl.when(pl.program_id(0) == 0)
    def _():
        s = _silu(ccol_ref[...])
        sb_ref[0] = jnp.broadcast_to(s[:, 0:1], (d, LANES))
        sb_ref[1] = jnp.broadcast_to(s[:, 1:2], (d, LANES))

    nl = tn // LANES

    def body(kc, accs):
        k0 = pl.multiple_of(kc * SUBLANES, SUBLANES)
        s0 = sb_ref[0, pl.ds(k0, SUBLANES), :]
        s1 = sb_ref[1, pl.ds(k0, SUBLANES), :]
        w = w_ref[pl.ds(k0, SUBLANES), :]
        out = []
        for j in range(nl):
            wj = w[:, j * LANES:(j + 1) * LANES]
            out.append(accs[2 * j] + wj * s0)
            out.append(accs[2 * j + 1] + wj * s1)
        return tuple(out)

    zero = jnp.zeros((SUBLANES, LANES), F32)
    accs = lax.fori_loop(0, d // SUBLANES, body, (zero,) * (2 * nl), unroll=8)
    r0 = jnp.concatenate([jnp.sum(accs[2 * j], axis=0, keepdims=True) for j in range(nl)], axis=1)
    r1 = jnp.concatenate([jnp.sum(accs[2 * j + 1], axis=0, keepdims=True) for j in range(nl)], axis=1)
    row = lax.broadcasted_iota(jnp.int32, (SUBLANES, tn), 0)
    b = b_ref[...]
    o_ref[...] = jnp.where(row == 0, r0 + b, jnp.where(row == 1, r1 + b, 0.0))


def _modulation(ccol, w_ada, b_ada):
    d, n = w_ada.shape
    tn = 512
    return pl.pallas_call(
        functools.partial(_mod_kernel, tn=tn),
        grid=(n // tn,),
        in_specs=[
            pl.BlockSpec((d, 2), lambda j: (0, 0)),
            pl.BlockSpec((d, tn), lambda j: (0, j)),
            pl.BlockSpec((1, tn), lambda j: (0, j)),
        ],
        out_specs=pl.BlockSpec((SUBLANES, tn), lambda j: (0, j)),
        out_shape=jax.ShapeDtypeStruct((SUBLANES, n), F32),
        scratch_shapes=[pltpu.VMEM((2, d, LANES), F32)],
        compiler_params=_cparams(("arbitrary",)),
        name="modulation",
    )(ccol, w_ada, b_ada)


def _ln_mod_kernel(x_ref, c_ref, mod_ref, o_ref, *, n_x_blocks):
    d = x_ref.shape[1]
    i = pl.program_id(0)

    def emit(src_ref, row):
        sh = mod_ref[row:row + 1, 0:d]
        sc = mod_ref[row:row + 1, d:2 * d]
        o_ref[...] = (_ln_rows(src_ref[...]) * (1.0 + sc) + sh).astype(o_ref.dtype)

    @pl.when(i < n_x_blocks)
    def _():
        emit(x_ref, 0)

    @pl.when(i >= n_x_blocks)
    def _():
        emit(c_ref, 1)


def _ln_mod(x, ctx, mod, tr):
    m, d = x.shape
    n_x_blocks = m // tr
    n_c_blocks = ctx.shape[0] // tr
    return pl.pallas_call(
        functools.partial(_ln_mod_kernel, n_x_blocks=n_x_blocks),
        grid=(n_x_blocks + n_c_blocks,),
        in_specs=[
            pl.BlockSpec((tr, d), lambda i: (jnp.minimum(i, n_x_blocks - 1), 0)),
            pl.BlockSpec((tr, d), lambda i: (jnp.maximum(i - n_x_blocks, 0), 0)),
            pl.BlockSpec((SUBLANES, 2 * d), lambda i: (0, 0)),
        ],
        out_specs=pl.BlockSpec((tr, d), lambda i: (i, 0)),
        out_shape=jax.ShapeDtypeStruct((m + ctx.shape[0], d), BF16),
        compiler_params=_cparams(("arbitrary",)),
        name="ln_mod",
    )(x, ctx, mod)


CAST_ROWS = 256
SIDE_ROWS = 64
MM_CHUNK = 256


def _mm_cast_kernel(*refs, n_w, bn, n_blocks, m_blocks, col0, n_cols, epilogue, scale, has_side, head_major):
    it = iter(refs)
    x_ref = next(it)
    w_hbm = [next(it) for _ in range(n_w)]
    side_in = next(it) if has_side else None
    o_ref = next(it)
    side_out = next(it) if has_side else None
    stage = [next(it) for _ in range(n_w)]
    wb = [next(it) for _ in range(n_w)]
    sem = next(it)

    j = pl.program_id(0)
    i = pl.program_id(1)
    k = stage[0].shape[0]
    w_last = n_cols - (n_blocks - 1) * bn

    def slab_copies(jj, width):
        return [pltpu.make_async_copy(w_hbm[a].at[:, pl.ds(col0 + jj * bn, width)],
                                      stage[a].at[:, pl.ds(0, width)], sem.at[a]) for a in range(n_w)]

    def for_slab(jj, fn):
        if w_last == bn:
            for cp in slab_copies(jj, bn):
                fn(cp)
            return

        @pl.when(jj < n_blocks - 1)
        def _():
            for cp in slab_copies(jj, bn):
                fn(cp)

        @pl.when(jj == n_blocks - 1)
        def _():
            for cp in slab_copies(n_blocks - 1, w_last):
                fn(cp)

    @pl.when((j == 0) & (i == 0))
    def _():
        for_slab(j, lambda cp: cp.start())

    @pl.when(i == 0)
    def _():
        for_slab(j, lambda cp: cp.wait())

        def cast_rows(r, carry):
            r0 = pl.multiple_of(r * CAST_ROWS, CAST_ROWS)
            for a in range(n_w):
                wb[a][pl.ds(r0, CAST_ROWS), :] = stage[a][pl.ds(r0, CAST_ROWS), :].astype(BF16)
            return carry

        lax.fori_loop(0, k // CAST_ROWS, cast_rows, 0)

    @pl.when((i == min(1, m_blocks - 1)) & (j + 1 < n_blocks))
    def _():
        for_slab(j + 1, lambda cp: cp.start())

    def compute(n_chunks):
        x = x_ref[...]
        for c in range(n_chunks):
            cols = slice(c * MM_CHUNK, (c + 1) * MM_CHUNK)
            accs = [jnp.dot(x, wb[a][:, cols], preferred_element_type=F32) for a in range(n_w)]
            if epilogue == "plain":
                y = accs[0]
            elif epilogue == "gelu":
                y = _gelu(accs[0])
            elif epilogue == "scale":
                y = accs[0] * scale
            else:
                y = _silu(accs[0]) * accs[1]
            y = y.astype(o_ref.dtype)
            if head_major:
                for hh in range(MM_CHUNK // HEAD_DIM):
                    o_ref[c * (MM_CHUNK // HEAD_DIM) + hh] = y[:, hh * HEAD_DIM:(hh + 1) * HEAD_DIM]
            else:
                o_ref[:, cols] = y

    full_chunks = bn // MM_CHUNK
    last_chunks = -(-w_last // MM_CHUNK)
    if last_chunks == full_chunks:
        compute(full_chunks)
    else:
        @pl.when(j < n_blocks - 1)
        def _():
            compute(full_chunks)

        @pl.when(j == n_blocks - 1)
        def _():
            compute(last_chunks)

    if side_in is not None:
        side_out[...] = side_in[...].astype(side_out.dtype)


def _mm_cast(x, ws, *, bm, bn, n_cols, m_rows=None, col0=0, epilogue="plain", scale=1.0, side=None,
             head_major=False, name):
    m, k = x.shape
    m = m if m_rows is None else m_rows
    n_w = len(ws)
    assert m % bm == 0 and bn % MM_CHUNK == 0 and k % CAST_ROWS == 0
    n_blocks = -(-n_cols // bn)
    m_blocks = m // bm
    in_specs = [pl.BlockSpec((bm, k), lambda j, i: (i, 0))]
    in_specs += [pl.BlockSpec(memory_space=pl.ANY)] * n_w
    if head_major:
        assert n_cols % bn == 0
        out_specs = [pl.BlockSpec((bn // HEAD_DIM, bm, HEAD_DIM), lambda j, i: (j, i, 0))]
        out_shape = [jax.ShapeDtypeStruct((n_cols // HEAD_DIM, m, HEAD_DIM), BF16)]
    else:
        out_specs = [pl.BlockSpec((bm, bn), lambda j, i: (i, j))]
        out_shape = [jax.ShapeDtypeStruct((m, n_cols), BF16)]
    args = [x, *ws]
    if side is not None:
        rows, c = side.shape
        side_blocks = rows // SIDE_ROWS
        assert side_blocks * SIDE_ROWS == rows and side_blocks <= n_blocks * m_blocks
        side_spec = pl.BlockSpec(
            (SIDE_ROWS, c), lambda j, i: (jnp.minimum(j * m_blocks + i, side_blocks - 1), 0))
        in_specs.append(side_spec)
        out_specs.append(side_spec)
        out_shape.append(jax.ShapeDtypeStruct((rows, c), BF16))
        args.append(side)
    kern = functools.partial(
        _mm_cast_kernel, n_w=n_w, bn=bn, n_blocks=n_blocks, m_blocks=m_blocks, col0=col0, n_cols=n_cols,
        epilogue=epilogue, scale=scale, has_side=side is not None, head_major=head_major)
    out = pl.pallas_call(
        kern,
        grid=(n_blocks, m_blocks),
        in_specs=in_specs,
        out_specs=out_specs,
        out_shape=out_shape,
        scratch_shapes=([pltpu.VMEM((k, bn), F32)] * n_w + [pltpu.VMEM((k, bn), BF16)] * n_w
                        + [pltpu.SemaphoreType.DMA((n_w,))]),
        compiler_params=_cparams(("arbitrary", "arbitrary")),
        name=name,
    )(*args)
    return out if side is not None else out[0]


LN_ROWS = 64


def _down_ln_kernel(g_ref, w_ref, x_ref, gate_ref, lg_ref, lb_ref, o_ref, *, k_blocks, k_last):
    kk = pl.program_id(1)

    def part(kw):
        return jnp.dot(g_ref[:, :kw], w_ref[:kw, :], preferred_element_type=F32)

    bk = w_ref.shape[0]

    @pl.when(kk == 0)
    def _():
        o_ref[...] = part(bk)

    @pl.when((kk > 0) & (kk < k_blocks - 1))
    def _():
        o_ref[...] += part(bk)

    @pl.when(kk == k_blocks - 1)
    def _():
        o_ref[...] += part(k_last)
        gate = gate_ref[...]
        lg = lg_ref[...]
        lb = lb_ref[...]

        def ln_chunk(r, carry):
            rows = pl.ds(pl.multiple_of(r * LN_ROWS, LN_ROWS), LN_ROWS)
            z = DEEPNORM_ALPHA * x_ref[rows, :] + gate * o_ref[rows, :]
            o_ref[rows, :] = _ln_rows(z) * lg + lb
            return carry

        lax.fori_loop(0, o_ref.shape[0] // LN_ROWS, ln_chunk, 0)


def _down_ln(g, w, x, gate, lg, lb, *, bm, bk):
    m, k = g.shape
    n = w.shape[1]
    assert m % bm == 0 and w.shape[0] == k
    k_blocks = -(-k // bk)
    k_last = k - (k_blocks - 1) * bk
    assert k_blocks >= 2 and k_last % LANES == 0
    row = pl.BlockSpec((bm, n), lambda i, kk: (i, 0))
    vec = pl.BlockSpec((1, n), lambda i, kk: (0, 0))
    return pl.pallas_call(
        functools.partial(_down_ln_kernel, k_blocks=k_blocks, k_last=k_last),
        grid=(m // bm, k_blocks),
        in_specs=[
            pl.BlockSpec((bm, bk), lambda i, kk: (i, kk)),
            pl.BlockSpec((bk, n), lambda i, kk: (kk, 0)),
            row, vec, vec, vec,
        ],
        out_specs=row,
        out_shape=jax.ShapeDtypeStruct((m, n), F32),
        compiler_params=_cparams(("arbitrary", "arbitrary")),
        name="swiglu_down_ln",
    )(g, w, x, gate, lg, lb)


N_PAIR_ROWS = 2 * WIN_H - 2


def _bias_kernel(rpb_ref, o_ref):
    h = pl.program_id(0)
    n_dcol = 2 * WIN_W - 1
    q = lax.broadcasted_iota(jnp.int32, (GRID_W, 2 * GRID_W), 0)
    l = lax.broadcasted_iota(jnp.int32, (GRID_W, 2 * GRID_W), 1)
    kc = l & (GRID_W - 1)
    hi = l >= GRID_W
    dcol = jnp.clip(kc - q, -(WIN_W - 1), WIN_W - 1) + (WIN_W - 1)
    cs = jnp.clip(q - WIN_W // 2, 0, GRID_W - WIN_W)
    inside = (kc >= cs) & (kc < cs + WIN_W)
    base = h * ((2 * WIN_H - 1) * n_dcol)

    def body(dra, carry):
        acc = jnp.zeros((GRID_W, 2 * GRID_W), F32)
        for d in range(n_dcol):
            lo_v = rpb_ref[base + dra * n_dcol + d]
            hi_v = rpb_ref[base + (dra + 1) * n_dcol + d]
            acc = jnp.where(dcol == d, jnp.where(hi, hi_v, lo_v), acc)
        o_ref[dra] = jnp.where(inside, acc, -jnp.inf)
        return carry

    lax.fori_loop(0, N_PAIR_ROWS, body, 0)


def _bias_table(rpb_flat, n_heads):
    return pl.pallas_call(
        _bias_kernel,
        grid=(n_heads,),
        in_specs=[pl.BlockSpec(memory_space=pltpu.SMEM)],
        out_specs=pl.BlockSpec((None, N_PAIR_ROWS, GRID_W, 2 * GRID_W), lambda h: (h, 0, 0, 0)),
        out_shape=jax.ShapeDtypeStruct((n_heads, N_PAIR_ROWS, GRID_W, 2 * GRID_W), F32),
        compiler_params=_cparams(("arbitrary",)),
        name="bias_table",
    )(rpb_flat)


ROWS_PER_STEP = 32


def _attn_kernel(q_ref, k_ref, v_ref, kc_ref, vc_ref, tp_ref, o_ref, *, n_rows):
    jb = pl.program_id(1)
    n_lat = WIN_H * GRID_W
    q = q_ref[...]
    kc = kc_ref[...]
    vc = vc_ref[...]
    dn_t = (((1,), (1,)), ((), ()))
    s_ctx = lax.dot_general(q, kc, dn_t, preferred_element_type=F32)
    starts, s_rows = [], []
    for t in range(ROWS_PER_STEP):
        r = jb * ROWS_PER_STEP + t
        ks = jnp.clip(r - WIN_H // 2, 0, n_rows - WIN_H)
        shift = r - ks
        start = pl.multiple_of(ks * GRID_W, GRID_W)
        starts.append(start)
        kw = k_ref[pl.ds(start, n_lat), :]
        qt = q[t * GRID_W:(t + 1) * GRID_W]
        s = lax.dot_general(qt, kw, dn_t, preferred_element_type=F32)
        bias = jnp.concatenate(
            [tp_ref[2 * p - shift + (WIN_H - 1)] for p in range(WIN_H // 2)], axis=1)
        s_rows.append(s + bias)
    s_lat = jnp.concatenate(s_rows, axis=0)
    tiles = [s_lat[:, c * LANES:(c + 1) * LANES] for c in range(n_lat // LANES)]
    tiles += [s_ctx[:, c * LANES:(c + 1) * LANES] for c in range(s_ctx.shape[1] // LANES)]
    m = jnp.max(functools.reduce(jnp.maximum, tiles), axis=1, keepdims=True)
    e = jnp.exp((s_lat - m).astype(BF16))
    ec = jnp.exp((s_ctx - m).astype(BF16))
    ones = jnp.ones((n_lat, HEAD_DIM), BF16)
    vc_aug = jnp.concatenate([vc, ones[:vc.shape[0]]], axis=1)
    oc = jnp.dot(ec, vc_aug, preferred_element_type=F32)
    for t in range(ROWS_PER_STEP):
        rows = slice(t * GRID_W, (t + 1) * GRID_W)
        vw_aug = jnp.concatenate([v_ref[pl.ds(starts[t], n_lat), :], ones], axis=1)
        o = jnp.dot(e[rows], vw_aug, preferred_element_type=F32) + oc[rows]
        o_ref[rows, :] = (o[:, :HEAD_DIM] / o[:, HEAD_DIM:]).astype(o_ref.dtype)


def _attention(q, kv, tp):
    n_heads, s, _ = q.shape
    n_ctx = kv.shape[1] - s
    assert s % n_ctx == 0
    ctx_blk = s // n_ctx
    n_rows = s // GRID_W
    tq = ROWS_PER_STEP * GRID_W
    return pl.pallas_call(
        functools.partial(_attn_kernel, n_rows=n_rows),
        grid=(n_heads, n_rows // ROWS_PER_STEP),
        in_specs=[
            pl.BlockSpec((None, tq, HEAD_DIM), lambda h, j: (h, j, 0)),
            pl.BlockSpec((None, s, HEAD_DIM), lambda h, j: (h, 0, 0)),
            pl.BlockSpec((None, s, HEAD_DIM), lambda h, j: (n_heads + h, 0, 0)),
            pl.BlockSpec((None, n_ctx, HEAD_DIM), lambda h, j: (h, ctx_blk, 0)),
            pl.BlockSpec((None, n_ctx, HEAD_DIM), lambda h, j: (n_heads + h, ctx_blk, 0)),
            pl.BlockSpec((None, N_PAIR_ROWS, GRID_W, 2 * GRID_W), lambda h, j: (h, 0, 0, 0)),
        ],
        out_specs=pl.BlockSpec((None, tq, HEAD_DIM), lambda h, j: (h, j, 0)),
        out_shape=jax.ShapeDtypeStruct((n_heads, s, HEAD_DIM), BF16),
        compiler_params=_cparams(("arbitrary", "arbitrary")),
        name="nbr_attention",
    )(q, kv, kv, kv, kv, tp)


SGU_CHUNKS_PER_STEP = 4


def _sgu_merge_kernel(gu_ref, gv_ref, ob_ref, ws_ref, bst_ref, lg_ref, lb_ref, gn_ref, o_ref, oa_ref,
                      *, n_heads, w_sgu):
    v = _ln_rows(gv_ref[...].astype(F32)) * lg_ref[...] + lb_ref[...]
    v = v.astype(BF16)
    bst = bst_ref[...]
    for h in range(n_heads):
        cols = slice(h * HEAD_DIM, (h + 1) * HEAD_DIM)
        vh = jnp.concatenate(
            [v[c * CHUNK:(c + 1) * CHUNK, cols] for c in range(SGU_CHUNKS_PER_STEP)], axis=1)
        mixed = jnp.dot(ws_ref[h], vh, preferred_element_type=F32) + bst[:, h:h + 1]
        for c in range(SGU_CHUNKS_PER_STEP):
            rows = slice(c * CHUNK, (c + 1) * CHUNK)
            oa_ref[rows, cols] = gu_ref[rows, cols].astype(F32) * mixed[:, c * HEAD_DIM:(c + 1) * HEAD_DIM]
    gn = gn_ref[...]
    o_ref[:, :w_sgu] = (_rms_rows(oa_ref[...]) * gn[:, :w_sgu]).astype(o_ref.dtype)
    ob = jnp.concatenate([ob_ref[h] for h in range(ob_ref.shape[0])], axis=1).astype(F32)
    o_ref[:, w_sgu:] = (_rms_rows(ob) * gn[:, w_sgu:]).astype(o_ref.dtype)


def _sgu_merge(p, ob, ws, bst, lg, lb, gn, *, n_heads):
    s = p.shape[0]
    w_sgu = n_heads * HEAD_DIM
    n_na_heads = ob.shape[0]
    w_na = n_na_heads * HEAD_DIM
    tr = SGU_CHUNKS_PER_STEP * CHUNK
    return pl.pallas_call(
        functools.partial(_sgu_merge_kernel, n_heads=n_heads, w_sgu=w_sgu),
        grid=(s // tr,),
        in_specs=[
            pl.BlockSpec((tr, w_sgu), lambda i: (i, 0)),
            pl.BlockSpec((tr, w_sgu), lambda i: (i, 1)),
            pl.BlockSpec((n_na_heads, tr, HEAD_DIM), lambda i: (0, i, 0)),
            pl.BlockSpec((n_heads, CHUNK, CHUNK), lambda i: (0, 0, 0)),
            pl.BlockSpec((CHUNK, n_heads), lambda i: (0, 0)),
            pl.BlockSpec((1, w_sgu), lambda i: (0, 0)),
            pl.BlockSpec((1, w_sgu), lambda i: (0, 0)),
            pl.BlockSpec((1, w_sgu + w_na), lambda i: (0, 0)),
        ],
        out_specs=pl.BlockSpec((tr, w_sgu + w_na), lambda i: (i, 0)),
        out_shape=jax.ShapeDtypeStruct((s, w_sgu + w_na), BF16),
        scratch_shapes=[pltpu.VMEM((tr, w_sgu), F32)],
        compiler_params=_cparams(("parallel",)),
        name="sgu_merge",
    )(p, p, ob, ws, bst, lg, lb, gn)


def _res_ln_mod_kernel(x_ref, y_ref, g_ref, lg_ref, lb_ref, sc_ref, sh_ref, x1_ref, h_ref):
    xn = _ln_rows(x_ref[...])
    z = DEEPNORM_ALPHA * xn + g_ref[...] * y_ref[...].astype(F32)
    x1 = _ln_rows(z) * lg_ref[...] + lb_ref[...]
    x1_ref[...] = x1
    h_ref[...] = (x1 * (1.0 + sc_ref[...]) + sh_ref[...]).astype(h_ref.dtype)


def _res_ln_mod(x, y, gate, lg, lb, sc, sh, tr):
    m, d = x.shape
    row = pl.BlockSpec((tr, d), lambda i: (i, 0))
    vec = pl.BlockSpec((1, d), lambda i: (0, 0))
    return pl.pallas_call(
        _res_ln_mod_kernel,
        grid=(m // tr,),
        in_specs=[row, row, vec, vec, vec, vec, vec],
        out_specs=[row, row],
        out_shape=[jax.ShapeDtypeStruct((m, d), F32), jax.ShapeDtypeStruct((m, d), BF16)],
        compiler_params=_cparams(("parallel",)),
        name="res_ln_mod",
    )(x, y, gate, lg, lb, sc, sh)


def kernel(x, c, ctx, c_ctx, w_ada, b_ada, w_in, w_s, b_s, sgu_g, sgu_b, rpb, gn_g, w_o, ln_g, ln_b,
           w_gate, w_up, w_down):
    batch, seq, d = x.shape
    assert batch == 1 and w_ada.shape[0] == DEPTH
    n_sgu_heads = w_s.shape[1]
    n_na_heads = rpb.shape[1]
    w_sgu = n_sgu_heads * HEAD_DIM
    w_na = n_na_heads * HEAD_DIM
    d_ff = w_gate.shape[2]
    assert w_in.shape[2] == 2 * w_sgu + 3 * w_na and w_sgu == w_na

    x2 = x.reshape(seq, d)
    ctx2 = ctx.reshape(ctx.shape[1], d)

    ccol = jnp.stack([c.reshape(d), c_ctx], axis=1)
    mod = _modulation(ccol, w_ada[0], b_ada[0].reshape(1, 6 * d))
    g1, sh2, sc2, g2 = [mod[0:1, i * d:(i + 1) * d] for i in range(2, 6)]

    h = _ln_mod(x2, ctx2, mod, 256)

    guv = _mm_cast(h, [w_in[0]], m_rows=seq, bm=1024, bn=1024, n_cols=2 * w_sgu, epilogue="gelu",
                   name="in_proj_uv")
    q = _mm_cast(h, [w_in[0]], m_rows=seq, bm=1024, bn=1024, n_cols=w_na, col0=2 * w_sgu,
                 epilogue="scale", scale=HEAD_DIM ** -0.5, head_major=True, name="in_proj_q")
    kv = _mm_cast(h, [w_in[0]], bm=768, bn=1024, n_cols=2 * w_na, col0=2 * w_sgu + w_na,
                  head_major=True, name="in_proj_kv")

    tp = _bias_table(rpb[0].reshape(-1), n_na_heads)
    ob = _attention(q, kv, tp)

    merged = _sgu_merge(guv, ob, w_s[0].astype(BF16), b_s[0].T, sgu_g, sgu_b, gn_g, n_heads=n_sgu_heads)

    y = _mm_cast(merged, [w_o[0]], bm=1024, bn=1024, n_cols=d, name="out_proj")
    x1, h2 = _res_ln_mod(x2, y, g1, ln_g[0, 0:1], ln_b[0, 0:1], sc2, sh2, 256)

    g, wd = _mm_cast(h2, [w_gate[0], w_up[0]], bm=1024, bn=512, n_cols=d_ff, epilogue="swiglu",
                     side=w_down[0], name="swiglu_up")
    out = _down_ln(g, wd, x1, g2, ln_g[0, 1:2], ln_b[0, 1:2], bm=512, bk=1024)
    return out.reshape(batch, seq, d)
```

```python
import functools
import math

import jax
import jax.numpy as jnp
from jax import lax
from jax.experimental import pallas as pl
from jax.experimental.pallas import tpu as pltpu

F32 = jnp.float32
BF16 = jnp.bfloat16

GRID_W = 64
HEAD_DIM = 128
CHUNK = 128
WIN_H = 8
WIN_W = 16
LN_EPS = 1e-5
DEPTH = 1
DEEPNORM_ALPHA = (2.0 * DEPTH) ** 0.25

LANES = 128
SUBLANES = 8
VMEM_LIMIT = 56 * 1024 * 1024


def _cparams(sem):
    return pltpu.CompilerParams(dimension_semantics=sem, vmem_limit_bytes=VMEM_LIMIT)


def _gelu(x):
    return 0.5 * x * (1.0 + lax.erf(x * (1.0 / math.sqrt(2.0))))


def _silu(x):
    return x * jax.nn.sigmoid(x)


def _ln_rows(x):
    mu = jnp.mean(x, axis=-1, keepdims=True)
    xc = x - mu
    var = jnp.mean(xc * xc, axis=-1, keepdims=True)
    return xc * lax.rsqrt(var + LN_EPS)


def _rms_rows(x):
    return x * lax.rsqrt(jnp.mean(x * x, axis=-1, keepdims=True) + LN_EPS)


def _mod_kernel(ccol_ref, w_ref, b_ref, o_ref, sb0_ref, sb_ref, *, tn):
    d = w_ref.shape[0]

    @pl.when(pl.program_id(0) == 0)
    def _():
        s = _silu(ccol_ref[...])
        sb_ref[0] = jnp.broadcast_to(s[:, 0:1], (d, LANES))
        sb_ref[1] = jnp.broadcast_to(s[:, 1:2], (d, LANES))
        sb0_ref[...] = sb_ref[0]

    nl = tn // LANES

    def body(kc, accs):
        k0 = pl.multiple_of(kc * SUBLANES, SUBLANES)
        s0 = sb_ref[0, pl.ds(k0, SUBLANES), :]
        s1 = sb_ref[1, pl.ds(k0, SUBLANES), :]
        w = w_ref[pl.ds(k0, SUBLANES), :]
        out = []
        for j in range(nl):
            wj = w[:, j * LANES:(j + 1) * LANES]
            out.append(accs[2 * j] + wj * s0)
            out.append(accs[2 * j + 1] + wj * s1)
        return tuple(out)

    zero = jnp.zeros((SUBLANES, LANES), F32)
    accs = lax.fori_loop(0, d // SUBLANES, body, (zero,) * (2 * nl), unroll=8)
    r0 = jnp.concatenate([jnp.sum(accs[2 * j], axis=0, keepdims=True) for j in range(nl)], axis=1)
    r1 = jnp.concatenate([jnp.sum(accs[2 * j + 1], axis=0, keepdims=True) for j in range(nl)], axis=1)
    row = lax.broadcasted_iota(jnp.int32, (SUBLANES, tn), 0)
    b = b_ref[...]
    o_ref[...] = jnp.where(row == 0, r0 + b, jnp.where(row == 1, r1 + b, 0.0))


def _modulation(ccol, w_ada, b_ada, n):
    d = w_ada.shape[0]
    tn = 512
    return pl.pallas_call(
        functools.partial(_mod_kernel, tn=tn),
        grid=(n // tn,),
        in_specs=[
            pl.BlockSpec((d, 2), lambda j: (0, 0)),
            pl.BlockSpec((d, tn), lambda j: (0, j)),
            pl.BlockSpec((1, tn), lambda j: (0, j)),
        ],
        out_specs=[pl.BlockSpec((SUBLANES, tn), lambda j: (0, j)),
                   pl.BlockSpec((d, LANES), lambda j: (0, 0))],
        out_shape=[jax.ShapeDtypeStruct((SUBLANES, n), F32), jax.ShapeDtypeStruct((d, LANES), F32)],
        scratch_shapes=[pltpu.VMEM((2, d, LANES), F32)],
        compiler_params=_cparams(("arbitrary",)),
        name="modulation",
    )(ccol, w_ada, b_ada)


MOD_ACCS = 4


def _deferred_mod_tile(sb_ref, w_ref, b_ref, o_ref):
    d, tn = w_ref.shape
    nl = tn // LANES
    accs = [[None] * MOD_ACCS for _ in range(nl)]
    for kc in range(d // SUBLANES):
        rows = slice(kc * SUBLANES, (kc + 1) * SUBLANES)
        s = sb_ref[rows, :]
        for jj in range(nl):
            p = w_ref[rows, jj * LANES:(jj + 1) * LANES] * s
            a = kc % MOD_ACCS
            accs[jj][a] = p if accs[jj][a] is None else accs[jj][a] + p
    tot = [jnp.sum(functools.reduce(jnp.add, accs[jj]), axis=0, keepdims=True) for jj in range(nl)]
    o_ref[...] = jnp.broadcast_to(jnp.concatenate(tot, axis=1) + b_ref[...], o_ref.shape)


def _ln_mod_kernel(x_ref, c_ref, mod_ref, o_ref, *, n_x_blocks):
    d = x_ref.shape[1]
    i = pl.program_id(0)

    def emit(src_ref, row):
        sh = mod_ref[row:row + 1, 0:d]
        sc = mod_ref[row:row + 1, d:2 * d]
        o_ref[...] = (_ln_rows(src_ref[...]) * (1.0 + sc) + sh).astype(o_ref.dtype)

    @pl.when(i < n_x_blocks)
    def _():
        emit(x_ref, 0)

    @pl.when(i >= n_x_blocks)
    def _():
        emit(c_ref, 1)


def _ln_mod(x, ctx, mod, tr):
    m, d = x.shape
    n_x_blocks = m // tr
    n_c_blocks = ctx.shape[0] // tr
    return pl.pallas_call(
        functools.partial(_ln_mod_kernel, n_x_blocks=n_x_blocks),
        grid=(n_x_blocks + n_c_blocks,),
        in_specs=[
            pl.BlockSpec((tr, d), lambda i: (jnp.minimum(i, n_x_blocks - 1), 0)),
            pl.BlockSpec((tr, d), lambda i: (jnp.maximum(i - n_x_blocks, 0), 0)),
            pl.BlockSpec((SUBLANES, 2 * d), lambda i: (0, 0)),
        ],
        out_specs=pl.BlockSpec((tr, d), lambda i: (i, 0)),
        out_shape=jax.ShapeDtypeStruct((m + ctx.shape[0], d), BF16),
        compiler_params=_cparams(("arbitrary",)),
        name="ln_mod",
    )(x, ctx, mod)


CAST_ROWS = 256
SIDE_ROWS = 64
MM_CHUNK = 256


def _mm_cast_kernel(*refs, n_w, bn, n_blocks, m_blocks, col0, n_cols, epilogue, scale, has_side, head_major):
    it = iter(refs)
    x_ref = next(it)
    w_hbm = [next(it) for _ in range(n_w)]
    side_in = next(it) if has_side else None
    o_ref = next(it)
    side_out = next(it) if has_side else None
    stage = [next(it) for _ in range(n_w)]
    wb = [next(it) for _ in range(n_w)]
    sem = next(it)

    j = pl.program_id(0)
    i = pl.program_id(1)
    k = stage[0].shape[0]
    w_last = n_cols - (n_blocks - 1) * bn

    def slab_copies(jj, width):
        return [pltpu.make_async_copy(w_hbm[a].at[:, pl.ds(col0 + jj * bn, width)],
                                      stage[a].at[:, pl.ds(0, width)], sem.at[a]) for a in range(n_w)]

    def for_slab(jj, fn):
        if w_last == bn:
            for cp in slab_copies(jj, bn):
                fn(cp)
            return

        @pl.when(jj < n_blocks - 1)
        def _():
            for cp in slab_copies(jj, bn):
                fn(cp)

        @pl.when(jj == n_blocks - 1)
        def _():
            for cp in slab_copies(n_blocks - 1, w_last):
                fn(cp)

    @pl.when((j == 0) & (i == 0))
    def _():
        for_slab(j, lambda cp: cp.start())

    @pl.when(i == 0)
    def _():
        for_slab(j, lambda cp: cp.wait())

        def cast_rows(r, carry):
            r0 = pl.multiple_of(r * CAST_ROWS, CAST_ROWS)
            for a in range(n_w):
                wb[a][pl.ds(r0, CAST_ROWS), :] = stage[a][pl.ds(r0, CAST_ROWS), :].astype(BF16)
            return carry

        lax.fori_loop(0, k // CAST_ROWS, cast_rows, 0)

    @pl.when((i == min(1, m_blocks - 1)) & (j + 1 < n_blocks))
    def _():
        for_slab(j + 1, lambda cp: cp.start())

    def compute(n_chunks):
        x = x_ref[...]
        for c in range(n_chunks):
            cols = slice(c * MM_CHUNK, (c + 1) * MM_CHUNK)
            accs = [jnp.dot(x, wb[a][:, cols], preferred_element_type=F32) for a in range(n_w)]
            if epilogue == "plain":
                y = accs[0]
            elif epilogue == "gelu":
                y = _gelu(accs[0])
            elif epilogue == "scale":
                y = accs[0] * scale
            else:
                y = _silu(accs[0]) * accs[1]
            y = y.astype(o_ref.dtype)
            if head_major:
                for hh in range(MM_CHUNK // HEAD_DIM):
                    o_ref[c * (MM_CHUNK // HEAD_DIM) + hh] = y[:, hh * HEAD_DIM:(hh + 1) * HEAD_DIM]
            else:
                o_ref[:, cols] = y

    full_chunks = bn // MM_CHUNK
    last_chunks = -(-w_last // MM_CHUNK)
    if last_chunks == full_chunks:
        compute(full_chunks)
    else:
        @pl.when(j < n_blocks - 1)
        def _():
            compute(full_chunks)

        @pl.when(j == n_blocks - 1)
        def _():
            compute(last_chunks)

    if side_in is not None:
        side_out[...] = side_in[...].astype(side_out.dtype)


def _mm_cast(x, ws, *, bm, bn, n_cols, m_rows=None, col0=0, epilogue="plain", scale=1.0, side=None,
             head_major=False, name):
    m, k = x.shape
    m = m if m_rows is None else m_rows
    n_w = len(ws)
    assert m % bm == 0 and bn % MM_CHUNK == 0 and k % CAST_ROWS == 0
    n_blocks = -(-n_cols // bn)
    m_blocks = m // bm
    in_specs = [pl.BlockSpec((bm, k), lambda j, i: (i, 0))]
    in_specs += [pl.BlockSpec(memory_space=pl.ANY)] * n_w
    if head_major:
        assert n_cols % bn == 0
        out_specs = [pl.BlockSpec((bn // HEAD_DIM, bm, HEAD_DIM), lambda j, i: (j, i, 0))]
        out_shape = [jax.ShapeDtypeStruct((n_cols // HEAD_DIM, m, HEAD_DIM), BF16)]
    else:
        out_specs = [pl.BlockSpec((bm, bn), lambda j, i: (i, j))]
        out_shape = [jax.ShapeDtypeStruct((m, n_cols), BF16)]
    args = [x, *ws]
    if side is not None:
        rows, c = side.shape
        side_blocks = rows // SIDE_ROWS
        assert side_blocks * SIDE_ROWS == rows and side_blocks <= n_blocks * m_blocks
        side_spec = pl.BlockSpec(
            (SIDE_ROWS, c), lambda j, i: (jnp.minimum(j * m_blocks + i, side_blocks - 1), 0))
        in_specs.append(side_spec)
        out_specs.append(side_spec)
        out_shape.append(jax.ShapeDtypeStruct((rows, c), BF16))
        args.append(side)
    kern = functools.partial(
        _mm_cast_kernel, n_w=n_w, bn=bn, n_blocks=n_blocks, m_blocks=m_blocks, col0=col0, n_cols=n_cols,
        epilogue=epilogue, scale=scale, has_side=side is not None, head_major=head_major)
    out = pl.pallas_call(
        kern,
        grid=(n_blocks, m_blocks),
        in_specs=in_specs,
        out_specs=out_specs,
        out_shape=out_shape,
        scratch_shapes=([pltpu.VMEM((k, bn), F32)] * n_w + [pltpu.VMEM((k, bn), BF16)] * n_w
                        + [pltpu.SemaphoreType.DMA((n_w,))]),
        compiler_params=_cparams(("arbitrary", "arbitrary")),
        name=name,
    )(*args)
    return out if side is not None else out[0]


LN_ROWS = 64


def _down_ln_kernel(g_ref, w_ref, x_ref, gate_ref, lg_ref, lb_ref, o_ref, *, k_blocks, k_last):
    kk = pl.program_id(1)

    def part(kw):
        return jnp.dot(g_ref[:, :kw], w_ref[:kw, :], preferred_element_type=F32)

    bk = w_ref.shape[0]

    @pl.when(kk == 0)
    def _():
        o_ref[...] = part(bk)

    @pl.when((kk > 0) & (kk < k_blocks - 1))
    def _():
        o_ref[...] += part(bk)

    @pl.when(kk == k_blocks - 1)
    def _():
        o_ref[...] += part(k_last)
        gate = gate_ref[...]
        lg = lg_ref[...]
        lb = lb_ref[...]

        def ln_chunk(r, carry):
            rows = pl.ds(pl.multiple_of(r * LN_ROWS, LN_ROWS), LN_ROWS)
            z = DEEPNORM_ALPHA * x_ref[rows, :] + gate * o_ref[rows, :]
            o_ref[rows, :] = _ln_rows(z) * lg + lb
            return carry

        lax.fori_loop(0, o_ref.shape[0] // LN_ROWS, ln_chunk, 0)


def _down_ln(g, w, x, gate, lg, lb, *, bm, bk):
    m, k = g.shape
    n = w.shape[1]
    assert m % bm == 0 and w.shape[0] == k
    k_blocks = -(-k // bk)
    k_last = k - (k_blocks - 1) * bk
    assert k_blocks >= 2 and k_last % LANES == 0
    row = pl.BlockSpec((bm, n), lambda i, kk: (i, 0))
    vec = pl.BlockSpec((1, n), lambda i, kk: (0, 0))
    return pl.pallas_call(
        functools.partial(_down_ln_kernel, k_blocks=k_blocks, k_last=k_last),
        grid=(m // bm, k_blocks),
        in_specs=[
            pl.BlockSpec((bm, bk), lambda i, kk: (i, kk)),
            pl.BlockSpec((bk, n), lambda i, kk: (kk, 0)),
            row, vec, vec, vec,
        ],
        out_specs=row,
        out_shape=jax.ShapeDtypeStruct((m, n), F32),
        compiler_params=_cparams(("arbitrary", "arbitrary")),
        name="swiglu_down_ln",
    )(g, w, x, gate, lg, lb)


N_PAIR_ROWS = 2 * WIN_H - 2


def _bias_kernel(rpb_ref, o_ref):
    h = pl.program_id(0)
    n_dcol = 2 * WIN_W - 1
    q = lax.broadcasted_iota(jnp.int32, (GRID_W, 2 * GRID_W), 0)
    l = lax.broadcasted_iota(jnp.int32, (GRID_W, 2 * GRID_W), 1)
    kc = l & (GRID_W - 1)
    hi = l >= GRID_W
    dcol = jnp.clip(kc - q, -(WIN_W - 1), WIN_W - 1) + (WIN_W - 1)
    cs = jnp.clip(q - WIN_W // 2, 0, GRID_W - WIN_W)
    inside = (kc >= cs) & (kc < cs + WIN_W)
    base = h * ((2 * WIN_H - 1) * n_dcol)

    def body(dra, carry):
        acc = jnp.zeros((GRID_W, 2 * GRID_W), F32)
        for d in range(n_dcol):
            lo_v = rpb_ref[base + dra * n_dcol + d]
            hi_v = rpb_ref[base + (dra + 1) * n_dcol + d]
            acc = jnp.where(dcol == d, jnp.where(hi, hi_v, lo_v), acc)
        o_ref[dra] = jnp.where(inside, acc, -jnp.inf)
        return carry

    lax.fori_loop(0, N_PAIR_ROWS, body, 0)


def _bias_table(rpb_flat, n_heads):
    return pl.pallas_call(
        _bias_kernel,
        grid=(n_heads,),
        in_specs=[pl.BlockSpec(memory_space=pltpu.SMEM)],
        out_specs=pl.BlockSpec((None, N_PAIR_ROWS, GRID_W, 2 * GRID_W), lambda h: (h, 0, 0, 0)),
        out_shape=jax.ShapeDtypeStruct((n_heads, N_PAIR_ROWS, GRID_W, 2 * GRID_W), F32),
        compiler_params=_cparams(("arbitrary",)),
        name="bias_table",
    )(rpb_flat)


ROWS_PER_STEP = 32


def _attn_kernel(q_ref, k_ref, v_ref, kc_ref, vc_ref, tp_ref, sb_ref, wa_ref, ba_ref, o_ref, mod_ref,
                 *, n_rows):
    _deferred_mod_tile(sb_ref, wa_ref, ba_ref, mod_ref)

    jb = pl.program_id(1)
    n_lat = WIN_H * GRID_W
    q = q_ref[...]
    kc = kc_ref[...]
    vc = vc_ref[...]
    dn_t = (((1,), (1,)), ((), ()))
    s_ctx = lax.dot_general(q, kc, dn_t, preferred_element_type=F32)
    starts, s_rows = [], []
    for t in range(ROWS_PER_STEP):
        r = jb * ROWS_PER_STEP + t
        ks = jnp.clip(r - WIN_H // 2, 0, n_rows - WIN_H)
        shift = r - ks
        start = pl.multiple_of(ks * GRID_W, GRID_W)
        starts.append(start)
        kw = k_ref[pl.ds(start, n_lat), :]
        qt = q[t * GRID_W:(t + 1) * GRID_W]
        s = lax.dot_general(qt, kw, dn_t, preferred_element_type=F32)
        bias = jnp.concatenate(
            [tp_ref[2 * p - shift + (WIN_H - 1)] for p in range(WIN_H // 2)], axis=1)
        s_rows.append(s + bias)
    s_lat = jnp.concatenate(s_rows, axis=0)
    tiles = [s_lat[:, c * LANES:(c + 1) * LANES] for c in range(n_lat // LANES)]
    tiles += [s_ctx[:, c * LANES:(c + 1) * LANES] for c in range(s_ctx.shape[1] // LANES)]
    m = jnp.max(functools.reduce(jnp.maximum, tiles), axis=1, keepdims=True)
    e = jnp.exp((s_lat - m).astype(BF16))
    ec = jnp.exp((s_ctx - m).astype(BF16))
    ones = jnp.ones((n_lat, HEAD_DIM), BF16)
    vc_aug = jnp.concatenate([vc, ones[:vc.shape[0]]], axis=1)
    oc = jnp.dot(ec, vc_aug, preferred_element_type=F32)
    for t in range(ROWS_PER_STEP):
        rows = slice(t * GRID_W, (t + 1) * GRID_W)
        vw_aug = jnp.concatenate([v_ref[pl.ds(starts[t], n_lat), :], ones], axis=1)
        o = jnp.dot(e[rows], vw_aug, preferred_element_type=F32) + oc[rows]
        o_ref[rows, :] = (o[:, :HEAD_DIM] / o[:, HEAD_DIM:]).astype(o_ref.dtype)


def _attention(q, kv, tp, sb0, w_ada, b_ada, mod_col0):
    n_heads, s, _ = q.shape
    n_ctx = kv.shape[1] - s
    assert s % n_ctx == 0
    ctx_blk = s // n_ctx
    n_rows = s // GRID_W
    tq = ROWS_PER_STEP * GRID_W
    bph = n_rows // ROWS_PER_STEP
    d, n_mod = w_ada.shape
    n_side = n_mod - mod_col0
    tn = n_side // (n_heads * bph)
    assert tn * n_heads * bph == n_side and tn % LANES == 0 and mod_col0 % tn == 0
    blk0 = mod_col0 // tn
    return pl.pallas_call(
        functools.partial(_attn_kernel, n_rows=n_rows),
        grid=(n_heads, bph),
        in_specs=[
            pl.BlockSpec((None, tq, HEAD_DIM), lambda h, j: (h, j, 0)),
            pl.BlockSpec((None, s, HEAD_DIM), lambda h, j: (h, 0, 0)),
            pl.BlockSpec((None, s, HEAD_DIM), lambda h, j: (n_heads + h, 0, 0)),
            pl.BlockSpec((None, n_ctx, HEAD_DIM), lambda h, j: (h, ctx_blk, 0)),
            pl.BlockSpec((None, n_ctx, HEAD_DIM), lambda h, j: (n_heads + h, ctx_blk, 0)),
            pl.BlockSpec((None, N_PAIR_ROWS, GRID_W, 2 * GRID_W), lambda h, j: (h, 0, 0, 0)),
            pl.BlockSpec((d, LANES), lambda h, j: (0, 0)),
            pl.BlockSpec((d, tn), lambda h, j: (0, blk0 + h * bph + j)),
            pl.BlockSpec((1, tn), lambda h, j: (0, blk0 + h * bph + j)),
        ],
        out_specs=[pl.BlockSpec((None, tq, HEAD_DIM), lambda h, j: (h, j, 0)),
                   pl.BlockSpec((SUBLANES, tn), lambda h, j: (0, h * bph + j))],
        out_shape=[jax.ShapeDtypeStruct((n_heads, s, HEAD_DIM), BF16),
                   jax.ShapeDtypeStruct((SUBLANES, n_side), F32)],
        compiler_params=_cparams(("arbitrary", "arbitrary")),
        name="nbr_attention",
    )(q, kv, kv, kv, kv, tp, sb0, w_ada, b_ada)


SGU_CHUNKS_PER_STEP = 4


def _sgu_merge_kernel(gu_ref, gv_ref, ob_ref, ws_ref, bst_ref, lg_ref, lb_ref, gn_ref, o_ref, oa_ref,
                      *, n_heads, w_sgu):
    v = _ln_rows(gv_ref[...].astype(F32)) * lg_ref[...] + lb_ref[...]
    v = v.astype(BF16)
    bst = bst_ref[...]
    for h in range(n_heads):
        cols = slice(h * HEAD_DIM, (h + 1) * HEAD_DIM)
        vh = jnp.concatenate(
            [v[c * CHUNK:(c + 1) * CHUNK, cols] for c in range(SGU_CHUNKS_PER_STEP)], axis=1)
        mixed = jnp.dot(ws_ref[h], vh, preferred_element_type=F32) + bst[:, h:h + 1]
        for c in range(SGU_CHUNKS_PER_STEP):
            rows = slice(c * CHUNK, (c + 1) * CHUNK)
            oa_ref[rows, cols] = gu_ref[rows, cols].astype(F32) * mixed[:, c * HEAD_DIM:(c + 1) * HEAD_DIM]
    gn = gn_ref[...]
    o_ref[:, :w_sgu] = (_rms_rows(oa_ref[...]) * gn[:, :w_sgu]).astype(o_ref.dtype)
    ob = jnp.concatenate([ob_ref[h] for h in range(ob_ref.shape[0])], axis=1).astype(F32)
    o_ref[:, w_sgu:] = (_rms_rows(ob) * gn[:, w_sgu:]).astype(o_ref.dtype)


def _sgu_merge(p, ob, ws, bst, lg, lb, gn, *, n_heads):
    s = p.shape[0]
    w_sgu = n_heads * HEAD_DIM
    n_na_heads = ob.shape[0]
    w_na = n_na_heads * HEAD_DIM
    tr = SGU_CHUNKS_PER_STEP * CHUNK
    return pl.pallas_call(
        functools.partial(_sgu_merge_kernel, n_heads=n_heads, w_sgu=w_sgu),
        grid=(s // tr,),
        in_specs=[
            pl.BlockSpec((tr, w_sgu), lambda i: (i, 0)),
            pl.BlockSpec((tr, w_sgu), lambda i: (i, 1)),
            pl.BlockSpec((n_na_heads, tr, HEAD_DIM), lambda i: (0, i, 0)),
            pl.BlockSpec((n_heads, CHUNK, CHUNK), lambda i: (0, 0, 0)),
            pl.BlockSpec((CHUNK, n_heads), lambda i: (0, 0)),
            pl.BlockSpec((1, w_sgu), lambda i: (0, 0)),
            pl.BlockSpec((1, w_sgu), lambda i: (0, 0)),
            pl.BlockSpec((1, w_sgu + w_na), lambda i: (0, 0)),
        ],
        out_specs=pl.BlockSpec((tr, w_sgu + w_na), lambda i: (i, 0)),
        out_shape=jax.ShapeDtypeStruct((s, w_sgu + w_na), BF16),
        scratch_shapes=[pltpu.VMEM((tr, w_sgu), F32)],
        compiler_params=_cparams(("parallel",)),
        name="sgu_merge",
    )(p, p, ob, ws, bst, lg, lb, gn)


def _res_ln_mod_kernel(x_ref, y_ref, g_ref, lg_ref, lb_ref, sc_ref, sh_ref, x1_ref, h_ref):
    xn = _ln_rows(x_ref[...])
    z = DEEPNORM_ALPHA * xn + g_ref[...] * y_ref[...].astype(F32)
    x1 = _ln_rows(z) * lg_ref[...] + lb_ref[...]
    x1_ref[...] = x1
    h_ref[...] = (x1 * (1.0 + sc_ref[...]) + sh_ref[...]).astype(h_ref.dtype)


def _res_ln_mod(x, y, gate, lg, lb, sc, sh, tr):
    m, d = x.shape
    row = pl.BlockSpec((tr, d), lambda i: (i, 0))
    vec = pl.BlockSpec((1, d), lambda i: (0, 0))
    return pl.pallas_call(
        _res_ln_mod_kernel,
        grid=(m // tr,),
        in_specs=[row, row, vec, vec, vec, vec, vec],
        out_specs=[row, row],
        out_shape=[jax.ShapeDtypeStruct((m, d), F32), jax.ShapeDtypeStruct((m, d), BF16)],
        compiler_params=_cparams(("parallel",)),
        name="res_ln_mod",
    )(x, y, gate, lg, lb, sc, sh)


def kernel(x, c, ctx, c_ctx, w_ada, b_ada, w_in, w_s, b_s, sgu_g, sgu_b, rpb, gn_g, w_o, ln_g, ln_b,
           w_gate, w_up, w_down):
    batch, seq, d = x.shape
    assert batch == 1 and w_ada.shape[0] == DEPTH
    n_sgu_heads = w_s.shape[1]
    n_na_heads = rpb.shape[1]
    w_sgu = n_sgu_heads * HEAD_DIM
    w_na = n_na_heads * HEAD_DIM
    d_ff = w_gate.shape[2]
    assert w_in.shape[2] == 2 * w_sgu + 3 * w_na and w_sgu == w_na

    x2 = x.reshape(seq, d)
    ctx2 = ctx.reshape(ctx.shape[1], d)

    ccol = jnp.stack([c.reshape(d), c_ctx], axis=1)
    b_ada2 = b_ada[0].reshape(1, 6 * d)
    mod, sb0 = _modulation(ccol, w_ada[0], b_ada2, 2 * d)

    h = _ln_mod(x2, ctx2, mod, 256)

    guv = _mm_cast(h, [w_in[0]], m_rows=seq, bm=1024, bn=1024, n_cols=2 * w_sgu, epilogue="gelu",
                   name="in_proj_uv")
    q = _mm_cast(h, [w_in[0]], m_rows=seq, bm=1024, bn=1024, n_cols=w_na, col0=2 * w_sgu,
                 epilogue="scale", scale=HEAD_DIM ** -0.5, head_major=True, name="in_proj_q")
    kv = _mm_cast(h, [w_in[0]], bm=768, bn=1024, n_cols=2 * w_na, col0=2 * w_sgu + w_na,
                  head_major=True, name="in_proj_kv")

    tp = _bias_table(rpb[0].reshape(-1), n_na_heads)
    ob, mod_b = _attention(q, kv, tp, sb0, w_ada[0], b_ada2, 2 * d)
    g1, sh2, sc2, g2 = [mod_b[0:1, i * d:(i + 1) * d] for i in range(4)]

    merged = _sgu_merge(guv, ob, w_s[0].astype(BF16), b_s[0].T, sgu_g, sgu_b, gn_g, n_heads=n_sgu_heads)

    y = _mm_cast(merged, [w_o[0]], bm=1024, bn=1024, n_cols=d, name="out_proj")
    x1, h2 = _res_ln_mod(x2, y, g1, ln_g[0, 0:1], ln_b[0, 0:1], sc2, sh2, 256)

    g, wd = _mm_cast(h2, [w_gate[0], w_up[0]], bm=1024, bn=512, n_cols=d_ff, epilogue="swiglu",
                     side=w_down[0], name="swiglu_up")
    out = _down_ln(g, wd, x1, g2, ln_g[0, 1:2], ln_b[0, 1:2], bm=512, bk=1024)
    return out.reshape(batch, seq, d)
```

```python
import functools
import math

import jax
import jax.numpy as jnp
from jax import lax
from jax.experimental import pallas as pl
from jax.experimental.pallas import tpu as pltpu

F32 = jnp.float32
BF16 = jnp.bfloat16

GRID_W = 64
HEAD_DIM = 128
CHUNK = 128
WIN_H = 8
WIN_W = 16
LN_EPS = 1e-5
DEPTH = 1
DEEPNORM_ALPHA = (2.0 * DEPTH) ** 0.25

LANES = 128
SUBLANES = 8
VMEM_LIMIT = 56 * 1024 * 1024


def _cparams(sem):
    return pltpu.CompilerParams(dimension_semantics=sem, vmem_limit_bytes=VMEM_LIMIT)


def _gelu(x):
    return 0.5 * x * (1.0 + lax.erf(x * (1.0 / math.sqrt(2.0))))


def _silu(x):
    return x * jax.nn.sigmoid(x)


def _ln_rows(x):
    mu = jnp.mean(x, axis=-1, keepdims=True)
    xc = x - mu
    var = jnp.mean(xc * xc, axis=-1, keepdims=True)
    return xc * lax.rsqrt(var + LN_EPS)


def _rms_rows(x):
    return x * lax.rsqrt(jnp.mean(x * x, axis=-1, keepdims=True) + LN_EPS)


def _mod_kernel(ccol_ref, w_ref, b_ref, o_ref, sb0_ref, sb_ref, *, tn):
    d = w_ref.shape[0]

    @pl.when(pl.program_id(0) == 0)
    def _():
        s = _silu(ccol_ref[...])
        sb_ref[0] = jnp.broadcast_to(s[:, 0:1], (d, LANES))
        sb_ref[1] = jnp.broadcast_to(s[:, 1:2], (d, LANES))
        sb0_ref[...] = sb_ref[0]

    nl = tn // LANES

    def body(kc, accs):
        k0 = pl.multiple_of(kc * SUBLANES, SUBLANES)
        s0 = sb_ref[0, pl.ds(k0, SUBLANES), :]
        s1 = sb_ref[1, pl.ds(k0, SUBLANES), :]
        w = w_ref[pl.ds(k0, SUBLANES), :]
        out = []
        for j in range(nl):
            wj = w[:, j * LANES:(j + 1) * LANES]
            out.append(accs[2 * j] + wj * s0)
            out.append(accs[2 * j + 1] + wj * s1)
        return tuple(out)

    zero = jnp.zeros((SUBLANES, LANES), F32)
    accs = lax.fori_loop(0, d // SUBLANES, body, (zero,) * (2 * nl), unroll=8)
    r0 = jnp.concatenate([jnp.sum(accs[2 * j], axis=0, keepdims=True) for j in range(nl)], axis=1)
    r1 = jnp.concatenate([jnp.sum(accs[2 * j + 1], axis=0, keepdims=True) for j in range(nl)], axis=1)
    row = lax.broadcasted_iota(jnp.int32, (SUBLANES, tn), 0)
    b = b_ref[...]
    o_ref[...] = jnp.where(row == 0, r0 + b, jnp.where(row == 1, r1 + b, 0.0))


def _modulation(ccol, w_ada, b_ada, n):
    d = w_ada.shape[0]
    tn = 512
    return pl.pallas_call(
        functools.partial(_mod_kernel, tn=tn),
        grid=(n // tn,),
        in_specs=[
            pl.BlockSpec((d, 2), lambda j: (0, 0)),
            pl.BlockSpec((d, tn), lambda j: (0, j)),
            pl.BlockSpec((1, tn), lambda j: (0, j)),
        ],
        out_specs=[pl.BlockSpec((SUBLANES, tn), lambda j: (0, j)),
                   pl.BlockSpec((d, LANES), lambda j: (0, 0))],
        out_shape=[jax.ShapeDtypeStruct((SUBLANES, n), F32), jax.ShapeDtypeStruct((d, LANES), F32)],
        scratch_shapes=[pltpu.VMEM((2, d, LANES), F32)],
        compiler_params=_cparams(("arbitrary",)),
        name="modulation",
    )(ccol, w_ada, b_ada)


MOD_ACCS = 4


def _deferred_mod_tile(sb_ref, w_ref, b_ref, o_ref):
    d, tn = w_ref.shape
    nl = tn // LANES
    accs = [[None] * MOD_ACCS for _ in range(nl)]
    for kc in range(d // SUBLANES):
        rows = slice(kc * SUBLANES, (kc + 1) * SUBLANES)
        s = sb_ref[rows, :]
        for jj in range(nl):
            p = w_ref[rows, jj * LANES:(jj + 1) * LANES] * s
            a = kc % MOD_ACCS
            accs[jj][a] = p if accs[jj][a] is None else accs[jj][a] + p
    tot = [jnp.sum(functools.reduce(jnp.add, accs[jj]), axis=0, keepdims=True) for jj in range(nl)]
    o_ref[...] = jnp.broadcast_to(jnp.concatenate(tot, axis=1) + b_ref[...], o_ref.shape)


def _ln_mod_kernel(x_ref, c_ref, mod_ref, o_ref, *, n_x_blocks):
    d = x_ref.shape[1]
    i = pl.program_id(0)

    def emit(src_ref, row):
        sh = mod_ref[row:row + 1, 0:d]
        sc = mod_ref[row:row + 1, d:2 * d]
        o_ref[...] = (_ln_rows(src_ref[...]) * (1.0 + sc) + sh).astype(o_ref.dtype)

    @pl.when(i < n_x_blocks)
    def _():
        emit(x_ref, 0)

    @pl.when(i >= n_x_blocks)
    def _():
        emit(c_ref, 1)


def _ln_mod(x, ctx, mod, tr):
    m, d = x.shape
    n_x_blocks = m // tr
    n_c_blocks = ctx.shape[0] // tr
    return pl.pallas_call(
        functools.partial(_ln_mod_kernel, n_x_blocks=n_x_blocks),
        grid=(n_x_blocks + n_c_blocks,),
        in_specs=[
            pl.BlockSpec((tr, d), lambda i: (jnp.minimum(i, n_x_blocks - 1), 0)),
            pl.BlockSpec((tr, d), lambda i: (jnp.maximum(i - n_x_blocks, 0), 0)),
            pl.BlockSpec((SUBLANES, 2 * d), lambda i: (0, 0)),
        ],
        out_specs=pl.BlockSpec((tr, d), lambda i: (i, 0)),
        out_shape=jax.ShapeDtypeStruct((m + ctx.shape[0], d), BF16),
        compiler_params=_cparams(("arbitrary",)),
        name="ln_mod",
    )(x, ctx, mod)


WEIGHT_CHUNKS = 8
SIDE_ROWS = 64
MM_CHUNK = 256


def _mm_cast_kernel(*refs, n_w, bn, n_blocks, m_blocks, col0, n_cols, epilogue, scale, has_side, head_major):
    it = iter(refs)
    x_ref = next(it)
    w_hbm = [next(it) for _ in range(n_w)]
    side_in = next(it) if has_side else None
    o_ref = next(it)
    side_out = next(it) if has_side else None
    stage = [next(it) for _ in range(n_w)]
    wb = [next(it) for _ in range(n_w)]
    sem = next(it)

    j = pl.program_id(0)
    i = pl.program_id(1)
    g = j * m_blocks + i
    ck = stage[0].shape[1]
    n_chunks = wb[0].shape[1] // ck
    w_last = n_cols - (n_blocks - 1) * bn
    last_col = col0 + n_cols - bn

    def chunk_copies(jj, c, buf):
        col = jnp.minimum(col0 + jnp.minimum(jj, n_blocks - 1) * bn, last_col)
        rows = pl.ds(pl.multiple_of(jnp.minimum(c, n_chunks - 1) * ck, ck), ck)
        return [pltpu.make_async_copy(w_hbm[a].at[rows, pl.ds(col, bn)], stage[a].at[buf], sem.at[a, buf])
                for a in range(n_w)]

    def cast_chunk(buf, slot, c):
        rows = pl.ds(pl.multiple_of(jnp.minimum(c, n_chunks - 1) * ck, ck), ck)
        for a in range(n_w):
            wb[a][slot, rows, :] = stage[a][buf].astype(BF16)

    @pl.when(g == 0)
    def _():
        for c in range(n_chunks):
            for cp in chunk_copies(0, c, c % 2):
                cp.start()
            for cp in chunk_copies(0, c, c % 2):
                cp.wait()
            cast_chunk(c % 2, 0, c)
        for cp in chunk_copies(1, 0, 0):
            cp.start()

    buf = g % 2
    for cp in chunk_copies(j + 1, i, buf):
        cp.wait()

    def compute(n_cc, w_off):
        cast_chunk(buf, (j + 1) % 2, i)
        x = x_ref[...]
        slot = j % 2
        for c in range(n_cc):
            cols = slice(c * MM_CHUNK, (c + 1) * MM_CHUNK)
            wcols = slice(w_off + c * MM_CHUNK, w_off + (c + 1) * MM_CHUNK)
            accs = [jnp.dot(x, wb[a][slot, :, wcols], preferred_element_type=F32) for a in range(n_w)]
            if epilogue == "plain":
                y = accs[0]
            elif epilogue == "gelu":
                y = _gelu(accs[0])
            elif epilogue == "scale":
                y = accs[0] * scale
            else:
                y = _silu(accs[0]) * accs[1]
            y = y.astype(o_ref.dtype)
            if head_major:
                for hh in range(MM_CHUNK // HEAD_DIM):
                    o_ref[c * (MM_CHUNK // HEAD_DIM) + hh] = y[:, hh * HEAD_DIM:(hh + 1) * HEAD_DIM]
            else:
                o_ref[:, cols] = y

    if w_last == bn:
        compute(bn // MM_CHUNK, 0)
    else:
        @pl.when(j < n_blocks - 1)
        def _():
            compute(bn // MM_CHUNK, 0)

        @pl.when(j == n_blocks - 1)
        def _():
            compute(-(-w_last // MM_CHUNK), bn - w_last)

    if side_in is not None:
        side_out[...] = side_in[...].astype(side_out.dtype)

    @pl.when(g + 1 < n_blocks * m_blocks)
    def _():
        wrap = i == m_blocks - 1
        for cp in chunk_copies(j + 1 + wrap.astype(jnp.int32), jnp.where(wrap, 0, i + 1), 1 - buf):
            cp.start()


def _mm_cast(x, ws, *, bm, bn, n_cols, m_rows=None, col0=0, epilogue="plain", scale=1.0, side=None,
             head_major=False, name):
    m, k = x.shape
    m = m if m_rows is None else m_rows
    n_w = len(ws)
    n_blocks = -(-n_cols // bn)
    m_blocks = m // bm
    ck = k // WEIGHT_CHUNKS
    assert m % bm == 0 and bn % MM_CHUNK == 0 and ck * WEIGHT_CHUNKS == k and ck % 16 == 0
    assert m_blocks >= WEIGHT_CHUNKS and n_cols >= bn and (n_cols % bn) % LANES == 0
    in_specs = [pl.BlockSpec((bm, k), lambda j, i: (i, 0))]
    in_specs += [pl.BlockSpec(memory_space=pl.ANY)] * n_w
    if head_major:
        assert n_cols % bn == 0
        out_specs = [pl.BlockSpec((bn // HEAD_DIM, bm, HEAD_DIM), lambda j, i: (j, i, 0))]
        out_shape = [jax.ShapeDtypeStruct((n_cols // HEAD_DIM, m, HEAD_DIM), BF16)]
    else:
        out_specs = [pl.BlockSpec((bm, bn), lambda j, i: (i, j))]
        out_shape = [jax.ShapeDtypeStruct((m, n_cols), BF16)]
    args = [x, *ws]
    if side is not None:
        rows, c = side.shape
        side_blocks = rows // SIDE_ROWS
        assert side_blocks * SIDE_ROWS == rows and side_blocks <= n_blocks * m_blocks
        side_spec = pl.BlockSpec(
            (SIDE_ROWS, c), lambda j, i: (jnp.minimum(j * m_blocks + i, side_blocks - 1), 0))
        in_specs.append(side_spec)
        out_specs.append(side_spec)
        out_shape.append(jax.ShapeDtypeStruct((rows, c), BF16))
        args.append(side)
    kern = functools.partial(
        _mm_cast_kernel, n_w=n_w, bn=bn, n_blocks=n_blocks, m_blocks=m_blocks, col0=col0, n_cols=n_cols,
        epilogue=epilogue, scale=scale, has_side=side is not None, head_major=head_major)
    out = pl.pallas_call(
        kern,
        grid=(n_blocks, m_blocks),
        in_specs=in_specs,
        out_specs=out_specs,
        out_shape=out_shape,
        scratch_shapes=([pltpu.VMEM((2, ck, bn), F32)] * n_w + [pltpu.VMEM((2, k, bn), BF16)] * n_w
                        + [pltpu.SemaphoreType.DMA((n_w, 2))]),
        compiler_params=_cparams(("arbitrary", "arbitrary")),
        name=name,
    )(*args)
    return out if side is not None else out[0]


LN_ROWS = 64


def _down_ln_kernel(g_ref, w_ref, x_ref, gate_ref, lg_ref, lb_ref, o_ref, *, k_blocks, k_last):
    kk = pl.program_id(1)

    def part(kw):
        return jnp.dot(g_ref[:, :kw], w_ref[:kw, :], preferred_element_type=F32)

    bk = w_ref.shape[0]

    @pl.when(kk == 0)
    def _():
        o_ref[...] = part(bk)

    @pl.when((kk > 0) & (kk < k_blocks - 1))
    def _():
        o_ref[...] += part(bk)

    @pl.when(kk == k_blocks - 1)
    def _():
        o_ref[...] += part(k_last)
        gate = gate_ref[...]
        lg = lg_ref[...]
        lb = lb_ref[...]

        def ln_chunk(r, carry):
            rows = pl.ds(pl.multiple_of(r * LN_ROWS, LN_ROWS), LN_ROWS)
            z = DEEPNORM_ALPHA * x_ref[rows, :] + gate * o_ref[rows, :]
            o_ref[rows, :] = _ln_rows(z) * lg + lb
            return carry

        lax.fori_loop(0, o_ref.shape[0] // LN_ROWS, ln_chunk, 0)


def _down_ln(g, w, x, gate, lg, lb, *, bm, bk):
    m, k = g.shape
    n = w.shape[1]
    assert m % bm == 0 and w.shape[0] == k
    k_blocks = -(-k // bk)
    k_last = k - (k_blocks - 1) * bk
    assert k_blocks >= 2 and k_last % LANES == 0
    row = pl.BlockSpec((bm, n), lambda i, kk: (i, 0))
    vec = pl.BlockSpec((1, n), lambda i, kk: (0, 0))
    return pl.pallas_call(
        functools.partial(_down_ln_kernel, k_blocks=k_blocks, k_last=k_last),
        grid=(m // bm, k_blocks),
        in_specs=[
            pl.BlockSpec((bm, bk), lambda i, kk: (i, kk)),
            pl.BlockSpec((bk, n), lambda i, kk: (kk, 0)),
            row, vec, vec, vec,
        ],
        out_specs=row,
        out_shape=jax.ShapeDtypeStruct((m, n), F32),
        compiler_params=_cparams(("arbitrary", "arbitrary")),
        name="swiglu_down_ln",
    )(g, w, x, gate, lg, lb)


N_PAIR_ROWS = 2 * WIN_H - 2


def _bias_kernel(rpb_ref, o_ref):
    h = pl.program_id(0)
    n_dcol = 2 * WIN_W - 1
    q = lax.broadcasted_iota(jnp.int32, (GRID_W, 2 * GRID_W), 0)
    l = lax.broadcasted_iota(jnp.int32, (GRID_W, 2 * GRID_W), 1)
    kc = l & (GRID_W - 1)
    hi = l >= GRID_W
    dcol = jnp.clip(kc - q, -(WIN_W - 1), WIN_W - 1) + (WIN_W - 1)
    cs = jnp.clip(q - WIN_W // 2, 0, GRID_W - WIN_W)
    inside = (kc >= cs) & (kc < cs + WIN_W)
    base = h * ((2 * WIN_H - 1) * n_dcol)

    def body(dra, carry):
        acc = jnp.zeros((GRID_W, 2 * GRID_W), F32)
        for d in range(n_dcol):
            lo_v = rpb_ref[base + dra * n_dcol + d]
            hi_v = rpb_ref[base + (dra + 1) * n_dcol + d]
            acc = jnp.where(dcol == d, jnp.where(hi, hi_v, lo_v), acc)
        o_ref[dra] = jnp.where(inside, acc, -jnp.inf)
        return carry

    lax.fori_loop(0, N_PAIR_ROWS, body, 0)


def _bias_table(rpb_flat, n_heads):
    return pl.pallas_call(
        _bias_kernel,
        grid=(n_heads,),
        in_specs=[pl.BlockSpec(memory_space=pltpu.SMEM)],
        out_specs=pl.BlockSpec((None, N_PAIR_ROWS, GRID_W, 2 * GRID_W), lambda h: (h, 0, 0, 0)),
        out_shape=jax.ShapeDtypeStruct((n_heads, N_PAIR_ROWS, GRID_W, 2 * GRID_W), F32),
        compiler_params=_cparams(("arbitrary",)),
        name="bias_table",
    )(rpb_flat)


ROWS_PER_STEP = 32


def _attn_kernel(q_ref, k_ref, v_ref, kc_ref, vc_ref, tp_ref, sb_ref, wa_ref, ba_ref, o_ref, mod_ref,
                 kt_ref, *, n_rows):
    jb = pl.program_id(1)
    s_keys = k_ref.shape[0]

    @pl.when(jb == 0)
    def _():
        kt_ref[0] = k_ref[...].T
        n_shift = s_keys - LANES
        kt_ref[1, :, :n_shift] = k_ref[GRID_W:GRID_W + n_shift, :].T

    _deferred_mod_tile(sb_ref, wa_ref, ba_ref, mod_ref)

    n_lat = WIN_H * GRID_W
    q = q_ref[...]
    kc = kc_ref[...]
    vc = vc_ref[...]
    dn_t = (((1,), (1,)), ((), ()))
    s_ctx = lax.dot_general(q, kc, dn_t, preferred_element_type=F32)
    starts, s_rows = [], []
    for t in range(ROWS_PER_STEP):
        r = jb * ROWS_PER_STEP + t
        ks = jnp.clip(r - WIN_H // 2, 0, n_rows - WIN_H)
        shift = r - ks
        start = pl.multiple_of(ks * GRID_W, GRID_W)
        starts.append(start)
        lane0 = pl.multiple_of((ks // 2) * LANES, LANES)
        kw_t = kt_ref[ks % 2, :, pl.ds(lane0, n_lat)]
        qt = q[t * GRID_W:(t + 1) * GRID_W]
        s = jnp.dot(qt, kw_t, preferred_element_type=F32)
        bias = jnp.concatenate(
            [tp_ref[2 * p - shift + (WIN_H - 1)] for p in range(WIN_H // 2)], axis=1)
        s_rows.append(s + bias)
    s_lat = jnp.concatenate(s_rows, axis=0)
    tiles = [s_lat[:, c * LANES:(c + 1) * LANES] for c in range(n_lat // LANES)]
    tiles += [s_ctx[:, c * LANES:(c + 1) * LANES] for c in range(s_ctx.shape[1] // LANES)]
    m = jnp.max(functools.reduce(jnp.maximum, tiles), axis=1, keepdims=True)
    e = jnp.exp((s_lat - m).astype(BF16))
    ec = jnp.exp((s_ctx - m).astype(BF16))
    ones = jnp.ones((n_lat, HEAD_DIM), BF16)
    vc_aug = jnp.concatenate([vc, ones[:vc.shape[0]]], axis=1)
    oc = jnp.dot(ec, vc_aug, preferred_element_type=F32)
    den = jnp.dot(e, ones, preferred_element_type=F32) + oc[:, HEAD_DIM:]
    for t in range(ROWS_PER_STEP):
        rows = slice(t * GRID_W, (t + 1) * GRID_W)
        vw = v_ref[pl.ds(starts[t], n_lat), :]
        o = jnp.dot(e[rows], vw, preferred_element_type=F32) + oc[rows, :HEAD_DIM]
        o_ref[rows, :] = (o / den[rows]).astype(o_ref.dtype)


def _attention(q, kv, tp, sb0, w_ada, b_ada, mod_col0):
    n_heads, s, _ = q.shape
    n_ctx = kv.shape[1] - s
    assert s % n_ctx == 0
    ctx_blk = s // n_ctx
    n_rows = s // GRID_W
    tq = ROWS_PER_STEP * GRID_W
    bph = n_rows // ROWS_PER_STEP
    d, n_mod = w_ada.shape
    n_side = n_mod - mod_col0
    tn = n_side // (n_heads * bph)
    assert tn * n_heads * bph == n_side and tn % LANES == 0 and mod_col0 % tn == 0
    blk0 = mod_col0 // tn
    return pl.pallas_call(
        functools.partial(_attn_kernel, n_rows=n_rows),
        grid=(n_heads, bph),
        in_specs=[
            pl.BlockSpec((None, tq, HEAD_DIM), lambda h, j: (h, j, 0)),
            pl.BlockSpec((None, s, HEAD_DIM), lambda h, j: (h, 0, 0)),
            pl.BlockSpec((None, s, HEAD_DIM), lambda h, j: (n_heads + h, 0, 0)),
            pl.BlockSpec((None, n_ctx, HEAD_DIM), lambda h, j: (h, ctx_blk, 0)),
            pl.BlockSpec((None, n_ctx, HEAD_DIM), lambda h, j: (n_heads + h, ctx_blk, 0)),
            pl.BlockSpec((None, N_PAIR_ROWS, GRID_W, 2 * GRID_W), lambda h, j: (h, 0, 0, 0)),
            pl.BlockSpec((d, LANES), lambda h, j: (0, 0)),
            pl.BlockSpec((d, tn), lambda h, j: (0, blk0 + h * bph + j)),
            pl.BlockSpec((1, tn), lambda h, j: (0, blk0 + h * bph + j)),
        ],
        out_specs=[pl.BlockSpec((None, tq, HEAD_DIM), lambda h, j: (h, j, 0)),
                   pl.BlockSpec((SUBLANES, tn), lambda h, j: (0, h * bph + j))],
        out_shape=[jax.ShapeDtypeStruct((n_heads, s, HEAD_DIM), BF16),
                   jax.ShapeDtypeStruct((SUBLANES, n_side), F32)],
        scratch_shapes=[pltpu.VMEM((2, HEAD_DIM, s), BF16)],
        compiler_params=_cparams(("arbitrary", "arbitrary")),
        name="nbr_attention",
    )(q, kv, kv, kv, kv, tp, sb0, w_ada, b_ada)


SGU_CHUNKS_PER_STEP = 4


def _sgu_merge_kernel(gu_ref, gv_ref, ob_ref, ws_ref, bst_ref, lg_ref, lb_ref, gn_ref, o_ref, oa_ref,
                      *, n_heads, w_sgu):
    v = _ln_rows(gv_ref[...].astype(F32)) * lg_ref[...] + lb_ref[...]
    v = v.astype(BF16)
    bst = bst_ref[...]
    for h in range(n_heads):
        cols = slice(h * HEAD_DIM, (h + 1) * HEAD_DIM)
        vh = jnp.concatenate(
            [v[c * CHUNK:(c + 1) * CHUNK, cols] for c in range(SGU_CHUNKS_PER_STEP)], axis=1)
        mixed = jnp.dot(ws_ref[h], vh, preferred_element_type=F32) + bst[:, h:h + 1]
        for c in range(SGU_CHUNKS_PER_STEP):
            rows = slice(c * CHUNK, (c + 1) * CHUNK)
            oa_ref[rows, cols] = gu_ref[rows, cols].astype(F32) * mixed[:, c * HEAD_DIM:(c + 1) * HEAD_DIM]
    gn = gn_ref[...]
    o_ref[:, :w_sgu] = (_rms_rows(oa_ref[...]) * gn[:, :w_sgu]).astype(o_ref.dtype)
    ob = jnp.concatenate([ob_ref[h] for h in range(ob_ref.shape[0])], axis=1).astype(F32)
    o_ref[:, w_sgu:] = (_rms_rows(ob) * gn[:, w_sgu:]).astype(o_ref.dtype)


def _sgu_merge(p, ob, ws, bst, lg, lb, gn, *, n_heads):
    s = p.shape[0]
    w_sgu = n_heads * HEAD_DIM
    n_na_heads = ob.shape[0]
    w_na = n_na_heads * HEAD_DIM
    tr = SGU_CHUNKS_PER_STEP * CHUNK
    return pl.pallas_call(
        functools.partial(_sgu_merge_kernel, n_heads=n_heads, w_sgu=w_sgu),
        grid=(s // tr,),
        in_specs=[
            pl.BlockSpec((tr, w_sgu), lambda i: (i, 0)),
            pl.BlockSpec((tr, w_sgu), lambda i: (i, 1)),
            pl.BlockSpec((n_na_heads, tr, HEAD_DIM), lambda i: (0, i, 0)),
            pl.BlockSpec((n_heads, CHUNK, CHUNK), lambda i: (0, 0, 0)),
            pl.BlockSpec((CHUNK, n_heads), lambda i: (0, 0)),
            pl.BlockSpec((1, w_sgu), lambda i: (0, 0)),
            pl.BlockSpec((1, w_sgu), lambda i: (0, 0)),
            pl.BlockSpec((1, w_sgu + w_na), lambda i: (0, 0)),
        ],
        out_specs=pl.BlockSpec((tr, w_sgu + w_na), lambda i: (i, 0)),
        out_shape=jax.ShapeDtypeStruct((s, w_sgu + w_na), BF16),
        scratch_shapes=[pltpu.VMEM((tr, w_sgu), F32)],
        compiler_params=_cparams(("parallel",)),
        name="sgu_merge",
    )(p, p, ob, ws, bst, lg, lb, gn)


def _res_ln_mod_kernel(x_ref, y_ref, g_ref, lg_ref, lb_ref, sc_ref, sh_ref, x1_ref, h_ref):
    xn = _ln_rows(x_ref[...])
    z = DEEPNORM_ALPHA * xn + g_ref[...] * y_ref[...].astype(F32)
    x1 = _ln_rows(z) * lg_ref[...] + lb_ref[...]
    x1_ref[...] = x1
    h_ref[...] = (x1 * (1.0 + sc_ref[...]) + sh_ref[...]).astype(h_ref.dtype)


def _res_ln_mod(x, y, gate, lg, lb, sc, sh, tr):
    m, d = x.shape
    row = pl.BlockSpec((tr, d), lambda i: (i, 0))
    vec = pl.BlockSpec((1, d), lambda i: (0, 0))
    return pl.pallas_call(
        _res_ln_mod_kernel,
        grid=(m // tr,),
        in_specs=[row, row, vec, vec, vec, vec, vec],
        out_specs=[row, row],
        out_shape=[jax.ShapeDtypeStruct((m, d), F32), jax.ShapeDtypeStruct((m, d), BF16)],
        compiler_params=_cparams(("parallel",)),
        name="res_ln_mod",
    )(x, y, gate, lg, lb, sc, sh)


def kernel(x, c, ctx, c_ctx, w_ada, b_ada, w_in, w_s, b_s, sgu_g, sgu_b, rpb, gn_g, w_o, ln_g, ln_b,
           w_gate, w_up, w_down):
    batch, seq, d = x.shape
    assert batch == 1 and w_ada.shape[0] == DEPTH
    n_sgu_heads = w_s.shape[1]
    n_na_heads = rpb.shape[1]
    w_sgu = n_sgu_heads * HEAD_DIM
    w_na = n_na_heads * HEAD_DIM
    d_ff = w_gate.shape[2]
    assert w_in.shape[2] == 2 * w_sgu + 3 * w_na and w_sgu == w_na

    x2 = x.reshape(seq, d)
    ctx2 = ctx.reshape(ctx.shape[1], d)

    ccol = jnp.stack([c.reshape(d), c_ctx], axis=1)
    b_ada2 = b_ada[0].reshape(1, 6 * d)
    mod, sb0 = _modulation(ccol, w_ada[0], b_ada2, 2 * d)

    h = _ln_mod(x2, ctx2, mod, 256)

    guv = _mm_cast(h, [w_in[0]], m_rows=seq, bm=1024, bn=1024, n_cols=2 * w_sgu, epilogue="gelu",
                   name="in_proj_uv")
    q = _mm_cast(h, [w_in[0]], m_rows=seq, bm=1024, bn=1024, n_cols=w_na, col0=2 * w_sgu,
                 epilogue="scale", scale=HEAD_DIM ** -0.5, head_major=True, name="in_proj_q")
    kv = _mm_cast(h, [w_in[0]], bm=768, bn=1024, n_cols=2 * w_na, col0=2 * w_sgu + w_na,
                  head_major=True, name="in_proj_kv")

    tp = _bias_table(rpb[0].reshape(-1), n_na_heads)
    ob, mod_b = _attention(q, kv, tp, sb0, w_ada[0], b_ada2, 2 * d)
    g1, sh2, sc2, g2 = [mod_b[0:1, i * d:(i + 1) * d] for i in range(4)]

    merged = _sgu_merge(guv, ob, w_s[0].astype(BF16), b_s[0].T, sgu_g, sgu_b, gn_g, n_heads=n_sgu_heads)

    y = _mm_cast(merged, [w_o[0]], bm=1024, bn=1024, n_cols=d, name="out_proj")
    x1, h2 = _res_ln_mod(x2, y, g1, ln_g[0, 0:1], ln_b[0, 0:1], sc2, sh2, 256)

    g, wd = _mm_cast(h2, [w_gate[0], w_up[0]], bm=1024, bn=512, n_cols=d_ff, epilogue="swiglu",
                     side=w_down[0], name="swiglu_up")
    out = _down_ln(g, wd, x1, g2, ln_g[0, 1:2], ln_b[0, 1:2], bm=512, bk=1024)
    return out.reshape(batch, seq, d)
```

```python
import functools
import math

import jax
import jax.numpy as jnp
from jax import lax
from jax.experimental import pallas as pl
from jax.experimental.pallas import tpu as pltpu

F32 = jnp.float32
BF16 = jnp.bfloat16

GRID_W = 64
HEAD_DIM = 128
CHUNK = 128
WIN_H = 8
WIN_W = 16
LN_EPS = 1e-5
DEPTH = 1
DEEPNORM_ALPHA = (2.0 * DEPTH) ** 0.25

LANES = 128
SUBLANES = 8
VMEM_LIMIT = 56 * 1024 * 1024


def _cparams(sem):
    return pltpu.CompilerParams(dimension_semantics=sem, vmem_limit_bytes=VMEM_LIMIT)


def _gelu(x):
    return 0.5 * x * (1.0 + lax.erf(x * (1.0 / math.sqrt(2.0))))


def _silu(x):
    return x * jax.nn.sigmoid(x)


def _ln_rows(x):
    mu = jnp.mean(x, axis=-1, keepdims=True)
    xc = x - mu
    var = jnp.mean(xc * xc, axis=-1, keepdims=True)
    return xc * lax.rsqrt(var + LN_EPS)


def _rms_rows(x):
    return x * lax.rsqrt(jnp.mean(x * x, axis=-1, keepdims=True) + LN_EPS)


def _mod_kernel(ccol_ref, w_ref, b_ref, o_ref, sb0_ref, sb_ref, *, tn):
    d = w_ref.shape[0]

    @pl.when(pl.program_id(0) == 0)
    def _():
        s = _silu(ccol_ref[...])
        sb_ref[0] = jnp.broadcast_to(s[:, 0:1], (d, LANES))
        sb_ref[1] = jnp.broadcast_to(s[:, 1:2], (d, LANES))
        sb0_ref[...] = sb_ref[0]

    nl = tn // LANES

    def body(kc, accs):
        k0 = pl.multiple_of(kc * SUBLANES, SUBLANES)
        s0 = sb_ref[0, pl.ds(k0, SUBLANES), :]
        s1 = sb_ref[1, pl.ds(k0, SUBLANES), :]
        w = w_ref[pl.ds(k0, SUBLANES), :]
        out = []
        for j in range(nl):
            wj = w[:, j * LANES:(j + 1) * LANES]
            out.append(accs[2 * j] + wj * s0)
            out.append(accs[2 * j + 1] + wj * s1)
        return tuple(out)

    zero = jnp.zeros((SUBLANES, LANES), F32)
    accs = lax.fori_loop(0, d // SUBLANES, body, (zero,) * (2 * nl), unroll=8)
    r0 = jnp.concatenate([jnp.sum(accs[2 * j], axis=0, keepdims=True) for j in range(nl)], axis=1)
    r1 = jnp.concatenate([jnp.sum(accs[2 * j + 1], axis=0, keepdims=True) for j in range(nl)], axis=1)
    row = lax.broadcasted_iota(jnp.int32, (SUBLANES, tn), 0)
    b = b_ref[...]
    o_ref[...] = jnp.where(row == 0, r0 + b, jnp.where(row == 1, r1 + b, 0.0))


def _modulation(ccol, w_ada, b_ada, n):
    d = w_ada.shape[0]
    tn = 512
    return pl.pallas_call(
        functools.partial(_mod_kernel, tn=tn),
        grid=(n // tn,),
        in_specs=[
            pl.BlockSpec((d, 2), lambda j: (0, 0)),
            pl.BlockSpec((d, tn), lambda j: (0, j)),
            pl.BlockSpec((1, tn), lambda j: (0, j)),
        ],
        out_specs=[pl.BlockSpec((SUBLANES, tn), lambda j: (0, j)),
                   pl.BlockSpec((d, LANES), lambda j: (0, 0))],
        out_shape=[jax.ShapeDtypeStruct((SUBLANES, n), F32), jax.ShapeDtypeStruct((d, LANES), F32)],
        scratch_shapes=[pltpu.VMEM((2, d, LANES), F32)],
        compiler_params=_cparams(("arbitrary",)),
        name="modulation",
    )(ccol, w_ada, b_ada)


MOD_ACCS = 4


def _deferred_mod_tile(sb_ref, w_ref, b_ref, o_ref):
    d, tn = w_ref.shape
    nl = tn // LANES
    accs = [[None] * MOD_ACCS for _ in range(nl)]
    for kc in range(d // SUBLANES):
        rows = slice(kc * SUBLANES, (kc + 1) * SUBLANES)
        s = sb_ref[rows, :]
        for jj in range(nl):
            p = w_ref[rows, jj * LANES:(jj + 1) * LANES] * s
            a = kc % MOD_ACCS
            accs[jj][a] = p if accs[jj][a] is None else accs[jj][a] + p
    tot = [jnp.sum(functools.reduce(jnp.add, accs[jj]), axis=0, keepdims=True) for jj in range(nl)]
    o_ref[...] = jnp.broadcast_to(jnp.concatenate(tot, axis=1) + b_ref[...], o_ref.shape)


def _ln_mod_kernel(x_ref, c_ref, mod_ref, o_ref, *, n_x_blocks):
    d = x_ref.shape[1]
    i = pl.program_id(0)

    def emit(src_ref, row):
        sh = mod_ref[row:row + 1, 0:d]
        sc = mod_ref[row:row + 1, d:2 * d]
        o_ref[...] = (_ln_rows(src_ref[...]) * (1.0 + sc) + sh).astype(o_ref.dtype)

    @pl.when(i < n_x_blocks)
    def _():
        emit(x_ref, 0)

    @pl.when(i >= n_x_blocks)
    def _():
        emit(c_ref, 1)


def _ln_mod(x, ctx, mod, tr):
    m, d = x.shape
    n_x_blocks = m // tr
    n_c_blocks = ctx.shape[0] // tr
    return pl.pallas_call(
        functools.partial(_ln_mod_kernel, n_x_blocks=n_x_blocks),
        grid=(n_x_blocks + n_c_blocks,),
        in_specs=[
            pl.BlockSpec((tr, d), lambda i: (jnp.minimum(i, n_x_blocks - 1), 0)),
            pl.BlockSpec((tr, d), lambda i: (jnp.maximum(i - n_x_blocks, 0), 0)),
            pl.BlockSpec((SUBLANES, 2 * d), lambda i: (0, 0)),
        ],
        out_specs=pl.BlockSpec((tr, d), lambda i: (i, 0)),
        out_shape=jax.ShapeDtypeStruct((m + ctx.shape[0], d), BF16),
        compiler_params=_cparams(("arbitrary",)),
        name="ln_mod",
    )(x, ctx, mod)


WEIGHT_CHUNKS = 8
SIDE_ROWS = 64
MM_CHUNK = 256


def _mm_cast_kernel(*refs, n_w, bn, n_blocks, m_blocks, col0, n_cols, epilogue, scale, has_side, head_major):
    it = iter(refs)
    x_ref = next(it)
    w_hbm = [next(it) for _ in range(n_w)]
    side_in = next(it) if has_side else None
    o_ref = next(it)
    side_out = next(it) if has_side else None
    stage = [next(it) for _ in range(n_w)]
    wb = [next(it) for _ in range(n_w)]
    sem = next(it)

    j = pl.program_id(0)
    i = pl.program_id(1)
    g = j * m_blocks + i
    ck = stage[0].shape[1]
    n_chunks = wb[0].shape[1] // ck
    w_last = n_cols - (n_blocks - 1) * bn
    last_col = col0 + n_cols - bn

    def chunk_copies(jj, c, buf):
        col = jnp.minimum(col0 + jnp.minimum(jj, n_blocks - 1) * bn, last_col)
        rows = pl.ds(pl.multiple_of(jnp.minimum(c, n_chunks - 1) * ck, ck), ck)
        return [pltpu.make_async_copy(w_hbm[a].at[rows, pl.ds(col, bn)], stage[a].at[buf], sem.at[a, buf])
                for a in range(n_w)]

    def cast_chunk(buf, slot, c):
        rows = pl.ds(pl.multiple_of(jnp.minimum(c, n_chunks - 1) * ck, ck), ck)
        for a in range(n_w):
            wb[a][slot, rows, :] = stage[a][buf].astype(BF16)

    @pl.when(g == 0)
    def _():
        for c in range(n_chunks):
            for cp in chunk_copies(0, c, c % 2):
                cp.start()
            for cp in chunk_copies(0, c, c % 2):
                cp.wait()
            cast_chunk(c % 2, 0, c)
        for cp in chunk_copies(1, 0, 0):
            cp.start()

    buf = g % 2
    for cp in chunk_copies(j + 1, i, buf):
        cp.wait()

    @pl.when(g + 1 < n_blocks * m_blocks)
    def _():
        wrap = i == m_blocks - 1
        for cp in chunk_copies(j + 1 + wrap.astype(jnp.int32), jnp.where(wrap, 0, i + 1), 1 - buf):
            cp.start()

    def compute(n_cc, w_off):
        cast_chunk(buf, (j + 1) % 2, i)
        x = x_ref[...]
        slot = j % 2
        for c in range(n_cc):
            cols = slice(c * MM_CHUNK, (c + 1) * MM_CHUNK)
            wcols = slice(w_off + c * MM_CHUNK, w_off + (c + 1) * MM_CHUNK)
            accs = [jnp.dot(x, wb[a][slot, :, wcols], preferred_element_type=F32) for a in range(n_w)]
            if epilogue == "plain":
                y = accs[0]
            elif epilogue == "gelu":
                y = _gelu(accs[0])
            elif epilogue == "scale":
                y = accs[0] * scale
            else:
                y = _silu(accs[0]) * accs[1]
            y = y.astype(o_ref.dtype)
            if head_major:
                for hh in range(MM_CHUNK // HEAD_DIM):
                    o_ref[c * (MM_CHUNK // HEAD_DIM) + hh] = y[:, hh * HEAD_DIM:(hh + 1) * HEAD_DIM]
            else:
                o_ref[:, cols] = y

    if w_last == bn:
        compute(bn // MM_CHUNK, 0)
    else:
        @pl.when(j < n_blocks - 1)
        def _():
            compute(bn // MM_CHUNK, 0)

        @pl.when(j == n_blocks - 1)
        def _():
            compute(-(-w_last // MM_CHUNK), bn - w_last)

    if side_in is not None:
        side_out[...] = side_in[...].astype(side_out.dtype)


def _mm_cast(x, ws, *, bm, bn, n_cols, m_rows=None, col0=0, epilogue="plain", scale=1.0, side=None,
             head_major=False, name):
    m, k = x.shape
    m = m if m_rows is None else m_rows
    n_w = len(ws)
    n_blocks = -(-n_cols // bn)
    m_blocks = m // bm
    ck = k // WEIGHT_CHUNKS
    assert m % bm == 0 and bn % MM_CHUNK == 0 and ck * WEIGHT_CHUNKS == k and ck % 16 == 0
    assert m_blocks >= WEIGHT_CHUNKS and n_cols >= bn and (n_cols % bn) % LANES == 0
    in_specs = [pl.BlockSpec((bm, k), lambda j, i: (i, 0))]
    in_specs += [pl.BlockSpec(memory_space=pl.ANY)] * n_w
    if head_major:
        assert n_cols % bn == 0
        out_specs = [pl.BlockSpec((bn // HEAD_DIM, bm, HEAD_DIM), lambda j, i: (j, i, 0))]
        out_shape = [jax.ShapeDtypeStruct((n_cols // HEAD_DIM, m, HEAD_DIM), BF16)]
    else:
        out_specs = [pl.BlockSpec((bm, bn), lambda j, i: (i, j))]
        out_shape = [jax.ShapeDtypeStruct((m, n_cols), BF16)]
    args = [x, *ws]
    if side is not None:
        rows, c = side.shape
        side_blocks = rows // SIDE_ROWS
        assert side_blocks * SIDE_ROWS == rows and side_blocks <= n_blocks * m_blocks
        side_spec = pl.BlockSpec(
            (SIDE_ROWS, c), lambda j, i: (jnp.minimum(j * m_blocks + i, side_blocks - 1), 0))
        in_specs.append(side_spec)
        out_specs.append(side_spec)
        out_shape.append(jax.ShapeDtypeStruct((rows, c), BF16))
        args.append(side)
    kern = functools.partial(
        _mm_cast_kernel, n_w=n_w, bn=bn, n_blocks=n_blocks, m_blocks=m_blocks, col0=col0, n_cols=n_cols,
        epilogue=epilogue, scale=scale, has_side=side is not None, head_major=head_major)
    out = pl.pallas_call(
        kern,
        grid=(n_blocks, m_blocks),
        in_specs=in_specs,
        out_specs=out_specs,
        out_shape=out_shape,
        scratch_shapes=([pltpu.VMEM((2, ck, bn), F32)] * n_w + [pltpu.VMEM((2, k, bn), BF16)] * n_w
                        + [pltpu.SemaphoreType.DMA((n_w, 2))]),
        compiler_params=_cparams(("arbitrary", "arbitrary")),
        name=name,
    )(*args)
    return out if side is not None else out[0]


LN_ROWS = 64


def _down_ln_kernel(g_ref, w_ref, x_ref, gate_ref, lg_ref, lb_ref, o_ref, *, k_blocks, k_last):
    kk = pl.program_id(1)

    def part(kw):
        return jnp.dot(g_ref[:, :kw], w_ref[:kw, :], preferred_element_type=F32)

    bk = w_ref.shape[0]

    @pl.when(kk == 0)
    def _():
        o_ref[...] = part(bk)

    @pl.when((kk > 0) & (kk < k_blocks - 1))
    def _():
        o_ref[...] += part(bk)

    @pl.when(kk == k_blocks - 1)
    def _():
        o_ref[...] += part(k_last)
        gate = gate_ref[...]
        lg = lg_ref[...]
        lb = lb_ref[...]

        def ln_chunk(r, carry):
            rows = pl.ds(pl.multiple_of(r * LN_ROWS, LN_ROWS), LN_ROWS)
            z = DEEPNORM_ALPHA * x_ref[rows, :] + gate * o_ref[rows, :]
            o_ref[rows, :] = _ln_rows(z) * lg + lb
            return carry

        lax.fori_loop(0, o_ref.shape[0] // LN_ROWS, ln_chunk, 0)


def _down_ln(g, w, x, gate, lg, lb, *, bm, bk):
    m, k = g.shape
    n = w.shape[1]
    assert m % bm == 0 and w.shape[0] == k
    k_blocks = -(-k // bk)
    k_last = k - (k_blocks - 1) * bk
    assert k_blocks >= 2 and k_last % LANES == 0
    row = pl.BlockSpec((bm, n), lambda i, kk: (i, 0))
    vec = pl.BlockSpec((1, n), lambda i, kk: (0, 0))
    return pl.pallas_call(
        functools.partial(_down_ln_kernel, k_blocks=k_blocks, k_last=k_last),
        grid=(m // bm, k_blocks),
        in_specs=[
            pl.BlockSpec((bm, bk), lambda i, kk: (i, kk)),
            pl.BlockSpec((bk, n), lambda i, kk: (kk, 0)),
            row, vec, vec, vec,
        ],
        out_specs=row,
        out_shape=jax.ShapeDtypeStruct((m, n), F32),
        compiler_params=_cparams(("arbitrary", "arbitrary")),
        name="swiglu_down_ln",
    )(g, w, x, gate, lg, lb)


N_PAIR_ROWS = 2 * WIN_H - 2


def _bias_kernel(rpb_ref, o_ref):
    h = pl.program_id(0)
    n_dcol = 2 * WIN_W - 1
    q = lax.broadcasted_iota(jnp.int32, (GRID_W, 2 * GRID_W), 0)
    l = lax.broadcasted_iota(jnp.int32, (GRID_W, 2 * GRID_W), 1)
    kc = l & (GRID_W - 1)
    hi = l >= GRID_W
    dcol = jnp.clip(kc - q, -(WIN_W - 1), WIN_W - 1) + (WIN_W - 1)
    cs = jnp.clip(q - WIN_W // 2, 0, GRID_W - WIN_W)
    inside = (kc >= cs) & (kc < cs + WIN_W)
    base = h * ((2 * WIN_H - 1) * n_dcol)

    def body(dra, carry):
        acc = jnp.zeros((GRID_W, 2 * GRID_W), F32)
        for d in range(n_dcol):
            lo_v = rpb_ref[base + dra * n_dcol + d]
            hi_v = rpb_ref[base + (dra + 1) * n_dcol + d]
            acc = jnp.where(dcol == d, jnp.where(hi, hi_v, lo_v), acc)
        o_ref[dra] = jnp.where(inside, acc, -jnp.inf)
        return carry

    lax.fori_loop(0, N_PAIR_ROWS, body, 0)


def _bias_table(rpb_flat, n_heads):
    return pl.pallas_call(
        _bias_kernel,
        grid=(n_heads,),
        in_specs=[pl.BlockSpec(memory_space=pltpu.SMEM)],
        out_specs=pl.BlockSpec((None, N_PAIR_ROWS, GRID_W, 2 * GRID_W), lambda h: (h, 0, 0, 0)),
        out_shape=jax.ShapeDtypeStruct((n_heads, N_PAIR_ROWS, GRID_W, 2 * GRID_W), F32),
        compiler_params=_cparams(("arbitrary",)),
        name="bias_table",
    )(rpb_flat)


ROWS_PER_STEP = 32


def _attn_kernel(q_ref, k_ref, v_ref, kc_ref, vc_ref, tp_ref, sb_ref, wa_ref, ba_ref, o_ref, mod_ref,
                 kt_ref, *, n_rows):
    jb = pl.program_id(1)
    s_keys = k_ref.shape[0]

    @pl.when(jb == 0)
    def _():
        kt_ref[0] = k_ref[...].T
        n_shift = s_keys - LANES
        kt_ref[1, :, :n_shift] = k_ref[GRID_W:GRID_W + n_shift, :].T

    _deferred_mod_tile(sb_ref, wa_ref, ba_ref, mod_ref)

    n_lat = WIN_H * GRID_W
    q = q_ref[...]
    kc = kc_ref[...]
    vc = vc_ref[...]
    dn_t = (((1,), (1,)), ((), ()))
    s_ctx = lax.dot_general(q, kc, dn_t, preferred_element_type=F32)
    starts, s_rows = [], []
    for t in range(ROWS_PER_STEP):
        r = jb * ROWS_PER_STEP + t
        ks = jnp.clip(r - WIN_H // 2, 0, n_rows - WIN_H)
        shift = r - ks
        start = pl.multiple_of(ks * GRID_W, GRID_W)
        starts.append(start)
        lane0 = pl.multiple_of((ks // 2) * LANES, LANES)
        kw_t = kt_ref[ks % 2, :, pl.ds(lane0, n_lat)]
        qt = q[t * GRID_W:(t + 1) * GRID_W]
        s = jnp.dot(qt, kw_t, preferred_element_type=F32)
        bias = jnp.concatenate(
            [tp_ref[2 * p - shift + (WIN_H - 1)] for p in range(WIN_H // 2)], axis=1)
        s_rows.append(s + bias)
    s_lat = jnp.concatenate(s_rows, axis=0)
    tiles = [s_lat[:, c * LANES:(c + 1) * LANES] for c in range(n_lat // LANES)]
    tiles += [s_ctx[:, c * LANES:(c + 1) * LANES] for c in range(s_ctx.shape[1] // LANES)]
    m = jnp.max(functools.reduce(jnp.maximum, tiles), axis=1, keepdims=True)
    e = jnp.exp((s_lat - m).astype(BF16))
    ec = jnp.exp((s_ctx - m).astype(BF16))
    ones = jnp.ones((n_lat, HEAD_DIM), BF16)
    vc_aug = jnp.concatenate([vc, ones[:vc.shape[0]]], axis=1)
    oc = jnp.dot(ec, vc_aug, preferred_element_type=F32)
    den = jnp.dot(e, ones, preferred_element_type=F32) + oc[:, HEAD_DIM:]
    for t in range(ROWS_PER_STEP):
        rows = slice(t * GRID_W, (t + 1) * GRID_W)
        vw = v_ref[pl.ds(starts[t], n_lat), :]
        o = jnp.dot(e[rows], vw, preferred_element_type=F32) + oc[rows, :HEAD_DIM]
        o_ref[rows, :] = (o / den[rows]).astype(o_ref.dtype)


def _attention(q, kv, tp, sb0, w_ada, b_ada, mod_col0):
    n_heads, s, _ = q.shape
    n_ctx = kv.shape[1] - s
    assert s % n_ctx == 0
    ctx_blk = s // n_ctx
    n_rows = s // GRID_W
    tq = ROWS_PER_STEP * GRID_W
    bph = n_rows // ROWS_PER_STEP
    d, n_mod = w_ada.shape
    n_side = n_mod - mod_col0
    tn = n_side // (n_heads * bph)
    assert tn * n_heads * bph == n_side and tn % LANES == 0 and mod_col0 % tn == 0
    blk0 = mod_col0 // tn
    return pl.pallas_call(
        functools.partial(_attn_kernel, n_rows=n_rows),
        grid=(n_heads, bph),
        in_specs=[
            pl.BlockSpec((None, tq, HEAD_DIM), lambda h, j: (h, j, 0)),
            pl.BlockSpec((None, s, HEAD_DIM), lambda h, j: (h, 0, 0)),
            pl.BlockSpec((None, s, HEAD_DIM), lambda h, j: (n_heads + h, 0, 0)),
            pl.BlockSpec((None, n_ctx, HEAD_DIM), lambda h, j: (h, ctx_blk, 0)),
            pl.BlockSpec((None, n_ctx, HEAD_DIM), lambda h, j: (n_heads + h, ctx_blk, 0)),
            pl.BlockSpec((None, N_PAIR_ROWS, GRID_W, 2 * GRID_W), lambda h, j: (h, 0, 0, 0)),
            pl.BlockSpec((d, LANES), lambda h, j: (0, 0)),
            pl.BlockSpec((d, tn), lambda h, j: (0, blk0 + h * bph + j)),
            pl.BlockSpec((1, tn), lambda h, j: (0, blk0 + h * bph + j)),
        ],
        out_specs=[pl.BlockSpec((None, tq, HEAD_DIM), lambda h, j: (h, j, 0)),
                   pl.BlockSpec((SUBLANES, tn), lambda h, j: (0, h * bph + j))],
        out_shape=[jax.ShapeDtypeStruct((n_heads, s, HEAD_DIM), BF16),
                   jax.ShapeDtypeStruct((SUBLANES, n_side), F32)],
        scratch_shapes=[pltpu.VMEM((2, HEAD_DIM, s), BF16)],
        compiler_params=_cparams(("arbitrary", "arbitrary")),
        name="nbr_attention",
    )(q, kv, kv, kv, kv, tp, sb0, w_ada, b_ada)


SGU_CHUNKS_PER_STEP = 4


def _sgu_merge_kernel(gu_ref, gv_ref, ob_ref, ws_ref, bst_ref, lg_ref, lb_ref, gn_ref, o_ref, oa_ref,
                      *, n_heads, w_sgu):
    v = _ln_rows(gv_ref[...].astype(F32)) * lg_ref[...] + lb_ref[...]
    v = v.astype(BF16)
    bst = bst_ref[...]
    for h in range(n_heads):
        cols = slice(h * HEAD_DIM, (h + 1) * HEAD_DIM)
        vh = jnp.concatenate(
            [v[c * CHUNK:(c + 1) * CHUNK, cols] for c in range(SGU_CHUNKS_PER_STEP)], axis=1)
        mixed = jnp.dot(ws_ref[h], vh, preferred_element_type=F32) + bst[:, h:h + 1]
        for c in range(SGU_CHUNKS_PER_STEP):
            rows = slice(c * CHUNK, (c + 1) * CHUNK)
            oa_ref[rows, cols] = gu_ref[rows, cols].astype(F32) * mixed[:, c * HEAD_DIM:(c + 1) * HEAD_DIM]
    gn = gn_ref[...]
    o_ref[:, :w_sgu] = (_rms_rows(oa_ref[...]) * gn[:, :w_sgu]).astype(o_ref.dtype)
    ob = jnp.concatenate([ob_ref[h] for h in range(ob_ref.shape[0])], axis=1).astype(F32)
    o_ref[:, w_sgu:] = (_rms_rows(ob) * gn[:, w_sgu:]).astype(o_ref.dtype)


def _sgu_merge(p, ob, ws, bst, lg, lb, gn, *, n_heads):
    s = p.shape[0]
    w_sgu = n_heads * HEAD_DIM
    n_na_heads = ob.shape[0]
    w_na = n_na_heads * HEAD_DIM
    tr = SGU_CHUNKS_PER_STEP * CHUNK
    return pl.pallas_call(
        functools.partial(_sgu_merge_kernel, n_heads=n_heads, w_sgu=w_sgu),
        grid=(s // tr,),
        in_specs=[
            pl.BlockSpec((tr, w_sgu), lambda i: (i, 0)),
            pl.BlockSpec((tr, w_sgu), lambda i: (i, 1)),
            pl.BlockSpec((n_na_heads, tr, HEAD_DIM), lambda i: (0, i, 0)),
            pl.BlockSpec((n_heads, CHUNK, CHUNK), lambda i: (0, 0, 0)),
            pl.BlockSpec((CHUNK, n_heads), lambda i: (0, 0)),
            pl.BlockSpec((1, w_sgu), lambda i: (0, 0)),
            pl.BlockSpec((1, w_sgu), lambda i: (0, 0)),
            pl.BlockSpec((1, w_sgu + w_na), lambda i: (0, 0)),
        ],
        out_specs=pl.BlockSpec((tr, w_sgu + w_na), lambda i: (i, 0)),
        out_shape=jax.ShapeDtypeStruct((s, w_sgu + w_na), BF16),
        scratch_shapes=[pltpu.VMEM((tr, w_sgu), F32)],
        compiler_params=_cparams(("parallel",)),
        name="sgu_merge",
    )(p, p, ob, ws, bst, lg, lb, gn)


def _res_ln_mod_kernel(x_ref, y_ref, g_ref, lg_ref, lb_ref, sc_ref, sh_ref, x1_ref, h_ref):
    xn = _ln_rows(x_ref[...])
    z = DEEPNORM_ALPHA * xn + g_ref[...] * y_ref[...].astype(F32)
    x1 = _ln_rows(z) * lg_ref[...] + lb_ref[...]
    x1_ref[...] = x1
    h_ref[...] = (x1 * (1.0 + sc_ref[...]) + sh_ref[...]).astype(h_ref.dtype)


def _res_ln_mod(x, y, gate, lg, lb, sc, sh, tr):
    m, d = x.shape
    row = pl.BlockSpec((tr, d), lambda i: (i, 0))
    vec = pl.BlockSpec((1, d), lambda i: (0, 0))
    return pl.pallas_call(
        _res_ln_mod_kernel,
        grid=(m // tr,),
        in_specs=[row, row, vec, vec, vec, vec, vec],
        out_specs=[row, row],
        out_shape=[jax.ShapeDtypeStruct((m, d), F32), jax.ShapeDtypeStruct((m, d), BF16)],
        compiler_params=_cparams(("parallel",)),
        name="res_ln_mod",
    )(x, y, gate, lg, lb, sc, sh)


def kernel(x, c, ctx, c_ctx, w_ada, b_ada, w_in, w_s, b_s, sgu_g, sgu_b, rpb, gn_g, w_o, ln_g, ln_b,
           w_gate, w_up, w_down):
    batch, seq, d = x.shape
    assert batch == 1 and w_ada.shape[0] == DEPTH
    n_sgu_heads = w_s.shape[1]
    n_na_heads = rpb.shape[1]
    w_sgu = n_sgu_heads * HEAD_DIM
    w_na = n_na_heads * HEAD_DIM
    d_ff = w_gate.shape[2]
    assert w_in.shape[2] == 2 * w_sgu + 3 * w_na and w_sgu == w_na

    x2 = x.reshape(seq, d)
    ctx2 = ctx.reshape(ctx.shape[1], d)

    ccol = jnp.stack([c.reshape(d), c_ctx], axis=1)
    b_ada2 = b_ada[0].reshape(1, 6 * d)
    mod, sb0 = _modulation(ccol, w_ada[0], b_ada2, 2 * d)

    h = _ln_mod(x2, ctx2, mod, 256)

    guv = _mm_cast(h, [w_in[0]], m_rows=seq, bm=1024, bn=1024, n_cols=2 * w_sgu, epilogue="gelu",
                   name="in_proj_uv")
    q = _mm_cast(h, [w_in[0]], m_rows=seq, bm=1024, bn=1024, n_cols=w_na, col0=2 * w_sgu,
                 epilogue="scale", scale=HEAD_DIM ** -0.5, head_major=True, name="in_proj_q")
    kv = _mm_cast(h, [w_in[0]], bm=768, bn=1024, n_cols=2 * w_na, col0=2 * w_sgu + w_na,
                  head_major=True, name="in_proj_kv")

    tp = _bias_table(rpb[0].reshape(-1), n_na_heads)
    ob, mod_b = _attention(q, kv, tp, sb0, w_ada[0], b_ada2, 2 * d)
    g1, sh2, sc2, g2 = [mod_b[0:1, i * d:(i + 1) * d] for i in range(4)]

    merged = _sgu_merge(guv, ob, w_s[0].astype(BF16), b_s[0].T, sgu_g, sgu_b, gn_g, n_heads=n_sgu_heads)

    y = _mm_cast(merged, [w_o[0]], bm=1024, bn=1024, n_cols=d, name="out_proj")
    x1, h2 = _res_ln_mod(x2, y, g1, ln_g[0, 0:1], ln_b[0, 0:1], sc2, sh2, 256)

    g, wd = _mm_cast(h2, [w_gate[0], w_up[0]], bm=1024, bn=512, n_cols=d_ff, epilogue="swiglu",
                     side=w_down[0], name="swiglu_up")
    out = _down_ln(g, wd, x1, g2, ln_g[0, 1:2], ln_b[0, 1:2], bm=512, bk=1024)
    return out.reshape(batch, seq, d)
```

```python
import functools
import math

import jax
import jax.numpy as jnp
from jax import lax
from jax.experimental import pallas as pl
from jax.experimental.pallas import tpu as pltpu

F32 = jnp.float32
BF16 = jnp.bfloat16

GRID_W = 64
HEAD_DIM = 128
CHUNK = 128
WIN_H = 8
WIN_W = 16
LN_EPS = 1e-5
DEPTH = 1
DEEPNORM_ALPHA = (2.0 * DEPTH) ** 0.25

LANES = 128
SUBLANES = 8
VMEM_LIMIT = 56 * 1024 * 1024


def _cparams(sem):
    return pltpu.CompilerParams(dimension_semantics=sem, vmem_limit_bytes=VMEM_LIMIT)


def _gelu(x):
    return 0.5 * x * (1.0 + lax.erf(x * (1.0 / math.sqrt(2.0))))


def _silu(x):
    return x * jax.nn.sigmoid(x)


def _ln_rows(x):
    mu = jnp.mean(x, axis=-1, keepdims=True)
    xc = x - mu
    var = jnp.mean(xc * xc, axis=-1, keepdims=True)
    return xc * lax.rsqrt(var + LN_EPS)


def _rms_rows(x):
    return x * lax.rsqrt(jnp.mean(x * x, axis=-1, keepdims=True) + LN_EPS)


def _mod_kernel(ccol_ref, w_ref, b_ref, o_ref, sb0_ref, sb_ref, *, tn):
    d = w_ref.shape[0]

    @pl.when(pl.program_id(0) == 0)
    def _():
        s = _silu(ccol_ref[...])
        sb_ref[0] = jnp.broadcast_to(s[:, 0:1], (d, LANES))
        sb_ref[1] = jnp.broadcast_to(s[:, 1:2], (d, LANES))
        sb0_ref[...] = sb_ref[0]

    nl = tn // LANES

    def body(kc, accs):
        k0 = pl.multiple_of(kc * SUBLANES, SUBLANES)
        s0 = sb_ref[0, pl.ds(k0, SUBLANES), :]
        s1 = sb_ref[1, pl.ds(k0, SUBLANES), :]
        w = w_ref[pl.ds(k0, SUBLANES), :]
        out = []
        for j in range(nl):
            wj = w[:, j * LANES:(j + 1) * LANES]
            out.append(accs[2 * j] + wj * s0)
            out.append(accs[2 * j + 1] + wj * s1)
        return tuple(out)

    zero = jnp.zeros((SUBLANES, LANES), F32)
    accs = lax.fori_loop(0, d // SUBLANES, body, (zero,) * (2 * nl), unroll=8)
    r0 = jnp.concatenate([jnp.sum(accs[2 * j], axis=0, keepdims=True) for j in range(nl)], axis=1)
    r1 = jnp.concatenate([jnp.sum(accs[2 * j + 1], axis=0, keepdims=True) for j in range(nl)], axis=1)
    row = lax.broadcasted_iota(jnp.int32, (SUBLANES, tn), 0)
    b = b_ref[...]
    o_ref[...] = jnp.where(row == 0, r0 + b, jnp.where(row == 1, r1 + b, 0.0))


def _modulation(ccol, w_ada, b_ada, n):
    d = w_ada.shape[0]
    tn = 512
    return pl.pallas_call(
        functools.partial(_mod_kernel, tn=tn),
        grid=(n // tn,),
        in_specs=[
            pl.BlockSpec((d, 2), lambda j: (0, 0)),
            pl.BlockSpec((d, tn), lambda j: (0, j)),
            pl.BlockSpec((1, tn), lambda j: (0, j)),
        ],
        out_specs=[pl.BlockSpec((SUBLANES, tn), lambda j: (0, j)),
                   pl.BlockSpec((d, LANES), lambda j: (0, 0))],
        out_shape=[jax.ShapeDtypeStruct((SUBLANES, n), F32), jax.ShapeDtypeStruct((d, LANES), F32)],
        scratch_shapes=[pltpu.VMEM((2, d, LANES), F32)],
        compiler_params=_cparams(("arbitrary",)),
        name="modulation",
    )(ccol, w_ada, b_ada)


MOD_ACCS = 4


def _deferred_mod_tile(sb_ref, w_ref, b_ref, o_ref):
    d, tn = w_ref.shape
    nl = tn // LANES
    accs = [[None] * MOD_ACCS for _ in range(nl)]
    for kc in range(d // SUBLANES):
        rows = slice(kc * SUBLANES, (kc + 1) * SUBLANES)
        s = sb_ref[rows, :]
        for jj in range(nl):
            p = w_ref[rows, jj * LANES:(jj + 1) * LANES] * s
            a = kc % MOD_ACCS
            accs[jj][a] = p if accs[jj][a] is None else accs[jj][a] + p
    tot = [jnp.sum(functools.reduce(jnp.add, accs[jj]), axis=0, keepdims=True) for jj in range(nl)]
    o_ref[...] = jnp.broadcast_to(jnp.concatenate(tot, axis=1) + b_ref[...], o_ref.shape)


def _ln_mod_kernel(x_ref, c_ref, mod_ref, o_ref, *, n_x_blocks):
    d = x_ref.shape[1]
    i = pl.program_id(0)

    def emit(src_ref, row):
        sh = mod_ref[row:row + 1, 0:d]
        sc = mod_ref[row:row + 1, d:2 * d]
        o_ref[...] = (_ln_rows(src_ref[...]) * (1.0 + sc) + sh).astype(o_ref.dtype)

    @pl.when(i < n_x_blocks)
    def _():
        emit(x_ref, 0)

    @pl.when(i >= n_x_blocks)
    def _():
        emit(c_ref, 1)


def _ln_mod(x, ctx, mod, tr):
    m, d = x.shape
    n_x_blocks = m // tr
    n_c_blocks = ctx.shape[0] // tr
    return pl.pallas_call(
        functools.partial(_ln_mod_kernel, n_x_blocks=n_x_blocks),
        grid=(n_x_blocks + n_c_blocks,),
        in_specs=[
            pl.BlockSpec((tr, d), lambda i: (jnp.minimum(i, n_x_blocks - 1), 0)),
            pl.BlockSpec((tr, d), lambda i: (jnp.maximum(i - n_x_blocks, 0), 0)),
            pl.BlockSpec((SUBLANES, 2 * d), lambda i: (0, 0)),
        ],
        out_specs=pl.BlockSpec((tr, d), lambda i: (i, 0)),
        out_shape=jax.ShapeDtypeStruct((m + ctx.shape[0], d), BF16),
        compiler_params=_cparams(("arbitrary",)),
        name="ln_mod",
    )(x, ctx, mod)


CAST_ROWS = 256
SIDE_ROWS = 64
MM_CHUNK = 256


def _mm_cast_kernel(*refs, n_w, bn, n_blocks, m_blocks, col0, n_cols, epilogue, scale, has_side, head_major):
    it = iter(refs)
    x_ref = next(it)
    w_hbm = [next(it) for _ in range(n_w)]
    side_in = next(it) if has_side else None
    o_ref = next(it)
    side_out = next(it) if has_side else None
    stage = [next(it) for _ in range(n_w)]
    wb = [next(it) for _ in range(n_w)]
    sem = next(it)

    j = pl.program_id(0)
    i = pl.program_id(1)
    k = stage[0].shape[0]
    w_last = n_cols - (n_blocks - 1) * bn

    def slab_copies(jj, width):
        return [pltpu.make_async_copy(w_hbm[a].at[:, pl.ds(col0 + jj * bn, width)],
                                      stage[a].at[:, pl.ds(0, width)], sem.at[a]) for a in range(n_w)]

    def for_slab(jj, fn):
        if w_last == bn:
            for cp in slab_copies(jj, bn):
                fn(cp)
            return

        @pl.when(jj < n_blocks - 1)
        def _():
            for cp in slab_copies(jj, bn):
                fn(cp)

        @pl.when(jj == n_blocks - 1)
        def _():
            for cp in slab_copies(n_blocks - 1, w_last):
                fn(cp)

    @pl.when((j == 0) & (i == 0))
    def _():
        for_slab(j, lambda cp: cp.start())

    @pl.when(i == 0)
    def _():
        for_slab(j, lambda cp: cp.wait())

        def cast_rows(r, carry):
            r0 = pl.multiple_of(r * CAST_ROWS, CAST_ROWS)
            for a in range(n_w):
                wb[a][pl.ds(r0, CAST_ROWS), :] = stage[a][pl.ds(r0, CAST_ROWS), :].astype(BF16)
            return carry

        lax.fori_loop(0, k // CAST_ROWS, cast_rows, 0)

    @pl.when((i == min(1, m_blocks - 1)) & (j + 1 < n_blocks))
    def _():
        for_slab(j + 1, lambda cp: cp.start())

    def compute(n_chunks):
        x = x_ref[...]
        for c in range(n_chunks):
            cols = slice(c * MM_CHUNK, (c + 1) * MM_CHUNK)
            accs = [jnp.dot(x, wb[a][:, cols], preferred_element_type=F32) for a in range(n_w)]
            if epilogue == "plain":
                y = accs[0]
            elif epilogue == "gelu":
                y = _gelu(accs[0])
            elif epilogue == "scale":
                y = accs[0] * scale
            else:
                y = _silu(accs[0]) * accs[1]
            y = y.astype(o_ref.dtype)
            if head_major:
                for hh in range(MM_CHUNK // HEAD_DIM):
                    o_ref[c * (MM_CHUNK // HEAD_DIM) + hh] = y[:, hh * HEAD_DIM:(hh + 1) * HEAD_DIM]
            else:
                o_ref[:, cols] = y

    full_chunks = bn // MM_CHUNK
    last_chunks = -(-w_last // MM_CHUNK)
    if last_chunks == full_chunks:
        compute(full_chunks)
    else:
        @pl.when(j < n_blocks - 1)
        def _():
            compute(full_chunks)

        @pl.when(j == n_blocks - 1)
        def _():
            compute(last_chunks)

    if side_in is not None:
        side_out[...] = side_in[...].astype(side_out.dtype)


def _mm_cast(x, ws, *, bm, bn, n_cols, m_rows=None, col0=0, epilogue="plain", scale=1.0, side=None,
             head_major=False, name):
    m, k = x.shape
    m = m if m_rows is None else m_rows
    n_w = len(ws)
    assert m % bm == 0 and bn % MM_CHUNK == 0 and k % CAST_ROWS == 0
    n_blocks = -(-n_cols // bn)
    m_blocks = m // bm
    in_specs = [pl.BlockSpec((bm, k), lambda j, i: (i, 0))]
    in_specs += [pl.BlockSpec(memory_space=pl.ANY)] * n_w
    if head_major:
        assert n_cols % bn == 0
        out_specs = [pl.BlockSpec((bn // HEAD_DIM, bm, HEAD_DIM), lambda j, i: (j, i, 0))]
        out_shape = [jax.ShapeDtypeStruct((n_cols // HEAD_DIM, m, HEAD_DIM), BF16)]
    else:
        out_specs = [pl.BlockSpec((bm, bn), lambda j, i: (i, j))]
        out_shape = [jax.ShapeDtypeStruct((m, n_cols), BF16)]
    args = [x, *ws]
    if side is not None:
        rows, c = side.shape
        side_blocks = rows // SIDE_ROWS
        assert side_blocks * SIDE_ROWS == rows and side_blocks <= n_blocks * m_blocks
        side_spec = pl.BlockSpec(
            (SIDE_ROWS, c), lambda j, i: (jnp.minimum(j * m_blocks + i, side_blocks - 1), 0))
        in_specs.append(side_spec)
        out_specs.append(side_spec)
        out_shape.append(jax.ShapeDtypeStruct((rows, c), BF16))
        args.append(side)
    kern = functools.partial(
        _mm_cast_kernel, n_w=n_w, bn=bn, n_blocks=n_blocks, m_blocks=m_blocks, col0=col0, n_cols=n_cols,
        epilogue=epilogue, scale=scale, has_side=side is not None, head_major=head_major)
    out = pl.pallas_call(
        kern,
        grid=(n_blocks, m_blocks),
        in_specs=in_specs,
        out_specs=out_specs,
        out_shape=out_shape,
        scratch_shapes=([pltpu.VMEM((k, bn), F32)] * n_w + [pltpu.VMEM((k, bn), BF16)] * n_w
                        + [pltpu.SemaphoreType.DMA((n_w,))]),
        compiler_params=_cparams(("arbitrary", "arbitrary")),
        name=name,
    )(*args)
    return out if side is not None else out[0]


LN_ROWS = 64


def _down_ln_kernel(g_ref, w_ref, x_ref, gate_ref, lg_ref, lb_ref, o_ref, *, k_blocks, k_last):
    kk = pl.program_id(1)

    def part(kw):
        return jnp.dot(g_ref[:, :kw], w_ref[:kw, :], preferred_element_type=F32)

    bk = w_ref.shape[0]

    @pl.when(kk == 0)
    def _():
        o_ref[...] = part(bk)

    @pl.when((kk > 0) & (kk < k_blocks - 1))
    def _():
        o_ref[...] += part(bk)

    @pl.when(kk == k_blocks - 1)
    def _():
        o_ref[...] += part(k_last)
        gate = gate_ref[...]
        lg = lg_ref[...]
        lb = lb_ref[...]

        def ln_chunk(r, carry):
            rows = pl.ds(pl.multiple_of(r * LN_ROWS, LN_ROWS), LN_ROWS)
            z = DEEPNORM_ALPHA * x_ref[rows, :] + gate * o_ref[rows, :]
            o_ref[rows, :] = _ln_rows(z) * lg + lb
            return carry

        lax.fori_loop(0, o_ref.shape[0] // LN_ROWS, ln_chunk, 0)


def _down_ln(g, w, x, gate, lg, lb, *, bm, bk):
    m, k = g.shape
    n = w.shape[1]
    assert m % bm == 0 and w.shape[0] == k
    k_blocks = -(-k // bk)
    k_last = k - (k_blocks - 1) * bk
    assert k_blocks >= 2 and k_last % LANES == 0
    row = pl.BlockSpec((bm, n), lambda i, kk: (i, 0))
    vec = pl.BlockSpec((1, n), lambda i, kk: (0, 0))
    return pl.pallas_call(
        functools.partial(_down_ln_kernel, k_blocks=k_blocks, k_last=k_last),
        grid=(m // bm, k_blocks),
        in_specs=[
            pl.BlockSpec((bm, bk), lambda i, kk: (i, kk)),
            pl.BlockSpec((bk, n), lambda i, kk: (kk, 0)),
            row, vec, vec, vec,
        ],
        out_specs=row,
        out_shape=jax.ShapeDtypeStruct((m, n), F32),
        compiler_params=_cparams(("arbitrary", "arbitrary")),
        name="swiglu_down_ln",
    )(g, w, x, gate, lg, lb)


N_PAIR_ROWS = 2 * WIN_H - 2


def _bias_kernel(rpb_ref, o_ref):
    h = pl.program_id(0)
    n_dcol = 2 * WIN_W - 1
    q = lax.broadcasted_iota(jnp.int32, (GRID_W, 2 * GRID_W), 0)
    l = lax.broadcasted_iota(jnp.int32, (GRID_W, 2 * GRID_W), 1)
    kc = l & (GRID_W - 1)
    hi = l >= GRID_W
    dcol = jnp.clip(kc - q, -(WIN_W - 1), WIN_W - 1) + (WIN_W - 1)
    cs = jnp.clip(q - WIN_W // 2, 0, GRID_W - WIN_W)
    inside = (kc >= cs) & (kc < cs + WIN_W)
    base = h * ((2 * WIN_H - 1) * n_dcol)

    def body(dra, carry):
        acc = jnp.zeros((GRID_W, 2 * GRID_W), F32)
        for d in range(n_dcol):
            lo_v = rpb_ref[base + dra * n_dcol + d]
            hi_v = rpb_ref[base + (dra + 1) * n_dcol + d]
            acc = jnp.where(dcol == d, jnp.where(hi, hi_v, lo_v), acc)
        o_ref[dra] = jnp.where(inside, acc, -jnp.inf)
        return carry

    lax.fori_loop(0, N_PAIR_ROWS, body, 0)


def _bias_table(rpb_flat, n_heads):
    return pl.pallas_call(
        _bias_kernel,
        grid=(n_heads,),
        in_specs=[pl.BlockSpec(memory_space=pltpu.SMEM)],
        out_specs=pl.BlockSpec((None, N_PAIR_ROWS, GRID_W, 2 * GRID_W), lambda h: (h, 0, 0, 0)),
        out_shape=jax.ShapeDtypeStruct((n_heads, N_PAIR_ROWS, GRID_W, 2 * GRID_W), F32),
        compiler_params=_cparams(("arbitrary",)),
        name="bias_table",
    )(rpb_flat)


ROWS_PER_STEP = 64


def _attn_kernel(q_ref, k_ref, v_ref, kc_ref, vc_ref, tp_ref, sb_ref, wa_ref, ba_ref, o_ref, mod_ref,
                 kt_ref, *, n_rows):
    jb = pl.program_id(1)
    s_keys = k_ref.shape[0]

    @pl.when(jb == 0)
    def _():
        kt_ref[0] = k_ref[...].T
        n_shift = s_keys - LANES
        kt_ref[1, :, :n_shift] = k_ref[GRID_W:GRID_W + n_shift, :].T

    _deferred_mod_tile(sb_ref, wa_ref, ba_ref, mod_ref)

    n_lat = WIN_H * GRID_W
    q = q_ref[...]
    kc = kc_ref[...]
    vc = vc_ref[...]
    dn_t = (((1,), (1,)), ((), ()))
    s_ctx = lax.dot_general(q, kc, dn_t, preferred_element_type=F32)
    starts, s_rows = [], []
    for t in range(ROWS_PER_STEP):
        r = jb * ROWS_PER_STEP + t
        ks = jnp.clip(r - WIN_H // 2, 0, n_rows - WIN_H)
        shift = r - ks
        start = pl.multiple_of(ks * GRID_W, GRID_W)
        starts.append(start)
        lane0 = pl.multiple_of((ks // 2) * LANES, LANES)
        kw_t = kt_ref[ks % 2, :, pl.ds(lane0, n_lat)]
        qt = q[t * GRID_W:(t + 1) * GRID_W]
        s = jnp.dot(qt, kw_t, preferred_element_type=F32)
        bias = jnp.concatenate(
            [tp_ref[2 * p - shift + (WIN_H - 1)] for p in range(WIN_H // 2)], axis=1)
        s_rows.append(s + bias)
    s_lat = jnp.concatenate(s_rows, axis=0)
    tiles = [s_lat[:, c * LANES:(c + 1) * LANES] for c in range(n_lat // LANES)]
    tiles += [s_ctx[:, c * LANES:(c + 1) * LANES] for c in range(s_ctx.shape[1] // LANES)]
    m = jnp.max(functools.reduce(jnp.maximum, tiles), axis=1, keepdims=True)
    e = jnp.exp((s_lat - m).astype(BF16))
    ec = jnp.exp((s_ctx - m).astype(BF16))
    ones = jnp.ones((n_lat, HEAD_DIM), BF16)
    vc_aug = jnp.concatenate([vc, ones[:vc.shape[0]]], axis=1)
    oc = jnp.dot(ec, vc_aug, preferred_element_type=F32)
    den = jnp.dot(e, ones, preferred_element_type=F32) + oc[:, HEAD_DIM:]
    for t in range(ROWS_PER_STEP):
        rows = slice(t * GRID_W, (t + 1) * GRID_W)
        vw = v_ref[pl.ds(starts[t], n_lat), :]
        o = jnp.dot(e[rows], vw, preferred_element_type=F32) + oc[rows, :HEAD_DIM]
        o_ref[rows, :] = (o / den[rows]).astype(o_ref.dtype)


def _attention(q, kv, tp, sb0, w_ada, b_ada, mod_col0):
    n_heads, s, _ = q.shape
    n_ctx = kv.shape[1] - s
    assert s % n_ctx == 0
    ctx_blk = s // n_ctx
    n_rows = s // GRID_W
    tq = ROWS_PER_STEP * GRID_W
    bph = n_rows // ROWS_PER_STEP
    d, n_mod = w_ada.shape
    n_side = n_mod - mod_col0
    tn = n_side // (n_heads * bph)
    assert tn * n_heads * bph == n_side and tn % LANES == 0 and mod_col0 % tn == 0
    blk0 = mod_col0 // tn
    return pl.pallas_call(
        functools.partial(_attn_kernel, n_rows=n_rows),
        grid=(n_heads, bph),
        in_specs=[
            pl.BlockSpec((None, tq, HEAD_DIM), lambda h, j: (h, j, 0)),
            pl.BlockSpec((None, s, HEAD_DIM), lambda h, j: (h, 0, 0)),
            pl.BlockSpec((None, s, HEAD_DIM), lambda h, j: (n_heads + h, 0, 0)),
            pl.BlockSpec((None, n_ctx, HEAD_DIM), lambda h, j: (h, ctx_blk, 0)),
            pl.BlockSpec((None, n_ctx, HEAD_DIM), lambda h, j: (n_heads + h, ctx_blk, 0)),
            pl.BlockSpec((None, N_PAIR_ROWS, GRID_W, 2 * GRID_W), lambda h, j: (h, 0, 0, 0)),
            pl.BlockSpec((d, LANES), lambda h, j: (0, 0)),
            pl.BlockSpec((d, tn), lambda h, j: (0, blk0 + h * bph + j)),
            pl.BlockSpec((1, tn), lambda h, j: (0, blk0 + h * bph + j)),
        ],
        out_specs=[pl.BlockSpec((None, tq, HEAD_DIM), lambda h, j: (h, j, 0)),
                   pl.BlockSpec((SUBLANES, tn), lambda h, j: (0, h * bph + j))],
        out_shape=[jax.ShapeDtypeStruct((n_heads, s, HEAD_DIM), BF16),
                   jax.ShapeDtypeStruct((SUBLANES, n_side), F32)],
        scratch_shapes=[pltpu.VMEM((2, HEAD_DIM, s), BF16)],
        compiler_params=_cparams(("arbitrary", "arbitrary")),
        name="nbr_attention",
    )(q, kv, kv, kv, kv, tp, sb0, w_ada, b_ada)


SGU_CHUNKS_PER_STEP = 4


def _sgu_merge_kernel(gu_ref, gv_ref, ob_ref, ws_ref, bst_ref, lg_ref, lb_ref, gn_ref, o_ref, oa_ref,
                      *, n_heads, w_sgu):
    v = _ln_rows(gv_ref[...].astype(F32)) * lg_ref[...] + lb_ref[...]
    v = v.astype(BF16)
    bst = bst_ref[...]
    for h in range(n_heads):
        cols = slice(h * HEAD_DIM, (h + 1) * HEAD_DIM)
        vh = jnp.concatenate(
            [v[c * CHUNK:(c + 1) * CHUNK, cols] for c in range(SGU_CHUNKS_PER_STEP)], axis=1)
        mixed = jnp.dot(ws_ref[h], vh, preferred_element_type=F32) + bst[:, h:h + 1]
        for c in range(SGU_CHUNKS_PER_STEP):
            rows = slice(c * CHUNK, (c + 1) * CHUNK)
            oa_ref[rows, cols] = gu_ref[rows, cols].astype(F32) * mixed[:, c * HEAD_DIM:(c + 1) * HEAD_DIM]
    gn = gn_ref[...]
    o_ref[:, :w_sgu] = (_rms_rows(oa_ref[...]) * gn[:, :w_sgu]).astype(o_ref.dtype)
    ob = jnp.concatenate([ob_ref[h] for h in range(ob_ref.shape[0])], axis=1).astype(F32)
    o_ref[:, w_sgu:] = (_rms_rows(ob) * gn[:, w_sgu:]).astype(o_ref.dtype)


def _sgu_merge(p, ob, ws, bst, lg, lb, gn, *, n_heads):
    s = p.shape[0]
    w_sgu = n_heads * HEAD_DIM
    n_na_heads = ob.shape[0]
    w_na = n_na_heads * HEAD_DIM
    tr = SGU_CHUNKS_PER_STEP * CHUNK
    return pl.pallas_call(
        functools.partial(_sgu_merge_kernel, n_heads=n_heads, w_sgu=w_sgu),
        grid=(s // tr,),
        in_specs=[
            pl.BlockSpec((tr, w_sgu), lambda i: (i, 0)),
            pl.BlockSpec((tr, w_sgu), lambda i: (i, 1)),
            pl.BlockSpec((n_na_heads, tr, HEAD_DIM), lambda i: (0, i, 0)),
            pl.BlockSpec((n_heads, CHUNK, CHUNK), lambda i: (0, 0, 0)),
            pl.BlockSpec((CHUNK, n_heads), lambda i: (0, 0)),
            pl.BlockSpec((1, w_sgu), lambda i: (0, 0)),
            pl.BlockSpec((1, w_sgu), lambda i: (0, 0)),
            pl.BlockSpec((1, w_sgu + w_na), lambda i: (0, 0)),
        ],
        out_specs=pl.BlockSpec((tr, w_sgu + w_na), lambda i: (i, 0)),
        out_shape=jax.ShapeDtypeStruct((s, w_sgu + w_na), BF16),
        scratch_shapes=[pltpu.VMEM((tr, w_sgu), F32)],
        compiler_params=_cparams(("parallel",)),
        name="sgu_merge",
    )(p, p, ob, ws, bst, lg, lb, gn)


def _res_ln_mod_kernel(x_ref, y_ref, g_ref, lg_ref, lb_ref, sc_ref, sh_ref, x1_ref, h_ref):
    xn = _ln_rows(x_ref[...])
    z = DEEPNORM_ALPHA * xn + g_ref[...] * y_ref[...].astype(F32)
    x1 = _ln_rows(z) * lg_ref[...] + lb_ref[...]
    x1_ref[...] = x1
    h_ref[...] = (x1 * (1.0 + sc_ref[...]) + sh_ref[...]).astype(h_ref.dtype)


def _res_ln_mod(x, y, gate, lg, lb, sc, sh, tr):
    m, d = x.shape
    row = pl.BlockSpec((tr, d), lambda i: (i, 0))
    vec = pl.BlockSpec((1, d), lambda i: (0, 0))
    return pl.pallas_call(
        _res_ln_mod_kernel,
        grid=(m // tr,),
        in_specs=[row, row, vec, vec, vec, vec, vec],
        out_specs=[row, row],
        out_shape=[jax.ShapeDtypeStruct((m, d), F32), jax.ShapeDtypeStruct((m, d), BF16)],
        compiler_params=_cparams(("parallel",)),
        name="res_ln_mod",
    )(x, y, gate, lg, lb, sc, sh)


def kernel(x, c, ctx, c_ctx, w_ada, b_ada, w_in, w_s, b_s, sgu_g, sgu_b, rpb, gn_g, w_o, ln_g, ln_b,
           w_gate, w_up, w_down):
    batch, seq, d = x.shape
    assert batch == 1 and w_ada.shape[0] == DEPTH
    n_sgu_heads = w_s.shape[1]
    n_na_heads = rpb.shape[1]
    w_sgu = n_sgu_heads * HEAD_DIM
    w_na = n_na_heads * HEAD_DIM
    d_ff = w_gate.shape[2]
    assert w_in.shape[2] == 2 * w_sgu + 3 * w_na and w_sgu == w_na

    x2 = x.reshape(seq, d)
    ctx2 = ctx.reshape(ctx.shape[1], d)

    ccol = jnp.stack([c.reshape(d), c_ctx], axis=1)
    b_ada2 = b_ada[0].reshape(1, 6 * d)
    mod, sb0 = _modulation(ccol, w_ada[0], b_ada2, 2 * d)

    h = _ln_mod(x2, ctx2, mod, 256)

    guv = _mm_cast(h, [w_in[0]], m_rows=seq, bm=1024, bn=1024, n_cols=2 * w_sgu, epilogue="gelu",
                   name="in_proj_uv")
    q = _mm_cast(h, [w_in[0]], m_rows=seq, bm=1024, bn=1024, n_cols=w_na, col0=2 * w_sgu,
                 epilogue="scale", scale=HEAD_DIM ** -0.5, head_major=True, name="in_proj_q")
    kv = _mm_cast(h, [w_in[0]], bm=768, bn=1024, n_cols=2 * w_na, col0=2 * w_sgu + w_na,
                  head_major=True, name="in_proj_kv")

    tp = _bias_table(rpb[0].reshape(-1), n_na_heads)
    ob, mod_b = _attention(q, kv, tp, sb0, w_ada[0], b_ada2, 2 * d)
    g1, sh2, sc2, g2 = [mod_b[0:1, i * d:(i + 1) * d] for i in range(4)]

    merged = _sgu_merge(guv, ob, w_s[0].astype(BF16), b_s[0].T, sgu_g, sgu_b, gn_g, n_heads=n_sgu_heads)

    y = _mm_cast(merged, [w_o[0]], bm=1024, bn=1024, n_cols=d, name="out_proj")
    x1, h2 = _res_ln_mod(x2, y, g1, ln_g[0, 0:1], ln_b[0, 0:1], sc2, sh2, 256)

    g, wd = _mm_cast(h2, [w_gate[0], w_up[0]], bm=1024, bn=512, n_cols=d_ff, epilogue="swiglu",
                     side=w_down[0], name="swiglu_up")
    out = _down_ln(g, wd, x1, g2, ln_g[0, 1:2], ln_b[0, 1:2], bm=512, bk=1024)
    return out.reshape(batch, seq, d)
```

```python
import functools
import math

import jax
import jax.numpy as jnp
from jax import lax
from jax.experimental import pallas as pl
from jax.experimental.pallas import tpu as pltpu

F32 = jnp.float32
BF16 = jnp.bfloat16

GRID_W = 64
HEAD_DIM = 128
CHUNK = 128
WIN_H = 8
WIN_W = 16
LN_EPS = 1e-5
DEPTH = 1
DEEPNORM_ALPHA = (2.0 * DEPTH) ** 0.25

LANES = 128
SUBLANES = 8
VMEM_LIMIT = 56 * 1024 * 1024


def _cparams(sem):
    return pltpu.CompilerParams(dimension_semantics=sem, vmem_limit_bytes=VMEM_LIMIT)


def _gelu(x):
    return 0.5 * x * (1.0 + lax.erf(x * (1.0 / math.sqrt(2.0))))


def _silu(x):
    return x * jax.nn.sigmoid(x)


def _ln_rows(x):
    mu = jnp.mean(x, axis=-1, keepdims=True)
    xc = x - mu
    var = jnp.mean(xc * xc, axis=-1, keepdims=True)
    return xc * lax.rsqrt(var + LN_EPS)


def _rms_rows(x):
    return x * lax.rsqrt(jnp.mean(x * x, axis=-1, keepdims=True) + LN_EPS)


def _mod_kernel(ccol_ref, w_ref, b_ref, o_ref, sb0_ref, sb_ref, *, tn):
    d = w_ref.shape[0]

    @pl.when(pl.program_id(0) == 0)
    def _():
        s = _silu(ccol_ref[...])
        sb_ref[0] = jnp.broadcast_to(s[:, 0:1], (d, LANES))
        sb_ref[1] = jnp.broadcast_to(s[:, 1:2], (d, LANES))
        sb0_ref[...] = sb_ref[0]

    nl = tn // LANES

    def body(kc, accs):
        k0 = pl.multiple_of(kc * SUBLANES, SUBLANES)
        s0 = sb_ref[0, pl.ds(k0, SUBLANES), :]
        s1 = sb_ref[1, pl.ds(k0, SUBLANES), :]
        w = w_ref[pl.ds(k0, SUBLANES), :]
        out = []
        for j in range(nl):
            wj = w[:, j * LANES:(j + 1) * LANES]
            out.append(accs[2 * j] + wj * s0)
            out.append(accs[2 * j + 1] + wj * s1)
        return tuple(out)

    zero = jnp.zeros((SUBLANES, LANES), F32)
    accs = lax.fori_loop(0, d // SUBLANES, body, (zero,) * (2 * nl), unroll=8)
    r0 = jnp.concatenate([jnp.sum(accs[2 * j], axis=0, keepdims=True) for j in range(nl)], axis=1)
    r1 = jnp.concatenate([jnp.sum(accs[2 * j + 1], axis=0, keepdims=True) for j in range(nl)], axis=1)
    row = lax.broadcasted_iota(jnp.int32, (SUBLANES, tn), 0)
    b = b_ref[...]
    o_ref[...] = jnp.where(row == 0, r0 + b, jnp.where(row == 1, r1 + b, 0.0))


def _modulation(ccol, w_ada, b_ada, n):
    d = w_ada.shape[0]
    tn = 512
    return pl.pallas_call(
        functools.partial(_mod_kernel, tn=tn),
        grid=(n // tn,),
        in_specs=[
            pl.BlockSpec((d, 2), lambda j: (0, 0)),
            pl.BlockSpec((d, tn), lambda j: (0, j)),
            pl.BlockSpec((1, tn), lambda j: (0, j)),
        ],
        out_specs=[pl.BlockSpec((SUBLANES, tn), lambda j: (0, j)),
                   pl.BlockSpec((d, LANES), lambda j: (0, 0))],
        out_shape=[jax.ShapeDtypeStruct((SUBLANES, n), F32), jax.ShapeDtypeStruct((d, LANES), F32)],
        scratch_shapes=[pltpu.VMEM((2, d, LANES), F32)],
        compiler_params=_cparams(("arbitrary",)),
        name="modulation",
    )(ccol, w_ada, b_ada)


MOD_ACCS = 4


def _deferred_mod_tile(sb_ref, w_ref, b_ref, o_ref):
    d, tn = w_ref.shape
    nl = tn // LANES
    accs = [[None] * MOD_ACCS for _ in range(nl)]
    for kc in range(d // SUBLANES):
        rows = slice(kc * SUBLANES, (kc + 1) * SUBLANES)
        s = sb_ref[rows, :]
        for jj in range(nl):
            p = w_ref[rows, jj * LANES:(jj + 1) * LANES] * s
            a = kc % MOD_ACCS
            accs[jj][a] = p if accs[jj][a] is None else accs[jj][a] + p
    tot = [jnp.sum(functools.reduce(jnp.add, accs[jj]), axis=0, keepdims=True) for jj in range(nl)]
    o_ref[...] = jnp.broadcast_to(jnp.concatenate(tot, axis=1) + b_ref[...], o_ref.shape)


def _ln_mod_kernel(x_ref, c_ref, mod_ref, o_ref, *, n_x_blocks):
    d = x_ref.shape[1]
    i = pl.program_id(0)

    def emit(src_ref, row):
        sh = mod_ref[row:row + 1, 0:d]
        sc = mod_ref[row:row + 1, d:2 * d]
        o_ref[...] = (_ln_rows(src_ref[...]) * (1.0 + sc) + sh).astype(o_ref.dtype)

    @pl.when(i < n_x_blocks)
    def _():
        emit(x_ref, 0)

    @pl.when(i >= n_x_blocks)
    def _():
        emit(c_ref, 1)


def _ln_mod(x, ctx, mod, tr):
    m, d = x.shape
    n_x_blocks = m // tr
    n_c_blocks = ctx.shape[0] // tr
    return pl.pallas_call(
        functools.partial(_ln_mod_kernel, n_x_blocks=n_x_blocks),
        grid=(n_x_blocks + n_c_blocks,),
        in_specs=[
            pl.BlockSpec((tr, d), lambda i: (jnp.minimum(i, n_x_blocks - 1), 0)),
            pl.BlockSpec((tr, d), lambda i: (jnp.maximum(i - n_x_blocks, 0), 0)),
            pl.BlockSpec((SUBLANES, 2 * d), lambda i: (0, 0)),
        ],
        out_specs=pl.BlockSpec((tr, d), lambda i: (i, 0)),
        out_shape=jax.ShapeDtypeStruct((m + ctx.shape[0], d), BF16),
        compiler_params=_cparams(("arbitrary",)),
        name="ln_mod",
    )(x, ctx, mod)


CAST_ROWS = 256
SIDE_ROWS = 64
MM_CHUNK = 256


def _mm_cast_kernel(*refs, n_w, bn, n_blocks, m_blocks, col0, n_cols, epilogue, scale, has_side, head_major):
    it = iter(refs)
    x_ref = next(it)
    w_hbm = [next(it) for _ in range(n_w)]
    side_in = next(it) if has_side else None
    o_ref = next(it)
    side_out = next(it) if has_side else None
    stage = [next(it) for _ in range(n_w)]
    wb = [next(it) for _ in range(n_w)]
    sem = next(it)

    j = pl.program_id(0)
    i = pl.program_id(1)
    k = stage[0].shape[0]
    w_last = n_cols - (n_blocks - 1) * bn

    def slab_copies(jj, width):
        return [pltpu.make_async_copy(w_hbm[a].at[:, pl.ds(col0 + jj * bn, width)],
                                      stage[a].at[:, pl.ds(0, width)], sem.at[a]) for a in range(n_w)]

    def for_slab(jj, fn):
        if w_last == bn:
            for cp in slab_copies(jj, bn):
                fn(cp)
            return

        @pl.when(jj < n_blocks - 1)
        def _():
            for cp in slab_copies(jj, bn):
                fn(cp)

        @pl.when(jj == n_blocks - 1)
        def _():
            for cp in slab_copies(n_blocks - 1, w_last):
                fn(cp)

    @pl.when((j == 0) & (i == 0))
    def _():
        for_slab(j, lambda cp: cp.start())

    @pl.when(i == 0)
    def _():
        for_slab(j, lambda cp: cp.wait())

        def cast_rows(r, carry):
            r0 = pl.multiple_of(r * CAST_ROWS, CAST_ROWS)
            for a in range(n_w):
                wb[a][pl.ds(r0, CAST_ROWS), :] = stage[a][pl.ds(r0, CAST_ROWS), :].astype(BF16)
            return carry

        lax.fori_loop(0, k // CAST_ROWS, cast_rows, 0)

    @pl.when((i == min(1, m_blocks - 1)) & (j + 1 < n_blocks))
    def _():
        for_slab(j + 1, lambda cp: cp.start())

    def compute(n_chunks):
        x = x_ref[...]
        for c in range(n_chunks):
            cols = slice(c * MM_CHUNK, (c + 1) * MM_CHUNK)
            accs = [jnp.dot(x, wb[a][:, cols], preferred_element_type=F32) for a in range(n_w)]
            if epilogue == "plain":
                y = accs[0]
            elif epilogue == "gelu":
                y = _gelu(accs[0])
            elif epilogue == "scale":
                y = accs[0] * scale
            else:
                y = _silu(accs[0]) * accs[1]
            y = y.astype(o_ref.dtype)
            if head_major:
                for hh in range(MM_CHUNK // HEAD_DIM):
                    o_ref[c * (MM_CHUNK // HEAD_DIM) + hh] = y[:, hh * HEAD_DIM:(hh + 1) * HEAD_DIM]
            else:
                o_ref[:, cols] = y

    full_chunks = bn // MM_CHUNK
    last_chunks = -(-w_last // MM_CHUNK)
    if last_chunks == full_chunks:
        compute(full_chunks)
    else:
        @pl.when(j < n_blocks - 1)
        def _():
            compute(full_chunks)

        @pl.when(j == n_blocks - 1)
        def _():
            compute(last_chunks)

    if side_in is not None:
        side_out[...] = side_in[...].astype(side_out.dtype)


def _mm_cast(x, ws, *, bm, bn, n_cols, m_rows=None, col0=0, epilogue="plain", scale=1.0, side=None,
             head_major=False, name):
    m, k = x.shape
    m = m if m_rows is None else m_rows
    n_w = len(ws)
    assert m % bm == 0 and bn % MM_CHUNK == 0 and k % CAST_ROWS == 0
    n_blocks = -(-n_cols // bn)
    m_blocks = m // bm
    in_specs = [pl.BlockSpec((bm, k), lambda j, i: (i, 0))]
    in_specs += [pl.BlockSpec(memory_space=pl.ANY)] * n_w
    if head_major:
        assert n_cols % bn == 0
        out_specs = [pl.BlockSpec((bn // HEAD_DIM, bm, HEAD_DIM), lambda j, i: (j, i, 0))]
        out_shape = [jax.ShapeDtypeStruct((n_cols // HEAD_DIM, m, HEAD_DIM), BF16)]
    else:
        out_specs = [pl.BlockSpec((bm, bn), lambda j, i: (i, j))]
        out_shape = [jax.ShapeDtypeStruct((m, n_cols), BF16)]
    args = [x, *ws]
    if side is not None:
        rows, c = side.shape
        side_blocks = rows // SIDE_ROWS
        assert side_blocks * SIDE_ROWS == rows and side_blocks <= n_blocks * m_blocks
        side_spec = pl.BlockSpec(
            (SIDE_ROWS, c), lambda j, i: (jnp.minimum(j * m_blocks + i, side_blocks - 1), 0))
        in_specs.append(side_spec)
        out_specs.append(side_spec)
        out_shape.append(jax.ShapeDtypeStruct((rows, c), BF16))
        args.append(side)
    kern = functools.partial(
        _mm_cast_kernel, n_w=n_w, bn=bn, n_blocks=n_blocks, m_blocks=m_blocks, col0=col0, n_cols=n_cols,
        epilogue=epilogue, scale=scale, has_side=side is not None, head_major=head_major)
    out = pl.pallas_call(
        kern,
        grid=(n_blocks, m_blocks),
        in_specs=in_specs,
        out_specs=out_specs,
        out_shape=out_shape,
        scratch_shapes=([pltpu.VMEM((k, bn), F32)] * n_w + [pltpu.VMEM((k, bn), BF16)] * n_w
                        + [pltpu.SemaphoreType.DMA((n_w,))]),
        compiler_params=_cparams(("arbitrary", "arbitrary")),
        name=name,
    )(*args)
    return out if side is not None else out[0]


LN_ROWS = 64


def _down_ln_kernel(g_ref, w_ref, x_ref, gate_ref, lg_ref, lb_ref, o_ref, *, k_blocks, k_last):
    kk = pl.program_id(1)

    def part(kw):
        return jnp.dot(g_ref[:, :kw], w_ref[:kw, :], preferred_element_type=F32)

    bk = w_ref.shape[0]

    @pl.when(kk == 0)
    def _():
        o_ref[...] = part(bk)

    @pl.when((kk > 0) & (kk < k_blocks - 1))
    def _():
        o_ref[...] += part(bk)

    @pl.when(kk == k_blocks - 1)
    def _():
        o_ref[...] += part(k_last)
        gate = gate_ref[...]
        lg = lg_ref[...]
        lb = lb_ref[...]

        def ln_chunk(r, carry):
            rows = pl.ds(pl.multiple_of(r * LN_ROWS, LN_ROWS), LN_ROWS)
            z = DEEPNORM_ALPHA * x_ref[rows, :] + gate * o_ref[rows, :]
            o_ref[rows, :] = _ln_rows(z) * lg + lb
            return carry

        lax.fori_loop(0, o_ref.shape[0] // LN_ROWS, ln_chunk, 0)


def _down_ln(g, w, x, gate, lg, lb, *, bm, bk):
    m, k = g.shape
    n = w.shape[1]
    assert m % bm == 0 and w.shape[0] == k
    k_blocks = -(-k // bk)
    k_last = k - (k_blocks - 1) * bk
    assert k_blocks >= 2 and k_last % LANES == 0
    row = pl.BlockSpec((bm, n), lambda i, kk: (i, 0))
    vec = pl.BlockSpec((1, n), lambda i, kk: (0, 0))
    return pl.pallas_call(
        functools.partial(_down_ln_kernel, k_blocks=k_blocks, k_last=k_last),
        grid=(m // bm, k_blocks),
        in_specs=[
            pl.BlockSpec((bm, bk), lambda i, kk: (i, kk)),
            pl.BlockSpec((bk, n), lambda i, kk: (kk, 0)),
            row, vec, vec, vec,
        ],
        out_specs=row,
        out_shape=jax.ShapeDtypeStruct((m, n), F32),
        compiler_params=_cparams(("arbitrary", "arbitrary")),
        name="swiglu_down_ln",
    )(g, w, x, gate, lg, lb)


N_PAIR_ROWS = 2 * WIN_H - 2


def _bias_kernel(rpb_ref, o_ref):
    h = pl.program_id(0)
    n_dcol = 2 * WIN_W - 1
    q = lax.broadcasted_iota(jnp.int32, (GRID_W, 2 * GRID_W), 0)
    l = lax.broadcasted_iota(jnp.int32, (GRID_W, 2 * GRID_W), 1)
    kc = l & (GRID_W - 1)
    hi = l >= GRID_W
    dcol = jnp.clip(kc - q, -(WIN_W - 1), WIN_W - 1) + (WIN_W - 1)
    cs = jnp.clip(q - WIN_W // 2, 0, GRID_W - WIN_W)
    inside = (kc >= cs) & (kc < cs + WIN_W)
    base = h * ((2 * WIN_H - 1) * n_dcol)

    def body(dra, carry):
        acc = jnp.zeros((GRID_W, 2 * GRID_W), F32)
        for d in range(n_dcol):
            lo_v = rpb_ref[base + dra * n_dcol + d]
            hi_v = rpb_ref[base + (dra + 1) * n_dcol + d]
            acc = jnp.where(dcol == d, jnp.where(hi, hi_v, lo_v), acc)
        o_ref[dra] = jnp.where(inside, acc, -jnp.inf)
        return carry

    lax.fori_loop(0, N_PAIR_ROWS, body, 0)


def _bias_table(rpb_flat, n_heads):
    return pl.pallas_call(
        _bias_kernel,
        grid=(n_heads,),
        in_specs=[pl.BlockSpec(memory_space=pltpu.SMEM)],
        out_specs=pl.BlockSpec((None, N_PAIR_ROWS, GRID_W, 2 * GRID_W), lambda h: (h, 0, 0, 0)),
        out_shape=jax.ShapeDtypeStruct((n_heads, N_PAIR_ROWS, GRID_W, 2 * GRID_W), F32),
        compiler_params=_cparams(("arbitrary",)),
        name="bias_table",
    )(rpb_flat)


ROWS_PER_STEP = 64


def _attn_kernel(q_ref, k_ref, v_ref, kc_ref, vc_ref, tp_ref, sb_ref, wa_ref, ba_ref, o_ref, mod_ref,
                 kt_ref, *, n_rows):
    jb = pl.program_id(1)
    s_keys = k_ref.shape[0]

    @pl.when(jb == 0)
    def _():
        kt_ref[0] = k_ref[...].T
        n_shift = s_keys - LANES
        kt_ref[1, :, :n_shift] = k_ref[GRID_W:GRID_W + n_shift, :].T

    _deferred_mod_tile(sb_ref, wa_ref, ba_ref, mod_ref)

    n_lat = WIN_H * GRID_W
    q = q_ref[...]
    kc = kc_ref[...]
    vc = vc_ref[...]
    dn_t = (((1,), (1,)), ((), ()))
    s_ctx = lax.dot_general(q, kc, dn_t, preferred_element_type=F32)
    starts, s_rows = [], []
    for t in range(ROWS_PER_STEP):
        r = jb * ROWS_PER_STEP + t
        ks = jnp.clip(r - WIN_H // 2, 0, n_rows - WIN_H)
        shift = r - ks
        start = pl.multiple_of(ks * GRID_W, GRID_W)
        starts.append(start)
        lane0 = pl.multiple_of((ks // 2) * LANES, LANES)
        kw_t = kt_ref[ks % 2, :, pl.ds(lane0, n_lat)]
        qt = q[t * GRID_W:(t + 1) * GRID_W]
        s = jnp.dot(qt, kw_t, preferred_element_type=F32)
        bias = jnp.concatenate(
            [tp_ref[2 * p - shift + (WIN_H - 1)] for p in range(WIN_H // 2)], axis=1)
        s_rows.append(s + bias)
    s_lat = jnp.concatenate(s_rows, axis=0)
    tiles = [s_lat[:, c * LANES:(c + 1) * LANES] for c in range(n_lat // LANES)]
    tiles += [s_ctx[:, c * LANES:(c + 1) * LANES] for c in range(s_ctx.shape[1] // LANES)]
    m = jnp.max(functools.reduce(jnp.maximum, tiles), axis=1, keepdims=True)
    e = jnp.exp((s_lat - m).astype(BF16))
    ec = jnp.exp((s_ctx - m).astype(BF16))
    ones = jnp.ones((vc.shape[0], HEAD_DIM), BF16)
    vc_aug = jnp.concatenate([vc, ones], axis=1)
    oc = jnp.dot(ec, vc_aug, preferred_element_type=F32)
    e32 = e.astype(F32)
    e_sum = functools.reduce(jnp.add, [e32[:, c * LANES:(c + 1) * LANES] for c in range(n_lat // LANES)])
    den = jnp.sum(e_sum, axis=1, keepdims=True) + oc[:, HEAD_DIM:]
    for t in range(ROWS_PER_STEP):
        rows = slice(t * GRID_W, (t + 1) * GRID_W)
        vw = v_ref[pl.ds(starts[t], n_lat), :]
        o = jnp.dot(e[rows], vw, preferred_element_type=F32) + oc[rows, :HEAD_DIM]
        o_ref[rows, :] = (o / den[rows]).astype(o_ref.dtype)


def _attention(q, kv, tp, sb0, w_ada, b_ada, mod_col0):
    n_heads, s, _ = q.shape
    n_ctx = kv.shape[1] - s
    assert s % n_ctx == 0
    ctx_blk = s // n_ctx
    n_rows = s // GRID_W
    tq = ROWS_PER_STEP * GRID_W
    bph = n_rows // ROWS_PER_STEP
    d, n_mod = w_ada.shape
    n_side = n_mod - mod_col0
    tn = n_side // (n_heads * bph)
    assert tn * n_heads * bph == n_side and tn % LANES == 0 and mod_col0 % tn == 0
    blk0 = mod_col0 // tn
    return pl.pallas_call(
        functools.partial(_attn_kernel, n_rows=n_rows),
        grid=(n_heads, bph),
        in_specs=[
            pl.BlockSpec((None, tq, HEAD_DIM), lambda h, j: (h, j, 0)),
            pl.BlockSpec((None, s, HEAD_DIM), lambda h, j: (h, 0, 0)),
            pl.BlockSpec((None, s, HEAD_DIM), lambda h, j: (n_heads + h, 0, 0)),
            pl.BlockSpec((None, n_ctx, HEAD_DIM), lambda h, j: (h, ctx_blk, 0)),
            pl.BlockSpec((None, n_ctx, HEAD_DIM), lambda h, j: (n_heads + h, ctx_blk, 0)),
            pl.BlockSpec((None, N_PAIR_ROWS, GRID_W, 2 * GRID_W), lambda h, j: (h, 0, 0, 0)),
            pl.BlockSpec((d, LANES), lambda h, j: (0, 0)),
            pl.BlockSpec((d, tn), lambda h, j: (0, blk0 + h * bph + j)),
            pl.BlockSpec((1, tn), lambda h, j: (0, blk0 + h * bph + j)),
        ],
        out_specs=[pl.BlockSpec((None, tq, HEAD_DIM), lambda h, j: (h, j, 0)),
                   pl.BlockSpec((SUBLANES, tn), lambda h, j: (0, h * bph + j))],
        out_shape=[jax.ShapeDtypeStruct((n_heads, s, HEAD_DIM), BF16),
                   jax.ShapeDtypeStruct((SUBLANES, n_side), F32)],
        scratch_shapes=[pltpu.VMEM((2, HEAD_DIM, s), BF16)],
        compiler_params=_cparams(("arbitrary", "arbitrary")),
        name="nbr_attention",
    )(q, kv, kv, kv, kv, tp, sb0, w_ada, b_ada)


SGU_CHUNKS_PER_STEP = 4


def _sgu_merge_kernel(gu_ref, gv_ref, ob_ref, ws_ref, bst_ref, lg_ref, lb_ref, gn_ref, o_ref, oa_ref,
                      *, n_heads, w_sgu):
    v = _ln_rows(gv_ref[...].astype(F32)) * lg_ref[...] + lb_ref[...]
    v = v.astype(BF16)
    bst = bst_ref[...]
    for h in range(n_heads):
        cols = slice(h * HEAD_DIM, (h + 1) * HEAD_DIM)
        vh = jnp.concatenate(
            [v[c * CHUNK:(c + 1) * CHUNK, cols] for c in range(SGU_CHUNKS_PER_STEP)], axis=1)
        mixed = jnp.dot(ws_ref[h], vh, preferred_element_type=F32) + bst[:, h:h + 1]
        for c in range(SGU_CHUNKS_PER_STEP):
            rows = slice(c * CHUNK, (c + 1) * CHUNK)
            oa_ref[rows, cols] = gu_ref[rows, cols].astype(F32) * mixed[:, c * HEAD_DIM:(c + 1) * HEAD_DIM]
    gn = gn_ref[...]
    o_ref[:, :w_sgu] = (_rms_rows(oa_ref[...]) * gn[:, :w_sgu]).astype(o_ref.dtype)
    ob = jnp.concatenate([ob_ref[h] for h in range(ob_ref.shape[0])], axis=1).astype(F32)
    o_ref[:, w_sgu:] = (_rms_rows(ob) * gn[:, w_sgu:]).astype(o_ref.dtype)


def _sgu_merge(p, ob, ws, bst, lg, lb, gn, *, n_heads):
    s = p.shape[0]
    w_sgu = n_heads * HEAD_DIM
    n_na_heads = ob.shape[0]
    w_na = n_na_heads * HEAD_DIM
    tr = SGU_CHUNKS_PER_STEP * CHUNK
    return pl.pallas_call(
        functools.partial(_sgu_merge_kernel, n_heads=n_heads, w_sgu=w_sgu),
        grid=(s // tr,),
        in_specs=[
            pl.BlockSpec((tr, w_sgu), lambda i: (i, 0)),
            pl.BlockSpec((tr, w_sgu), lambda i: (i, 1)),
            pl.BlockSpec((n_na_heads, tr, HEAD_DIM), lambda i: (0, i, 0)),
            pl.BlockSpec((n_heads, CHUNK, CHUNK), lambda i: (0, 0, 0)),
            pl.BlockSpec((CHUNK, n_heads), lambda i: (0, 0)),
            pl.BlockSpec((1, w_sgu), lambda i: (0, 0)),
            pl.BlockSpec((1, w_sgu), lambda i: (0, 0)),
            pl.BlockSpec((1, w_sgu + w_na), lambda i: (0, 0)),
        ],
        out_specs=pl.BlockSpec((tr, w_sgu + w_na), lambda i: (i, 0)),
        out_shape=jax.ShapeDtypeStruct((s, w_sgu + w_na), BF16),
        scratch_shapes=[pltpu.VMEM((tr, w_sgu), F32)],
        compiler_params=_cparams(("parallel",)),
        name="sgu_merge",
    )(p, p, ob, ws, bst, lg, lb, gn)


def _res_ln_mod_kernel(x_ref, y_ref, g_ref, lg_ref, lb_ref, sc_ref, sh_ref, x1_ref, h_ref):
    xn = _ln_rows(x_ref[...])
    z = DEEPNORM_ALPHA * xn + g_ref[...] * y_ref[...].astype(F32)
    x1 = _ln_rows(z) * lg_ref[...] + lb_ref[...]
    x1_ref[...] = x1
    h_ref[...] = (x1 * (1.0 + sc_ref[...]) + sh_ref[...]).astype(h_ref.dtype)


def _res_ln_mod(x, y, gate, lg, lb, sc, sh, tr):
    m, d = x.shape
    row = pl.BlockSpec((tr, d), lambda i: (i, 0))
    vec = pl.BlockSpec((1, d), lambda i: (0, 0))
    return pl.pallas_call(
        _res_ln_mod_kernel,
        grid=(m // tr,),
        in_specs=[row, row, vec, vec, vec, vec, vec],
        out_specs=[row, row],
        out_shape=[jax.ShapeDtypeStruct((m, d), F32), jax.ShapeDtypeStruct((m, d), BF16)],
        compiler_params=_cparams(("parallel",)),
        name="res_ln_mod",
    )(x, y, gate, lg, lb, sc, sh)


def _tiles(seq, n_ctx):
    return dict(row=256, proj_bm=1024, proj_bn=1024, kv_bm=(seq + n_ctx) // 8, ff_bn=512,
                down_bm=512, down_bk=1024)


def kernel(x, c, ctx, c_ctx, w_ada, b_ada, w_in, w_s, b_s, sgu_g, sgu_b, rpb, gn_g, w_o, ln_g, ln_b,
           w_gate, w_up, w_down):
    batch, seq, d = x.shape
    assert batch == 1 and w_ada.shape[0] == DEPTH
    t = _tiles(seq, ctx.shape[1])
    n_sgu_heads = w_s.shape[1]
    n_na_heads = rpb.shape[1]
    w_sgu = n_sgu_heads * HEAD_DIM
    w_na = n_na_heads * HEAD_DIM
    d_ff = w_gate.shape[2]
    assert w_in.shape[2] == 2 * w_sgu + 3 * w_na and w_sgu == w_na

    x2 = x.reshape(seq, d)
    ctx2 = ctx.reshape(ctx.shape[1], d)

    ccol = jnp.stack([c.reshape(d), c_ctx], axis=1)
    b_ada2 = b_ada[0].reshape(1, 6 * d)
    mod, sb0 = _modulation(ccol, w_ada[0], b_ada2, 2 * d)

    h = _ln_mod(x2, ctx2, mod, t["row"])

    bm, bn = t["proj_bm"], t["proj_bn"]
    guv = _mm_cast(h, [w_in[0]], m_rows=seq, bm=bm, bn=bn, n_cols=2 * w_sgu, epilogue="gelu",
                   name="in_proj_uv")
    q = _mm_cast(h, [w_in[0]], m_rows=seq, bm=bm, bn=bn, n_cols=w_na, col0=2 * w_sgu,
                 epilogue="scale", scale=HEAD_DIM ** -0.5, head_major=True, name="in_proj_q")
    kv = _mm_cast(h, [w_in[0]], bm=t["kv_bm"], bn=bn, n_cols=2 * w_na, col0=2 * w_sgu + w_na,
                  head_major=True, name="in_proj_kv")

    tp = _bias_table(rpb[0].reshape(-1), n_na_heads)
    ob, mod_b = _attention(q, kv, tp, sb0, w_ada[0], b_ada2, 2 * d)
    g1, sh2, sc2, g2 = [mod_b[0:1, i * d:(i + 1) * d] for i in range(4)]

    merged = _sgu_merge(guv, ob, w_s[0].astype(BF16), b_s[0].T, sgu_g, sgu_b, gn_g, n_heads=n_sgu_heads)

    y = _mm_cast(merged, [w_o[0]], bm=bm, bn=bn, n_cols=d, name="out_proj")
    x1, h2 = _res_ln_mod(x2, y, g1, ln_g[0, 0:1], ln_b[0, 0:1], sc2, sh2, t["row"])

    g, wd = _mm_cast(h2, [w_gate[0], w_up[0]], bm=bm, bn=t["ff_bn"], n_cols=d_ff, epilogue="swiglu",
                     side=w_down[0], name="swiglu_up")
    out = _down_ln(g, wd, x1, g2, ln_g[0, 1:2], ln_b[0, 1:2], bm=t["down_bm"], bk=t["down_bk"])
    return out.reshape(batch, seq, d)
```

```python
import functools
import math

import jax
import jax.numpy as jnp
from jax import lax
from jax.experimental import pallas as pl
from jax.experimental.pallas import tpu as pltpu

F32 = jnp.float32
BF16 = jnp.bfloat16

GRID_W = 64
HEAD_DIM = 128
CHUNK = 128
WIN_H = 8
WIN_W = 16
LN_EPS = 1e-5
DEPTH = 1
DEEPNORM_ALPHA = (2.0 * DEPTH) ** 0.25
LOG2_E = math.log2(math.e)

LANES = 128
SUBLANES = 8
VMEM_LIMIT = 56 * 1024 * 1024


def _cparams(sem):
    return pltpu.CompilerParams(dimension_semantics=sem, vmem_limit_bytes=VMEM_LIMIT)


def _gelu(x):
    return 0.5 * x * (1.0 + lax.erf(x * (1.0 / math.sqrt(2.0))))


def _silu(x):
    return x * jax.nn.sigmoid(x)


def _ln_rows(x):
    mu = jnp.mean(x, axis=-1, keepdims=True)
    xc = x - mu
    var = jnp.mean(xc * xc, axis=-1, keepdims=True)
    return xc * lax.rsqrt(var + LN_EPS)


def _rms_rows(x):
    return x * lax.rsqrt(jnp.mean(x * x, axis=-1, keepdims=True) + LN_EPS)


def _mod_kernel(ccol_ref, w_ref, b_ref, o_ref, sb0_ref, sb_ref, *, tn):
    d = w_ref.shape[0]

    @pl.when(pl.program_id(0) == 0)
    def _():
        s = _silu(ccol_ref[...])
        sb_ref[0] = jnp.broadcast_to(s[:, 0:1], (d, LANES))
        sb_ref[1] = jnp.broadcast_to(s[:, 1:2], (d, LANES))
        sb0_ref[...] = sb_ref[0]

    nl = tn // LANES

    def body(kc, accs):
        k0 = pl.multiple_of(kc * SUBLANES, SUBLANES)
        s0 = sb_ref[0, pl.ds(k0, SUBLANES), :]
        s1 = sb_ref[1, pl.ds(k0, SUBLANES), :]
        w = w_ref[pl.ds(k0, SUBLANES), :]
        out = []
        for j in range(nl):
            wj = w[:, j * LANES:(j + 1) * LANES]
            out.append(accs[2 * j] + wj * s0)
            out.append(accs[2 * j + 1] + wj * s1)
        return tuple(out)

    zero = jnp.zeros((SUBLANES, LANES), F32)
    accs = lax.fori_loop(0, d // SUBLANES, body, (zero,) * (2 * nl), unroll=8)
    r0 = jnp.concatenate([jnp.sum(accs[2 * j], axis=0, keepdims=True) for j in range(nl)], axis=1)
    r1 = jnp.concatenate([jnp.sum(accs[2 * j + 1], axis=0, keepdims=True) for j in range(nl)], axis=1)
    row = lax.broadcasted_iota(jnp.int32, (SUBLANES, tn), 0)
    b = b_ref[...]
    o_ref[...] = jnp.where(row == 0, r0 + b, jnp.where(row == 1, r1 + b, 0.0))


def _modulation(ccol, w_ada, b_ada, n):
    d = w_ada.shape[0]
    tn = 512
    return pl.pallas_call(
        functools.partial(_mod_kernel, tn=tn),
        grid=(n // tn,),
        in_specs=[
            pl.BlockSpec((d, 2), lambda j: (0, 0)),
            pl.BlockSpec((d, tn), lambda j: (0, j)),
            pl.BlockSpec((1, tn), lambda j: (0, j)),
        ],
        out_specs=[pl.BlockSpec((SUBLANES, tn), lambda j: (0, j)),
                   pl.BlockSpec((d, LANES), lambda j: (0, 0))],
        out_shape=[jax.ShapeDtypeStruct((SUBLANES, n), F32), jax.ShapeDtypeStruct((d, LANES), F32)],
        scratch_shapes=[pltpu.VMEM((2, d, LANES), F32)],
        compiler_params=_cparams(("arbitrary",)),
        name="modulation",
    )(ccol, w_ada, b_ada)


MOD_ACCS = 4


def _deferred_mod_tile(sb_ref, w_ref, b_ref, o_ref):
    d, tn = w_ref.shape
    nl = tn // LANES
    accs = [[None] * MOD_ACCS for _ in range(nl)]
    for kc in range(d // SUBLANES):
        rows = slice(kc * SUBLANES, (kc + 1) * SUBLANES)
        s = sb_ref[rows, :]
        for jj in range(nl):
            p = w_ref[rows, jj * LANES:(jj + 1) * LANES] * s
            a = kc % MOD_ACCS
            accs[jj][a] = p if accs[jj][a] is None else accs[jj][a] + p
    tot = [jnp.sum(functools.reduce(jnp.add, accs[jj]), axis=0, keepdims=True) for jj in range(nl)]
    o_ref[...] = jnp.broadcast_to(jnp.concatenate(tot, axis=1) + b_ref[...], o_ref.shape)


def _ln_mod_kernel(x_ref, c_ref, mod_ref, o_ref, st_ref, *, n_x_blocks):
    d = x_ref.shape[1]
    i = pl.program_id(0)

    def emit(src_ref, row):
        sh = mod_ref[row:row + 1, 0:d]
        sc = mod_ref[row:row + 1, d:2 * d]
        x = src_ref[...]
        mu = jnp.mean(x, axis=-1, keepdims=True)
        xc = x - mu
        rstd = lax.rsqrt(jnp.mean(xc * xc, axis=-1, keepdims=True) + LN_EPS)
        o_ref[...] = (xc * rstd * (1.0 + sc) + sh).astype(o_ref.dtype)
        lane = lax.broadcasted_iota(jnp.int32, st_ref.shape, 1)
        st_ref[...] = jnp.where(lane == 0, mu, rstd)

    @pl.when(i < n_x_blocks)
    def _():
        emit(x_ref, 0)

    @pl.when(i >= n_x_blocks)
    def _():
        emit(c_ref, 1)


def _ln_mod(x, ctx, mod, tr):
    m, d = x.shape
    n_x_blocks = m // tr
    n_c_blocks = ctx.shape[0] // tr
    n_rows = m + ctx.shape[0]
    return pl.pallas_call(
        functools.partial(_ln_mod_kernel, n_x_blocks=n_x_blocks),
        grid=(n_x_blocks + n_c_blocks,),
        in_specs=[
            pl.BlockSpec((tr, d), lambda i: (jnp.minimum(i, n_x_blocks - 1), 0)),
            pl.BlockSpec((tr, d), lambda i: (jnp.maximum(i - n_x_blocks, 0), 0)),
            pl.BlockSpec((SUBLANES, 2 * d), lambda i: (0, 0)),
        ],
        out_specs=[pl.BlockSpec((tr, d), lambda i: (i, 0)), pl.BlockSpec((tr, LANES), lambda i: (i, 0))],
        out_shape=[jax.ShapeDtypeStruct((n_rows, d), BF16), jax.ShapeDtypeStruct((n_rows, LANES), F32)],
        compiler_params=_cparams(("arbitrary",)),
        name="ln_mod",
    )(x, ctx, mod)


CAST_ROWS = 256
SIDE_ROWS = 64
MM_CHUNK = 256


def _mm_cast_kernel(*refs, n_w, bn, n_blocks, m_blocks, col0, n_cols, epilogue, scale, has_side, head_major):
    it = iter(refs)
    x_ref = next(it)
    w_hbm = [next(it) for _ in range(n_w)]
    side_in = next(it) if has_side else None
    o_ref = next(it)
    side_out = next(it) if has_side else None
    stage = [next(it) for _ in range(n_w)]
    wb = [next(it) for _ in range(n_w)]
    sem = next(it)

    j = pl.program_id(0)
    i = pl.program_id(1)
    k = stage[0].shape[0]
    w_last = n_cols - (n_blocks - 1) * bn

    def slab_copies(jj, width):
        return [pltpu.make_async_copy(w_hbm[a].at[:, pl.ds(col0 + jj * bn, width)],
                                      stage[a].at[:, pl.ds(0, width)], sem.at[a]) for a in range(n_w)]

    def for_slab(jj, fn):
        if w_last == bn:
            for cp in slab_copies(jj, bn):
                fn(cp)
            return

        @pl.when(jj < n_blocks - 1)
        def _():
            for cp in slab_copies(jj, bn):
                fn(cp)

        @pl.when(jj == n_blocks - 1)
        def _():
            for cp in slab_copies(n_blocks - 1, w_last):
                fn(cp)

    @pl.when((j == 0) & (i == 0))
    def _():
        for_slab(j, lambda cp: cp.start())

    @pl.when(i == 0)
    def _():
        for_slab(j, lambda cp: cp.wait())

        def cast_rows(r, carry):
            r0 = pl.multiple_of(r * CAST_ROWS, CAST_ROWS)
            for a in range(n_w):
                wb[a][pl.ds(r0, CAST_ROWS), :] = stage[a][pl.ds(r0, CAST_ROWS), :].astype(BF16)
            return carry

        lax.fori_loop(0, k // CAST_ROWS, cast_rows, 0)

    @pl.when((i == min(1, m_blocks - 1)) & (j + 1 < n_blocks))
    def _():
        for_slab(j + 1, lambda cp: cp.start())

    def compute(n_chunks):
        x = x_ref[...]
        for c in range(n_chunks):
            cols = slice(c * MM_CHUNK, (c + 1) * MM_CHUNK)
            accs = [jnp.dot(x, wb[a][:, cols], preferred_element_type=F32) for a in range(n_w)]
            if epilogue == "plain":
                y = accs[0]
            elif epilogue == "gelu":
                y = _gelu(accs[0])
            elif epilogue == "scale":
                y = accs[0] * scale
            else:
                y = _silu(accs[0]) * accs[1]
            y = y.astype(o_ref.dtype)
            if head_major:
                for hh in range(MM_CHUNK // HEAD_DIM):
                    o_ref[c * (MM_CHUNK // HEAD_DIM) + hh] = y[:, hh * HEAD_DIM:(hh + 1) * HEAD_DIM]
            else:
                o_ref[:, cols] = y

    full_chunks = bn // MM_CHUNK
    last_chunks = -(-w_last // MM_CHUNK)
    if last_chunks == full_chunks:
        compute(full_chunks)
    else:
        @pl.when(j < n_blocks - 1)
        def _():
            compute(full_chunks)

        @pl.when(j == n_blocks - 1)
        def _():
            compute(last_chunks)

    if side_in is not None:
        side_out[...] = side_in[...].astype(side_out.dtype)


def _mm_cast(x, ws, *, bm, bn, n_cols, m_rows=None, col0=0, epilogue="plain", scale=1.0, side=None,
             head_major=False, name):
    m, k = x.shape
    m = m if m_rows is None else m_rows
    n_w = len(ws)
    assert m % bm == 0 and bn % MM_CHUNK == 0 and k % CAST_ROWS == 0
    n_blocks = -(-n_cols // bn)
    m_blocks = m // bm
    in_specs = [pl.BlockSpec((bm, k), lambda j, i: (i, 0))]
    in_specs += [pl.BlockSpec(memory_space=pl.ANY)] * n_w
    if head_major:
        assert n_cols % bn == 0
        out_specs = [pl.BlockSpec((bn // HEAD_DIM, bm, HEAD_DIM), lambda j, i: (j, i, 0))]
        out_shape = [jax.ShapeDtypeStruct((n_cols // HEAD_DIM, m, HEAD_DIM), BF16)]
    else:
        out_specs = [pl.BlockSpec((bm, bn), lambda j, i: (i, j))]
        out_shape = [jax.ShapeDtypeStruct((m, n_cols), BF16)]
    args = [x, *ws]
    if side is not None:
        rows, c = side.shape
        side_blocks = rows // SIDE_ROWS
        assert side_blocks * SIDE_ROWS == rows and side_blocks <= n_blocks * m_blocks
        side_spec = pl.BlockSpec(
            (SIDE_ROWS, c), lambda j, i: (jnp.minimum(j * m_blocks + i, side_blocks - 1), 0))
        in_specs.append(side_spec)
        out_specs.append(side_spec)
        out_shape.append(jax.ShapeDtypeStruct((rows, c), BF16))
        args.append(side)
    kern = functools.partial(
        _mm_cast_kernel, n_w=n_w, bn=bn, n_blocks=n_blocks, m_blocks=m_blocks, col0=col0, n_cols=n_cols,
        epilogue=epilogue, scale=scale, has_side=side is not None, head_major=head_major)
    out = pl.pallas_call(
        kern,
        grid=(n_blocks, m_blocks),
        in_specs=in_specs,
        out_specs=out_specs,
        out_shape=out_shape,
        scratch_shapes=([pltpu.VMEM((k, bn), F32)] * n_w + [pltpu.VMEM((k, bn), BF16)] * n_w
                        + [pltpu.SemaphoreType.DMA((n_w,))]),
        compiler_params=_cparams(("arbitrary", "arbitrary")),
        name=name,
    )(*args)
    return out if side is not None else out[0]


LN_ROWS = 64


def _down_ln_kernel(g_ref, w_ref, x_ref, gate_ref, lg_ref, lb_ref, o_ref, *, k_blocks, k_last):
    kk = pl.program_id(1)

    def part(kw):
        return jnp.dot(g_ref[:, :kw], w_ref[:kw, :], preferred_element_type=F32)

    bk = w_ref.shape[0]

    @pl.when(kk == 0)
    def _():
        o_ref[...] = part(bk)

    @pl.when((kk > 0) & (kk < k_blocks - 1))
    def _():
        o_ref[...] += part(bk)

    @pl.when(kk == k_blocks - 1)
    def _():
        o_ref[...] += part(k_last)
        gate = gate_ref[...]
        lg = lg_ref[...]
        lb = lb_ref[...]

        def ln_chunk(r, carry):
            rows = pl.ds(pl.multiple_of(r * LN_ROWS, LN_ROWS), LN_ROWS)
            z = DEEPNORM_ALPHA * x_ref[rows, :] + gate * o_ref[rows, :]
            o_ref[rows, :] = _ln_rows(z) * lg + lb
            return carry

        lax.fori_loop(0, o_ref.shape[0] // LN_ROWS, ln_chunk, 0)


def _down_ln(g, w, x, gate, lg, lb, *, bm, bk):
    m, k = g.shape
    n = w.shape[1]
    assert m % bm == 0 and w.shape[0] == k
    k_blocks = -(-k // bk)
    k_last = k - (k_blocks - 1) * bk
    assert k_blocks >= 2 and k_last % LANES == 0
    row = pl.BlockSpec((bm, n), lambda i, kk: (i, 0))
    vec = pl.BlockSpec((1, n), lambda i, kk: (0, 0))
    return pl.pallas_call(
        functools.partial(_down_ln_kernel, k_blocks=k_blocks, k_last=k_last),
        grid=(m // bm, k_blocks),
        in_specs=[
            pl.BlockSpec((bm, bk), lambda i, kk: (i, kk)),
            pl.BlockSpec((bk, n), lambda i, kk: (kk, 0)),
            row, vec, vec, vec,
        ],
        out_specs=row,
        out_shape=jax.ShapeDtypeStruct((m, n), F32),
        compiler_params=_cparams(("arbitrary", "arbitrary")),
        name="swiglu_down_ln",
    )(g, w, x, gate, lg, lb)


N_PAIR_ROWS = 2 * WIN_H - 2


def _bias_kernel(rpb_ref, o_ref):
    h = pl.program_id(0)
    n_dcol = 2 * WIN_W - 1
    q = lax.broadcasted_iota(jnp.int32, (GRID_W, 2 * GRID_W), 0)
    l = lax.broadcasted_iota(jnp.int32, (GRID_W, 2 * GRID_W), 1)
    kc = l & (GRID_W - 1)
    hi = l >= GRID_W
    dcol = jnp.clip(kc - q, -(WIN_W - 1), WIN_W - 1) + (WIN_W - 1)
    cs = jnp.clip(q - WIN_W // 2, 0, GRID_W - WIN_W)
    inside = (kc >= cs) & (kc < cs + WIN_W)
    base = h * ((2 * WIN_H - 1) * n_dcol)

    def body(dra, carry):
        acc = jnp.zeros((GRID_W, 2 * GRID_W), F32)
        for d in range(n_dcol):
            lo_v = rpb_ref[base + dra * n_dcol + d]
            hi_v = rpb_ref[base + (dra + 1) * n_dcol + d]
            acc = jnp.where(dcol == d, jnp.where(hi, hi_v, lo_v), acc)
        o_ref[dra] = jnp.where(inside, acc * LOG2_E, -jnp.inf)
        return carry

    lax.fori_loop(0, N_PAIR_ROWS, body, 0)


def _bias_table(rpb_flat, n_heads):
    return pl.pallas_call(
        _bias_kernel,
        grid=(n_heads,),
        in_specs=[pl.BlockSpec(memory_space=pltpu.SMEM)],
        out_specs=pl.BlockSpec((None, N_PAIR_ROWS, GRID_W, 2 * GRID_W), lambda h: (h, 0, 0, 0)),
        out_shape=jax.ShapeDtypeStruct((n_heads, N_PAIR_ROWS, GRID_W, 2 * GRID_W), F32),
        compiler_params=_cparams(("arbitrary",)),
        name="bias_table",
    )(rpb_flat)


ROWS_PER_STEP = 64


def _attn_kernel(q_ref, k_ref, v_ref, kc_ref, vc_ref, tp_ref, sb_ref, wa_ref, ba_ref, o_ref, mod_ref,
                 kt_ref, *, n_rows):
    jb = pl.program_id(1)
    s_keys = k_ref.shape[0]

    @pl.when(jb == 0)
    def _():
        kt_ref[0] = k_ref[...].T
        n_shift = s_keys - LANES
        kt_ref[1, :, :n_shift] = k_ref[GRID_W:GRID_W + n_shift, :].T

    _deferred_mod_tile(sb_ref, wa_ref, ba_ref, mod_ref)

    n_lat = WIN_H * GRID_W
    q = q_ref[...]
    kc = kc_ref[...]
    vc = vc_ref[...]
    dn_t = (((1,), (1,)), ((), ()))
    s_ctx = lax.dot_general(q, kc, dn_t, preferred_element_type=F32)
    starts, s_rows = [], []
    for t in range(ROWS_PER_STEP):
        r = jb * ROWS_PER_STEP + t
        ks = jnp.clip(r - WIN_H // 2, 0, n_rows - WIN_H)
        shift = r - ks
        start = pl.multiple_of(ks * GRID_W, GRID_W)
        starts.append(start)
        lane0 = pl.multiple_of((ks // 2) * LANES, LANES)
        kw_t = kt_ref[ks % 2, :, pl.ds(lane0, n_lat)]
        qt = q[t * GRID_W:(t + 1) * GRID_W]
        s = jnp.dot(qt, kw_t, preferred_element_type=F32)
        bias = jnp.concatenate(
            [tp_ref[2 * p - shift + (WIN_H - 1)] for p in range(WIN_H // 2)], axis=1)
        s_rows.append(s + bias)
    s_lat = jnp.concatenate(s_rows, axis=0)
    tiles = [s_lat[:, c * LANES:(c + 1) * LANES] for c in range(n_lat // LANES)]
    tiles += [s_ctx[:, c * LANES:(c + 1) * LANES] for c in range(s_ctx.shape[1] // LANES)]
    m = jnp.max(functools.reduce(jnp.maximum, tiles), axis=1, keepdims=True)
    e = jnp.exp2((s_lat - m).astype(BF16))
    ec = jnp.exp2((s_ctx - m).astype(BF16))
    ones = jnp.ones((vc.shape[0], HEAD_DIM), BF16)
    vc_aug = jnp.concatenate([vc, ones], axis=1)
    oc = jnp.dot(ec, vc_aug, preferred_element_type=F32)
    e32 = e.astype(F32)
    e_sum = functools.reduce(jnp.add, [e32[:, c * LANES:(c + 1) * LANES] for c in range(n_lat // LANES)])
    den = jnp.sum(e_sum, axis=1, keepdims=True) + oc[:, HEAD_DIM:]
    for t in range(ROWS_PER_STEP):
        rows = slice(t * GRID_W, (t + 1) * GRID_W)
        vw = v_ref[pl.ds(starts[t], n_lat), :]
        o = jnp.dot(e[rows], vw, preferred_element_type=F32) + oc[rows, :HEAD_DIM]
        o_ref[rows, :] = (o / den[rows]).astype(o_ref.dtype)


def _attention(q, kv, tp, sb0, w_ada, b_ada, mod_col0):
    n_heads, s, _ = q.shape
    n_ctx = kv.shape[1] - s
    assert s % n_ctx == 0
    ctx_blk = s // n_ctx
    n_rows = s // GRID_W
    tq = ROWS_PER_STEP * GRID_W
    bph = n_rows // ROWS_PER_STEP
    d, n_mod = w_ada.shape
    n_side = n_mod - mod_col0
    tn = n_side // (n_heads * bph)
    assert tn * n_heads * bph == n_side and tn % LANES == 0 and mod_col0 % tn == 0
    blk0 = mod_col0 // tn
    return pl.pallas_call(
        functools.partial(_attn_kernel, n_rows=n_rows),
        grid=(n_heads, bph),
        in_specs=[
            pl.BlockSpec((None, tq, HEAD_DIM), lambda h, j: (h, j, 0)),
            pl.BlockSpec((None, s, HEAD_DIM), lambda h, j: (h, 0, 0)),
            pl.BlockSpec((None, s, HEAD_DIM), lambda h, j: (n_heads + h, 0, 0)),
            pl.BlockSpec((None, n_ctx, HEAD_DIM), lambda h, j: (h, ctx_blk, 0)),
            pl.BlockSpec((None, n_ctx, HEAD_DIM), lambda h, j: (n_heads + h, ctx_blk, 0)),
            pl.BlockSpec((None, N_PAIR_ROWS, GRID_W, 2 * GRID_W), lambda h, j: (h, 0, 0, 0)),
            pl.BlockSpec((d, LANES), lambda h, j: (0, 0)),
            pl.BlockSpec((d, tn), lambda h, j: (0, blk0 + h * bph + j)),
            pl.BlockSpec((1, tn), lambda h, j: (0, blk0 + h * bph + j)),
        ],
        out_specs=[pl.BlockSpec((None, tq, HEAD_DIM), lambda h, j: (h, j, 0)),
                   pl.BlockSpec((SUBLANES, tn), lambda h, j: (0, h * bph + j))],
        out_shape=[jax.ShapeDtypeStruct((n_heads, s, HEAD_DIM), BF16),
                   jax.ShapeDtypeStruct((SUBLANES, n_side), F32)],
        scratch_shapes=[pltpu.VMEM((2, HEAD_DIM, s), BF16)],
        compiler_params=_cparams(("arbitrary", "arbitrary")),
        name="nbr_attention",
    )(q, kv, kv, kv, kv, tp, sb0, w_ada, b_ada)


SGU_CHUNKS_PER_STEP = 4


def _sgu_merge_kernel(gu_ref, gv_ref, ob_ref, ws_ref, bst_ref, lg_ref, lb_ref, gn_ref, o_ref, oa_ref,
                      *, n_heads, w_sgu):
    v = _ln_rows(gv_ref[...].astype(F32)) * lg_ref[...] + lb_ref[...]
    v = v.astype(BF16)
    bst = bst_ref[...]
    for h in range(n_heads):
        cols = slice(h * HEAD_DIM, (h + 1) * HEAD_DIM)
        vh = jnp.concatenate(
            [v[c * CHUNK:(c + 1) * CHUNK, cols] for c in range(SGU_CHUNKS_PER_STEP)], axis=1)
        mixed = jnp.dot(ws_ref[h], vh, preferred_element_type=F32) + bst[:, h:h + 1]
        for c in range(SGU_CHUNKS_PER_STEP):
            rows = slice(c * CHUNK, (c + 1) * CHUNK)
            oa_ref[rows, cols] = gu_ref[rows, cols].astype(F32) * mixed[:, c * HEAD_DIM:(c + 1) * HEAD_DIM]
    gn = gn_ref[...]
    o_ref[:, :w_sgu] = (_rms_rows(oa_ref[...]) * gn[:, :w_sgu]).astype(o_ref.dtype)
    ob = jnp.concatenate([ob_ref[h] for h in range(ob_ref.shape[0])], axis=1).astype(F32)
    o_ref[:, w_sgu:] = (_rms_rows(ob) * gn[:, w_sgu:]).astype(o_ref.dtype)


def _sgu_merge(p, ob, ws, bst, lg, lb, gn, *, n_heads):
    s = p.shape[0]
    w_sgu = n_heads * HEAD_DIM
    n_na_heads = ob.shape[0]
    w_na = n_na_heads * HEAD_DIM
    tr = SGU_CHUNKS_PER_STEP * CHUNK
    return pl.pallas_call(
        functools.partial(_sgu_merge_kernel, n_heads=n_heads, w_sgu=w_sgu),
        grid=(s // tr,),
        in_specs=[
            pl.BlockSpec((tr, w_sgu), lambda i: (i, 0)),
            pl.BlockSpec((tr, w_sgu), lambda i: (i, 1)),
            pl.BlockSpec((n_na_heads, tr, HEAD_DIM), lambda i: (0, i, 0)),
            pl.BlockSpec((n_heads, CHUNK, CHUNK), lambda i: (0, 0, 0)),
            pl.BlockSpec((CHUNK, n_heads), lambda i: (0, 0)),
            pl.BlockSpec((1, w_sgu), lambda i: (0, 0)),
            pl.BlockSpec((1, w_sgu), lambda i: (0, 0)),
            pl.BlockSpec((1, w_sgu + w_na), lambda i: (0, 0)),
        ],
        out_specs=pl.BlockSpec((tr, w_sgu + w_na), lambda i: (i, 0)),
        out_shape=jax.ShapeDtypeStruct((s, w_sgu + w_na), BF16),
        scratch_shapes=[pltpu.VMEM((tr, w_sgu), F32)],
        compiler_params=_cparams(("parallel",)),
        name="sgu_merge",
    )(p, p, ob, ws, bst, lg, lb, gn)


def _res_ln_mod_kernel(x_ref, st_ref, y_ref, g_ref, lg_ref, lb_ref, sc_ref, sh_ref, x1_ref, h_ref):
    st = st_ref[...]
    xn = (x_ref[...] - st[:, 0:1]) * st[:, 1:2]
    z = DEEPNORM_ALPHA * xn + g_ref[...] * y_ref[...].astype(F32)
    x1 = _ln_rows(z) * lg_ref[...] + lb_ref[...]
    x1_ref[...] = x1
    h_ref[...] = (x1 * (1.0 + sc_ref[...]) + sh_ref[...]).astype(h_ref.dtype)


def _res_ln_mod(x, stats, y, gate, lg, lb, sc, sh, tr):
    m, d = x.shape
    row = pl.BlockSpec((tr, d), lambda i: (i, 0))
    vec = pl.BlockSpec((1, d), lambda i: (0, 0))
    return pl.pallas_call(
        _res_ln_mod_kernel,
        grid=(m // tr,),
        in_specs=[row, pl.BlockSpec((tr, LANES), lambda i: (i, 0)), row, vec, vec, vec, vec, vec],
        out_specs=[row, row],
        out_shape=[jax.ShapeDtypeStruct((m, d), F32), jax.ShapeDtypeStruct((m, d), BF16)],
        compiler_params=_cparams(("parallel",)),
        name="res_ln_mod",
    )(x, stats, y, gate, lg, lb, sc, sh)


def _tiles(seq, n_ctx):
    return dict(row=256, proj_bm=1024, proj_bn=1024, kv_bm=(seq + n_ctx) // 8, ff_bn=512,
                down_bm=512, down_bk=1024)


def kernel(x, c, ctx, c_ctx, w_ada, b_ada, w_in, w_s, b_s, sgu_g, sgu_b, rpb, gn_g, w_o, ln_g, ln_b,
           w_gate, w_up, w_down):
    batch, seq, d = x.shape
    assert batch == 1 and w_ada.shape[0] == DEPTH
    t = _tiles(seq, ctx.shape[1])
    n_sgu_heads = w_s.shape[1]
    n_na_heads = rpb.shape[1]
    w_sgu = n_sgu_heads * HEAD_DIM
    w_na = n_na_heads * HEAD_DIM
    d_ff = w_gate.shape[2]
    assert w_in.shape[2] == 2 * w_sgu + 3 * w_na and w_sgu == w_na

    x2 = x.reshape(seq, d)
    ctx2 = ctx.reshape(ctx.shape[1], d)

    ccol = jnp.stack([c.reshape(d), c_ctx], axis=1)
    b_ada2 = b_ada[0].reshape(1, 6 * d)
    mod, sb0 = _modulation(ccol, w_ada[0], b_ada2, 2 * d)

    h, ln_stats = _ln_mod(x2, ctx2, mod, t["row"])

    bm, bn = t["proj_bm"], t["proj_bn"]
    guv = _mm_cast(h, [w_in[0]], m_rows=seq, bm=bm, bn=bn, n_cols=2 * w_sgu, epilogue="gelu",
                   name="in_proj_uv")
    q = _mm_cast(h, [w_in[0]], m_rows=seq, bm=bm, bn=bn, n_cols=w_na, col0=2 * w_sgu,
                 epilogue="scale", scale=HEAD_DIM ** -0.5 * LOG2_E, head_major=True, name="in_proj_q")
    kv = _mm_cast(h, [w_in[0]], bm=t["kv_bm"], bn=bn, n_cols=2 * w_na, col0=2 * w_sgu + w_na,
                  head_major=True, name="in_proj_kv")

    tp = _bias_table(rpb[0].reshape(-1), n_na_heads)
    ob, mod_b = _attention(q, kv, tp, sb0, w_ada[0], b_ada2, 2 * d)
    g1, sh2, sc2, g2 = [mod_b[0:1, i * d:(i + 1) * d] for i in range(4)]

    merged = _sgu_merge(guv, ob, w_s[0].astype(BF16), b_s[0].T, sgu_g, sgu_b, gn_g, n_heads=n_sgu_heads)

    y = _mm_cast(merged, [w_o[0]], bm=bm, bn=bn, n_cols=d, name="out_proj")
    x1, h2 = _res_ln_mod(x2, ln_stats, y, g1, ln_g[0, 0:1], ln_b[0, 0:1], sc2, sh2, t["row"])

    g, wd = _mm_cast(h2, [w_gate[0], w_up[0]], bm=bm, bn=t["ff_bn"], n_cols=d_ff, epilogue="swiglu",
                     side=w_down[0], name="swiglu_up")
    out = _down_ln(g, wd, x1, g2, ln_g[0, 1:2], ln_b[0, 1:2], bm=t["down_bm"], bk=t["down_bk"])
    return out.reshape(batch, seq, d)
```

```python
import functools
import math

import jax
import jax.numpy as jnp
from jax import lax
from jax.experimental import pallas as pl
from jax.experimental.pallas import tpu as pltpu

F32 = jnp.float32
BF16 = jnp.bfloat16

GRID_W = 64
HEAD_DIM = 128
CHUNK = 128
WIN_H = 8
WIN_W = 16
LN_EPS = 1e-5
DEPTH = 1
DEEPNORM_ALPHA = (2.0 * DEPTH) ** 0.25
LOG2_E = math.log2(math.e)

LANES = 128
SUBLANES = 8
VMEM_LIMIT = 56 * 1024 * 1024


def _cparams(sem):
    return pltpu.CompilerParams(dimension_semantics=sem, vmem_limit_bytes=VMEM_LIMIT)


def _gelu(x):
    return 0.5 * x * (1.0 + lax.erf(x * (1.0 / math.sqrt(2.0))))


def _silu(x):
    return x * jax.nn.sigmoid(x)


def _ln_rows(x):
    mu = jnp.mean(x, axis=-1, keepdims=True)
    xc = x - mu
    var = jnp.mean(xc * xc, axis=-1, keepdims=True)
    return xc * lax.rsqrt(var + LN_EPS)


def _rms_rows(x):
    return x * lax.rsqrt(jnp.mean(x * x, axis=-1, keepdims=True) + LN_EPS)


def _mod_kernel(ccol_ref, w_ref, b_ref, o_ref, sb0_ref, sb_ref, *, tn):
    d = w_ref.shape[0]

    @pl.when(pl.program_id(0) == 0)
    def _():
        s = _silu(ccol_ref[...])
        sb_ref[0] = jnp.broadcast_to(s[:, 0:1], (d, LANES))
        sb_ref[1] = jnp.broadcast_to(s[:, 1:2], (d, LANES))
        sb0_ref[...] = sb_ref[0]

    nl = tn // LANES

    def body(kc, accs):
        k0 = pl.multiple_of(kc * SUBLANES, SUBLANES)
        s0 = sb_ref[0, pl.ds(k0, SUBLANES), :]
        s1 = sb_ref[1, pl.ds(k0, SUBLANES), :]
        w = w_ref[pl.ds(k0, SUBLANES), :]
        out = []
        for j in range(nl):
            wj = w[:, j * LANES:(j + 1) * LANES]
            out.append(accs[2 * j] + wj * s0)
            out.append(accs[2 * j + 1] + wj * s1)
        return tuple(out)

    zero = jnp.zeros((SUBLANES, LANES), F32)
    accs = lax.fori_loop(0, d // SUBLANES, body, (zero,) * (2 * nl), unroll=8)
    r0 = jnp.concatenate([jnp.sum(accs[2 * j], axis=0, keepdims=True) for j in range(nl)], axis=1)
    r1 = jnp.concatenate([jnp.sum(accs[2 * j + 1], axis=0, keepdims=True) for j in range(nl)], axis=1)
    row = lax.broadcasted_iota(jnp.int32, (SUBLANES, tn), 0)
    b = b_ref[...]
    o_ref[...] = jnp.where(row == 0, r0 + b, jnp.where(row == 1, r1 + b, 0.0))


def _modulation(ccol, w_ada, b_ada, n):
    d = w_ada.shape[0]
    tn = 512
    return pl.pallas_call(
        functools.partial(_mod_kernel, tn=tn),
        grid=(n // tn,),
        in_specs=[
            pl.BlockSpec((d, 2), lambda j: (0, 0)),
            pl.BlockSpec((d, tn), lambda j: (0, j)),
            pl.BlockSpec((1, tn), lambda j: (0, j)),
        ],
        out_specs=[pl.BlockSpec((SUBLANES, tn), lambda j: (0, j)),
                   pl.BlockSpec((d, LANES), lambda j: (0, 0))],
        out_shape=[jax.ShapeDtypeStruct((SUBLANES, n), F32), jax.ShapeDtypeStruct((d, LANES), F32)],
        scratch_shapes=[pltpu.VMEM((2, d, LANES), F32)],
        compiler_params=_cparams(("arbitrary",)),
        name="modulation",
    )(ccol, w_ada, b_ada)


MOD_ACCS = 4


def _deferred_mod_tile(sb_ref, w_ref, b_ref, o_ref):
    d, tn = w_ref.shape
    nl = tn // LANES
    accs = [[None] * MOD_ACCS for _ in range(nl)]
    for kc in range(d // SUBLANES):
        rows = slice(kc * SUBLANES, (kc + 1) * SUBLANES)
        s = sb_ref[rows, :]
        for jj in range(nl):
            p = w_ref[rows, jj * LANES:(jj + 1) * LANES] * s
            a = kc % MOD_ACCS
            accs[jj][a] = p if accs[jj][a] is None else accs[jj][a] + p
    tot = [jnp.sum(functools.reduce(jnp.add, accs[jj]), axis=0, keepdims=True) for jj in range(nl)]
    o_ref[...] = jnp.broadcast_to(jnp.concatenate(tot, axis=1) + b_ref[...], o_ref.shape)


def _ln_mod_kernel(x_hbm, c_ref, mod_ref, o_ref, st_ref, xbuf, sem, *, n_x_blocks):
    d = c_ref.shape[1]
    i = pl.program_id(0)
    slot = _ring_fetch(x_hbm, xbuf, sem, i, n_x_blocks)

    def emit(x, row):
        sh = mod_ref[row:row + 1, 0:d]
        sc = mod_ref[row:row + 1, d:2 * d]
        mu = jnp.mean(x, axis=-1, keepdims=True)
        xc = x - mu
        rstd = lax.rsqrt(jnp.mean(xc * xc, axis=-1, keepdims=True) + LN_EPS)
        o_ref[...] = (xc * rstd * (1.0 + sc) + sh).astype(o_ref.dtype)
        lane = lax.broadcasted_iota(jnp.int32, st_ref.shape, 1)
        st_ref[...] = jnp.where(lane == 0, mu, rstd)

    @pl.when(i < n_x_blocks)
    def _():
        emit(xbuf[slot], 0)

    @pl.when(i >= n_x_blocks)
    def _():
        emit(c_ref[...], 1)


def _ln_mod(x, ctx, mod, tr):
    m, d = x.shape
    n_x_blocks = m // tr
    n_c_blocks = ctx.shape[0] // tr
    n_rows = m + ctx.shape[0]
    return pl.pallas_call(
        functools.partial(_ln_mod_kernel, n_x_blocks=n_x_blocks),
        grid=(n_x_blocks + n_c_blocks,),
        in_specs=[
            pl.BlockSpec(memory_space=pl.ANY),
            pl.BlockSpec((tr, d), lambda i: (jnp.maximum(i - n_x_blocks, 0), 0)),
            pl.BlockSpec((SUBLANES, 2 * d), lambda i: (0, 0)),
        ],
        out_specs=[pl.BlockSpec((tr, d), lambda i: (i, 0)), pl.BlockSpec((tr, LANES), lambda i: (i, 0))],
        out_shape=[jax.ShapeDtypeStruct((n_rows, d), BF16), jax.ShapeDtypeStruct((n_rows, LANES), F32)],
        scratch_shapes=[pltpu.VMEM((RING_SLOTS, tr, d), F32), pltpu.SemaphoreType.DMA((RING_SLOTS,))],
        compiler_params=_cparams(("arbitrary",)),
        name="ln_mod",
    )(x, ctx, mod)


CAST_ROWS = 256
SIDE_ROWS = 64
MM_CHUNK = 256


def _mm_cast_kernel(*refs, n_w, bn, n_blocks, m_blocks, col0, n_cols, epilogue, scale, has_side, head_major):
    it = iter(refs)
    x_ref = next(it)
    w_hbm = [next(it) for _ in range(n_w)]
    side_in = next(it) if has_side else None
    o_ref = next(it)
    side_out = next(it) if has_side else None
    stage = [next(it) for _ in range(n_w)]
    wb = [next(it) for _ in range(n_w)]
    sem = next(it)

    j = pl.program_id(0)
    i = pl.program_id(1)
    k = stage[0].shape[0]
    w_last = n_cols - (n_blocks - 1) * bn

    def slab_copies(jj, width):
        return [pltpu.make_async_copy(w_hbm[a].at[:, pl.ds(col0 + jj * bn, width)],
                                      stage[a].at[:, pl.ds(0, width)], sem.at[a]) for a in range(n_w)]

    def for_slab(jj, fn):
        if w_last == bn:
            for cp in slab_copies(jj, bn):
                fn(cp)
            return

        @pl.when(jj < n_blocks - 1)
        def _():
            for cp in slab_copies(jj, bn):
                fn(cp)

        @pl.when(jj == n_blocks - 1)
        def _():
            for cp in slab_copies(n_blocks - 1, w_last):
                fn(cp)

    @pl.when((j == 0) & (i == 0))
    def _():
        for_slab(j, lambda cp: cp.start())

    @pl.when(i == 0)
    def _():
        for_slab(j, lambda cp: cp.wait())

        def cast_rows(r, carry):
            r0 = pl.multiple_of(r * CAST_ROWS, CAST_ROWS)
            for a in range(n_w):
                wb[a][pl.ds(r0, CAST_ROWS), :] = stage[a][pl.ds(r0, CAST_ROWS), :].astype(BF16)
            return carry

        lax.fori_loop(0, k // CAST_ROWS, cast_rows, 0)

    @pl.when((i == min(1, m_blocks - 1)) & (j + 1 < n_blocks))
    def _():
        for_slab(j + 1, lambda cp: cp.start())

    def compute(n_chunks):
        x = x_ref[...]
        for c in range(n_chunks):
            cols = slice(c * MM_CHUNK, (c + 1) * MM_CHUNK)
            accs = [jnp.dot(x, wb[a][:, cols], preferred_element_type=F32) for a in range(n_w)]
            if epilogue == "plain":
                y = accs[0]
            elif epilogue == "gelu":
                y = _gelu(accs[0])
            elif epilogue == "scale":
                y = accs[0] * scale
            else:
                y = _silu(accs[0]) * accs[1]
            y = y.astype(o_ref.dtype)
            if head_major:
                for hh in range(MM_CHUNK // HEAD_DIM):
                    o_ref[c * (MM_CHUNK // HEAD_DIM) + hh] = y[:, hh * HEAD_DIM:(hh + 1) * HEAD_DIM]
            else:
                o_ref[:, cols] = y

    full_chunks = bn // MM_CHUNK
    last_chunks = -(-w_last // MM_CHUNK)
    if last_chunks == full_chunks:
        compute(full_chunks)
    else:
        @pl.when(j < n_blocks - 1)
        def _():
            compute(full_chunks)

        @pl.when(j == n_blocks - 1)
        def _():
            compute(last_chunks)

    if side_in is not None:
        side_out[...] = side_in[...].astype(side_out.dtype)


def _mm_cast(x, ws, *, bm, bn, n_cols, m_rows=None, col0=0, epilogue="plain", scale=1.0, side=None,
             head_major=False, name):
    m, k = x.shape
    m = m if m_rows is None else m_rows
    n_w = len(ws)
    assert m % bm == 0 and bn % MM_CHUNK == 0 and k % CAST_ROWS == 0
    n_blocks = -(-n_cols // bn)
    m_blocks = m // bm
    in_specs = [pl.BlockSpec((bm, k), lambda j, i: (i, 0))]
    in_specs += [pl.BlockSpec(memory_space=pl.ANY)] * n_w
    if head_major:
        assert n_cols % bn == 0
        out_specs = [pl.BlockSpec((bn // HEAD_DIM, bm, HEAD_DIM), lambda j, i: (j, i, 0))]
        out_shape = [jax.ShapeDtypeStruct((n_cols // HEAD_DIM, m, HEAD_DIM), BF16)]
    else:
        out_specs = [pl.BlockSpec((bm, bn), lambda j, i: (i, j))]
        out_shape = [jax.ShapeDtypeStruct((m, n_cols), BF16)]
    args = [x, *ws]
    if side is not None:
        rows, c = side.shape
        side_blocks = rows // SIDE_ROWS
        assert side_blocks * SIDE_ROWS == rows and side_blocks <= n_blocks * m_blocks
        side_spec = pl.BlockSpec(
            (SIDE_ROWS, c), lambda j, i: (jnp.minimum(j * m_blocks + i, side_blocks - 1), 0))
        in_specs.append(side_spec)
        out_specs.append(side_spec)
        out_shape.append(jax.ShapeDtypeStruct((rows, c), BF16))
        args.append(side)
    kern = functools.partial(
        _mm_cast_kernel, n_w=n_w, bn=bn, n_blocks=n_blocks, m_blocks=m_blocks, col0=col0, n_cols=n_cols,
        epilogue=epilogue, scale=scale, has_side=side is not None, head_major=head_major)
    out = pl.pallas_call(
        kern,
        grid=(n_blocks, m_blocks),
        in_specs=in_specs,
        out_specs=out_specs,
        out_shape=out_shape,
        scratch_shapes=([pltpu.VMEM((k, bn), F32)] * n_w + [pltpu.VMEM((k, bn), BF16)] * n_w
                        + [pltpu.SemaphoreType.DMA((n_w,))]),
        compiler_params=_cparams(("arbitrary", "arbitrary")),
        name=name,
    )(*args)
    return out if side is not None else out[0]


LN_ROWS = 64


def _down_ln_kernel(g_ref, w_ref, x_ref, gate_ref, lg_ref, lb_ref, o_ref, *, k_blocks, k_last):
    kk = pl.program_id(1)

    def part(kw):
        return jnp.dot(g_ref[:, :kw], w_ref[:kw, :], preferred_element_type=F32)

    bk = w_ref.shape[0]

    @pl.when(kk == 0)
    def _():
        o_ref[...] = part(bk)

    @pl.when((kk > 0) & (kk < k_blocks - 1))
    def _():
        o_ref[...] += part(bk)

    @pl.when(kk == k_blocks - 1)
    def _():
        o_ref[...] += part(k_last)
        gate = gate_ref[...]
        lg = lg_ref[...]
        lb = lb_ref[...]

        def ln_chunk(r, carry):
            rows = pl.ds(pl.multiple_of(r * LN_ROWS, LN_ROWS), LN_ROWS)
            z = DEEPNORM_ALPHA * x_ref[rows, :] + gate * o_ref[rows, :]
            o_ref[rows, :] = _ln_rows(z) * lg + lb
            return carry

        lax.fori_loop(0, o_ref.shape[0] // LN_ROWS, ln_chunk, 0)


def _down_ln(g, w, x, gate, lg, lb, *, bm, bk):
    m, k = g.shape
    n = w.shape[1]
    assert m % bm == 0 and w.shape[0] == k
    k_blocks = -(-k // bk)
    k_last = k - (k_blocks - 1) * bk
    assert k_blocks >= 2 and k_last % LANES == 0
    row = pl.BlockSpec((bm, n), lambda i, kk: (i, 0))
    vec = pl.BlockSpec((1, n), lambda i, kk: (0, 0))
    return pl.pallas_call(
        functools.partial(_down_ln_kernel, k_blocks=k_blocks, k_last=k_last),
        grid=(m // bm, k_blocks),
        in_specs=[
            pl.BlockSpec((bm, bk), lambda i, kk: (i, kk)),
            pl.BlockSpec((bk, n), lambda i, kk: (kk, 0)),
            row, vec, vec, vec,
        ],
        out_specs=row,
        out_shape=jax.ShapeDtypeStruct((m, n), F32),
        compiler_params=_cparams(("arbitrary", "arbitrary")),
        name="swiglu_down_ln",
    )(g, w, x, gate, lg, lb)


N_PAIR_ROWS = 2 * WIN_H - 2


def _bias_kernel(rpb_ref, o_ref):
    h = pl.program_id(0)
    n_dcol = 2 * WIN_W - 1
    q = lax.broadcasted_iota(jnp.int32, (GRID_W, 2 * GRID_W), 0)
    l = lax.broadcasted_iota(jnp.int32, (GRID_W, 2 * GRID_W), 1)
    kc = l & (GRID_W - 1)
    hi = l >= GRID_W
    dcol = jnp.clip(kc - q, -(WIN_W - 1), WIN_W - 1) + (WIN_W - 1)
    cs = jnp.clip(q - WIN_W // 2, 0, GRID_W - WIN_W)
    inside = (kc >= cs) & (kc < cs + WIN_W)
    base = h * ((2 * WIN_H - 1) * n_dcol)

    def body(dra, carry):
        acc = jnp.zeros((GRID_W, 2 * GRID_W), F32)
        for d in range(n_dcol):
            lo_v = rpb_ref[base + dra * n_dcol + d]
            hi_v = rpb_ref[base + (dra + 1) * n_dcol + d]
            acc = jnp.where(dcol == d, jnp.where(hi, hi_v, lo_v), acc)
        o_ref[dra] = jnp.where(inside, acc * LOG2_E, -jnp.inf)
        return carry

    lax.fori_loop(0, N_PAIR_ROWS, body, 0)


def _bias_table(rpb_flat, n_heads):
    return pl.pallas_call(
        _bias_kernel,
        grid=(n_heads,),
        in_specs=[pl.BlockSpec(memory_space=pltpu.SMEM)],
        out_specs=pl.BlockSpec((None, N_PAIR_ROWS, GRID_W, 2 * GRID_W), lambda h: (h, 0, 0, 0)),
        out_shape=jax.ShapeDtypeStruct((n_heads, N_PAIR_ROWS, GRID_W, 2 * GRID_W), F32),
        compiler_params=_cparams(("arbitrary",)),
        name="bias_table",
    )(rpb_flat)


ROWS_PER_STEP = 64


def _attn_kernel(q_ref, k_ref, v_ref, kc_ref, vc_ref, tp_ref, sb_ref, wa_ref, ba_ref, o_ref, mod_ref,
                 kt_ref, *, n_rows):
    jb = pl.program_id(1)
    s_keys = k_ref.shape[0]

    @pl.when(jb == 0)
    def _():
        kt_ref[0] = k_ref[...].T
        n_shift = s_keys - LANES
        kt_ref[1, :, :n_shift] = k_ref[GRID_W:GRID_W + n_shift, :].T

    _deferred_mod_tile(sb_ref, wa_ref, ba_ref, mod_ref)

    n_lat = WIN_H * GRID_W
    q = q_ref[...]
    kc = kc_ref[...]
    vc = vc_ref[...]
    dn_t = (((1,), (1,)), ((), ()))
    s_ctx = lax.dot_general(q, kc, dn_t, preferred_element_type=F32)
    starts, s_rows = [], []
    for t in range(ROWS_PER_STEP):
        r = jb * ROWS_PER_STEP + t
        ks = jnp.clip(r - WIN_H // 2, 0, n_rows - WIN_H)
        shift = r - ks
        start = pl.multiple_of(ks * GRID_W, GRID_W)
        starts.append(start)
        lane0 = pl.multiple_of((ks // 2) * LANES, LANES)
        kw_t = kt_ref[ks % 2, :, pl.ds(lane0, n_lat)]
        qt = q[t * GRID_W:(t + 1) * GRID_W]
        s = jnp.dot(qt, kw_t, preferred_element_type=F32)
        bias = jnp.concatenate(
            [tp_ref[2 * p - shift + (WIN_H - 1)] for p in range(WIN_H // 2)], axis=1)
        s_rows.append(s + bias)
    s_lat = jnp.concatenate(s_rows, axis=0)
    tiles = [s_lat[:, c * LANES:(c + 1) * LANES] for c in range(n_lat // LANES)]
    tiles += [s_ctx[:, c * LANES:(c + 1) * LANES] for c in range(s_ctx.shape[1] // LANES)]
    m = jnp.max(functools.reduce(jnp.maximum, tiles), axis=1, keepdims=True)
    e = jnp.exp2((s_lat - m).astype(BF16))
    ec = jnp.exp2((s_ctx - m).astype(BF16))
    ones = jnp.ones((vc.shape[0], HEAD_DIM), BF16)
    vc_aug = jnp.concatenate([vc, ones], axis=1)
    oc = jnp.dot(ec, vc_aug, preferred_element_type=F32)
    e32 = e.astype(F32)
    e_sum = functools.reduce(jnp.add, [e32[:, c * LANES:(c + 1) * LANES] for c in range(n_lat // LANES)])
    den = jnp.sum(e_sum, axis=1, keepdims=True) + oc[:, HEAD_DIM:]
    for t in range(ROWS_PER_STEP):
        rows = slice(t * GRID_W, (t + 1) * GRID_W)
        vw = v_ref[pl.ds(starts[t], n_lat), :]
        o = jnp.dot(e[rows], vw, preferred_element_type=F32) + oc[rows, :HEAD_DIM]
        o_ref[rows, :] = (o / den[rows]).astype(o_ref.dtype)


def _attention(q, kv, tp, sb0, w_ada, b_ada, mod_col0):
    n_heads, s, _ = q.shape
    n_ctx = kv.shape[1] - s
    assert s % n_ctx == 0
    ctx_blk = s // n_ctx
    n_rows = s // GRID_W
    tq = ROWS_PER_STEP * GRID_W
    bph = n_rows // ROWS_PER_STEP
    d, n_mod = w_ada.shape
    n_side = n_mod - mod_col0
    tn = n_side // (n_heads * bph)
    assert tn * n_heads * bph == n_side and tn % LANES == 0 and mod_col0 % tn == 0
    blk0 = mod_col0 // tn
    return pl.pallas_call(
        functools.partial(_attn_kernel, n_rows=n_rows),
        grid=(n_heads, bph),
        in_specs=[
            pl.BlockSpec((None, tq, HEAD_DIM), lambda h, j: (h, j, 0)),
            pl.BlockSpec((None, s, HEAD_DIM), lambda h, j: (h, 0, 0)),
            pl.BlockSpec((None, s, HEAD_DIM), lambda h, j: (n_heads + h, 0, 0)),
            pl.BlockSpec((None, n_ctx, HEAD_DIM), lambda h, j: (h, ctx_blk, 0)),
            pl.BlockSpec((None, n_ctx, HEAD_DIM), lambda h, j: (n_heads + h, ctx_blk, 0)),
            pl.BlockSpec((None, N_PAIR_ROWS, GRID_W, 2 * GRID_W), lambda h, j: (h, 0, 0, 0)),
            pl.BlockSpec((d, LANES), lambda h, j: (0, 0)),
            pl.BlockSpec((d, tn), lambda h, j: (0, blk0 + h * bph + j)),
            pl.BlockSpec((1, tn), lambda h, j: (0, blk0 + h * bph + j)),
        ],
        out_specs=[pl.BlockSpec((None, tq, HEAD_DIM), lambda h, j: (h, j, 0)),
                   pl.BlockSpec((SUBLANES, tn), lambda h, j: (0, h * bph + j))],
        out_shape=[jax.ShapeDtypeStruct((n_heads, s, HEAD_DIM), BF16),
                   jax.ShapeDtypeStruct((SUBLANES, n_side), F32)],
        scratch_shapes=[pltpu.VMEM((2, HEAD_DIM, s), BF16)],
        compiler_params=_cparams(("arbitrary", "arbitrary")),
        name="nbr_attention",
    )(q, kv, kv, kv, kv, tp, sb0, w_ada, b_ada)


SGU_CHUNKS_PER_STEP = 4


def _sgu_merge_kernel(gu_ref, gv_ref, ob_ref, ws_ref, bst_ref, lg_ref, lb_ref, gn_ref, o_ref, oa_ref,
                      *, n_heads, w_sgu):
    v = _ln_rows(gv_ref[...].astype(F32)) * lg_ref[...] + lb_ref[...]
    v = v.astype(BF16)
    bst = bst_ref[...]
    for h in range(n_heads):
        cols = slice(h * HEAD_DIM, (h + 1) * HEAD_DIM)
        vh = jnp.concatenate(
            [v[c * CHUNK:(c + 1) * CHUNK, cols] for c in range(SGU_CHUNKS_PER_STEP)], axis=1)
        mixed = jnp.dot(ws_ref[h], vh, preferred_element_type=F32) + bst[:, h:h + 1]
        for c in range(SGU_CHUNKS_PER_STEP):
            rows = slice(c * CHUNK, (c + 1) * CHUNK)
            oa_ref[rows, cols] = gu_ref[rows, cols].astype(F32) * mixed[:, c * HEAD_DIM:(c + 1) * HEAD_DIM]
    gn = gn_ref[...]
    o_ref[:, :w_sgu] = (_rms_rows(oa_ref[...]) * gn[:, :w_sgu]).astype(o_ref.dtype)
    ob = jnp.concatenate([ob_ref[h] for h in range(ob_ref.shape[0])], axis=1).astype(F32)
    o_ref[:, w_sgu:] = (_rms_rows(ob) * gn[:, w_sgu:]).astype(o_ref.dtype)


def _sgu_merge(p, ob, ws, bst, lg, lb, gn, *, n_heads):
    s = p.shape[0]
    w_sgu = n_heads * HEAD_DIM
    n_na_heads = ob.shape[0]
    w_na = n_na_heads * HEAD_DIM
    tr = SGU_CHUNKS_PER_STEP * CHUNK
    return pl.pallas_call(
        functools.partial(_sgu_merge_kernel, n_heads=n_heads, w_sgu=w_sgu),
        grid=(s // tr,),
        in_specs=[
            pl.BlockSpec((tr, w_sgu), lambda i: (i, 0)),
            pl.BlockSpec((tr, w_sgu), lambda i: (i, 1)),
            pl.BlockSpec((n_na_heads, tr, HEAD_DIM), lambda i: (0, i, 0)),
            pl.BlockSpec((n_heads, CHUNK, CHUNK), lambda i: (0, 0, 0)),
            pl.BlockSpec((CHUNK, n_heads), lambda i: (0, 0)),
            pl.BlockSpec((1, w_sgu), lambda i: (0, 0)),
            pl.BlockSpec((1, w_sgu), lambda i: (0, 0)),
            pl.BlockSpec((1, w_sgu + w_na), lambda i: (0, 0)),
        ],
        out_specs=pl.BlockSpec((tr, w_sgu + w_na), lambda i: (i, 0)),
        out_shape=jax.ShapeDtypeStruct((s, w_sgu + w_na), BF16),
        scratch_shapes=[pltpu.VMEM((tr, w_sgu), F32)],
        compiler_params=_cparams(("parallel",)),
        name="sgu_merge",
    )(p, p, ob, ws, bst, lg, lb, gn)


RING_SLOTS = 3


def _ring_fetch(src_hbm, buf, sem, step, n_blocks):
    rows = buf.shape[1]

    def copy(s):
        start = s * rows if isinstance(s, int) else pl.multiple_of(s * rows, rows)
        slot = s % RING_SLOTS
        return pltpu.make_async_copy(src_hbm.at[pl.ds(start, rows)], buf.at[slot], sem.at[slot])

    @pl.when(step == 0)
    def _():
        for s in range(min(RING_SLOTS - 1, n_blocks)):
            copy(s).start()

    @pl.when(step + (RING_SLOTS - 1) < n_blocks)
    def _():
        copy(step + (RING_SLOTS - 1)).start()

    @pl.when(step < n_blocks)
    def _():
        copy(step).wait()

    return step % RING_SLOTS


def _res_ln_mod_kernel(x_hbm, st_ref, y_ref, g_ref, lg_ref, lb_ref, sc_ref, sh_ref, x1_ref, h_ref, xbuf, sem,
                       *, n_blocks):
    slot = _ring_fetch(x_hbm, xbuf, sem, pl.program_id(0), n_blocks)
    st = st_ref[...]
    xn = (xbuf[slot] - st[:, 0:1]) * st[:, 1:2]
    z = DEEPNORM_ALPHA * xn + g_ref[...] * y_ref[...].astype(F32)
    x1 = _ln_rows(z) * lg_ref[...] + lb_ref[...]
    x1_ref[...] = x1
    h_ref[...] = (x1 * (1.0 + sc_ref[...]) + sh_ref[...]).astype(h_ref.dtype)


def _res_ln_mod(x, stats, y, gate, lg, lb, sc, sh, tr):
    m, d = x.shape
    row = pl.BlockSpec((tr, d), lambda i: (i, 0))
    vec = pl.BlockSpec((1, d), lambda i: (0, 0))
    return pl.pallas_call(
        functools.partial(_res_ln_mod_kernel, n_blocks=m // tr),
        grid=(m // tr,),
        in_specs=[pl.BlockSpec(memory_space=pl.ANY), pl.BlockSpec((tr, LANES), lambda i: (i, 0)), row,
                  vec, vec, vec, vec, vec],
        out_specs=[row, row],
        out_shape=[jax.ShapeDtypeStruct((m, d), F32), jax.ShapeDtypeStruct((m, d), BF16)],
        scratch_shapes=[pltpu.VMEM((RING_SLOTS, tr, d), F32), pltpu.SemaphoreType.DMA((RING_SLOTS,))],
        compiler_params=_cparams(("arbitrary",)),
        name="res_ln_mod",
    )(x, stats, y, gate, lg, lb, sc, sh)


def _tiles(seq, n_ctx):
    return dict(row=256, proj_bm=1024, proj_bn=1024, kv_bm=(seq + n_ctx) // 8, ff_bn=512,
                down_bm=512, down_bk=1024)


def kernel(x, c, ctx, c_ctx, w_ada, b_ada, w_in, w_s, b_s, sgu_g, sgu_b, rpb, gn_g, w_o, ln_g, ln_b,
           w_gate, w_up, w_down):
    batch, seq, d = x.shape
    assert batch == 1 and w_ada.shape[0] == DEPTH
    t = _tiles(seq, ctx.shape[1])
    n_sgu_heads = w_s.shape[1]
    n_na_heads = rpb.shape[1]
    w_sgu = n_sgu_heads * HEAD_DIM
    w_na = n_na_heads * HEAD_DIM
    d_ff = w_gate.shape[2]
    assert w_in.shape[2] == 2 * w_sgu + 3 * w_na and w_sgu == w_na

    x2 = x.reshape(seq, d)
    ctx2 = ctx.reshape(ctx.shape[1], d)

    ccol = jnp.stack([c.reshape(d), c_ctx], axis=1)
    b_ada2 = b_ada[0].reshape(1, 6 * d)
    mod, sb0 = _modulation(ccol, w_ada[0], b_ada2, 2 * d)

    h, ln_stats = _ln_mod(x2, ctx2, mod, t["row"])

    bm, bn = t["proj_bm"], t["proj_bn"]
    guv = _mm_cast(h, [w_in[0]], m_rows=seq, bm=bm, bn=bn, n_cols=2 * w_sgu, epilogue="gelu",
                   name="in_proj_uv")
    q = _mm_cast(h, [w_in[0]], m_rows=seq, bm=bm, bn=bn, n_cols=w_na, col0=2 * w_sgu,
                 epilogue="scale", scale=HEAD_DIM ** -0.5 * LOG2_E, head_major=True, name="in_proj_q")
    kv = _mm_cast(h, [w_in[0]], bm=t["kv_bm"], bn=bn, n_cols=2 * w_na, col0=2 * w_sgu + w_na,
                  head_major=True, name="in_proj_kv")

    tp = _bias_table(rpb[0].reshape(-1), n_na_heads)
    ob, mod_b = _attention(q, kv, tp, sb0, w_ada[0], b_ada2, 2 * d)
    g1, sh2, sc2, g2 = [mod_b[0:1, i * d:(i + 1) * d] for i in range(4)]

    merged = _sgu_merge(guv, ob, w_s[0].astype(BF16), b_s[0].T, sgu_g, sgu_b, gn_g, n_heads=n_sgu_heads)

    y = _mm_cast(merged, [w_o[0]], bm=bm, bn=bn, n_cols=d, name="out_proj")
    x1, h2 = _res_ln_mod(x2, ln_stats, y, g1, ln_g[0, 0:1], ln_b[0, 0:1], sc2, sh2, t["row"])

    g, wd = _mm_cast(h2, [w_gate[0], w_up[0]], bm=bm, bn=t["ff_bn"], n_cols=d_ff, epilogue="swiglu",
                     side=w_down[0], name="swiglu_up")
    out = _down_ln(g, wd, x1, g2, ln_g[0, 1:2], ln_b[0, 1:2], bm=t["down_bm"], bk=t["down_bk"])
    return out.reshape(batch, seq, d)
```

```python
import functools
import math

import jax
import jax.numpy as jnp
from jax import lax
from jax.experimental import pallas as pl
from jax.experimental.pallas import tpu as pltpu

F32 = jnp.float32
BF16 = jnp.bfloat16

GRID_W = 64
HEAD_DIM = 128
CHUNK = 128
WIN_H = 8
WIN_W = 16
LN_EPS = 1e-5
DEPTH = 1
DEEPNORM_ALPHA = (2.0 * DEPTH) ** 0.25
LOG2_E = math.log2(math.e)

LANES = 128
SUBLANES = 8
VMEM_LIMIT = 56 * 1024 * 1024


def _cparams(sem):
    return pltpu.CompilerParams(dimension_semantics=sem, vmem_limit_bytes=VMEM_LIMIT)


def _gelu(x):
    return 0.5 * x * (1.0 + lax.erf(x * (1.0 / math.sqrt(2.0))))


def _silu(x):
    return x * jax.nn.sigmoid(x)


def _ln_rows(x):
    mu = jnp.mean(x, axis=-1, keepdims=True)
    xc = x - mu
    var = jnp.mean(xc * xc, axis=-1, keepdims=True)
    return xc * lax.rsqrt(var + LN_EPS)


def _rms_rows(x):
    return x * lax.rsqrt(jnp.mean(x * x, axis=-1, keepdims=True) + LN_EPS)


def _mod_kernel(ccol_ref, w_hbm, b_ref, o_ref, sb0_ref, sb_ref, wbuf, sem, *, tn, n_blocks):
    d = wbuf.shape[1]
    w_ref = wbuf.at[_ring_fetch(w_hbm, wbuf, sem, pl.program_id(0), n_blocks, axis=1)]

    @pl.when(pl.program_id(0) == 0)
    def _():
        s = _silu(ccol_ref[...])
        sb_ref[0] = jnp.broadcast_to(s[:, 0:1], (d, LANES))
        sb_ref[1] = jnp.broadcast_to(s[:, 1:2], (d, LANES))
        sb0_ref[...] = sb_ref[0]

    nl = tn // LANES

    def body(kc, accs):
        k0 = pl.multiple_of(kc * SUBLANES, SUBLANES)
        s0 = sb_ref[0, pl.ds(k0, SUBLANES), :]
        s1 = sb_ref[1, pl.ds(k0, SUBLANES), :]
        w = w_ref[pl.ds(k0, SUBLANES), :]
        out = []
        for j in range(nl):
            wj = w[:, j * LANES:(j + 1) * LANES]
            out.append(accs[2 * j] + wj * s0)
            out.append(accs[2 * j + 1] + wj * s1)
        return tuple(out)

    zero = jnp.zeros((SUBLANES, LANES), F32)
    accs = lax.fori_loop(0, d // SUBLANES, body, (zero,) * (2 * nl), unroll=8)
    r0 = jnp.concatenate([jnp.sum(accs[2 * j], axis=0, keepdims=True) for j in range(nl)], axis=1)
    r1 = jnp.concatenate([jnp.sum(accs[2 * j + 1], axis=0, keepdims=True) for j in range(nl)], axis=1)
    row = lax.broadcasted_iota(jnp.int32, (SUBLANES, tn), 0)
    b = b_ref[...]
    o_ref[...] = jnp.where(row == 0, r0 + b, jnp.where(row == 1, r1 + b, 0.0))


def _modulation(ccol, w_ada, b_ada, n):
    d = w_ada.shape[0]
    tn = 512
    return pl.pallas_call(
        functools.partial(_mod_kernel, tn=tn, n_blocks=n // tn),
        grid=(n // tn,),
        in_specs=[
            pl.BlockSpec((d, 2), lambda j: (0, 0)),
            pl.BlockSpec(memory_space=pl.ANY),
            pl.BlockSpec((1, tn), lambda j: (0, j)),
        ],
        out_specs=[pl.BlockSpec((SUBLANES, tn), lambda j: (0, j)),
                   pl.BlockSpec((d, LANES), lambda j: (0, 0))],
        out_shape=[jax.ShapeDtypeStruct((SUBLANES, n), F32), jax.ShapeDtypeStruct((d, LANES), F32)],
        scratch_shapes=[pltpu.VMEM((2, d, LANES), F32), pltpu.VMEM((RING_SLOTS, d, tn), F32),
                        pltpu.SemaphoreType.DMA((RING_SLOTS,))],
        compiler_params=_cparams(("arbitrary",)),
        name="modulation",
    )(ccol, w_ada, b_ada)


MOD_ACCS = 4


def _deferred_mod_tile(sb_ref, w_ref, b_ref, o_ref):
    d, tn = w_ref.shape
    nl = tn // LANES
    accs = [[None] * MOD_ACCS for _ in range(nl)]
    for kc in range(d // SUBLANES):
        rows = slice(kc * SUBLANES, (kc + 1) * SUBLANES)
        s = sb_ref[rows, :]
        for jj in range(nl):
            p = w_ref[rows, jj * LANES:(jj + 1) * LANES] * s
            a = kc % MOD_ACCS
            accs[jj][a] = p if accs[jj][a] is None else accs[jj][a] + p
    tot = [jnp.sum(functools.reduce(jnp.add, accs[jj]), axis=0, keepdims=True) for jj in range(nl)]
    o_ref[...] = jnp.broadcast_to(jnp.concatenate(tot, axis=1) + b_ref[...], o_ref.shape)


def _ln_mod_kernel(x_hbm, c_ref, mod_ref, o_ref, st_ref, xbuf, sem, *, n_x_blocks):
    d = c_ref.shape[1]
    i = pl.program_id(0)
    slot = _ring_fetch(x_hbm, xbuf, sem, i, n_x_blocks)

    def emit(x, row):
        sh = mod_ref[row:row + 1, 0:d]
        sc = mod_ref[row:row + 1, d:2 * d]
        mu = jnp.mean(x, axis=-1, keepdims=True)
        xc = x - mu
        rstd = lax.rsqrt(jnp.mean(xc * xc, axis=-1, keepdims=True) + LN_EPS)
        o_ref[...] = (xc * rstd * (1.0 + sc) + sh).astype(o_ref.dtype)
        lane = lax.broadcasted_iota(jnp.int32, st_ref.shape, 1)
        st_ref[...] = jnp.where(lane == 0, mu, rstd)

    @pl.when(i < n_x_blocks)
    def _():
        emit(xbuf[slot], 0)

    @pl.when(i >= n_x_blocks)
    def _():
        emit(c_ref[...], 1)


def _ln_mod(x, ctx, mod, tr):
    m, d = x.shape
    n_x_blocks = m // tr
    n_c_blocks = ctx.shape[0] // tr
    n_rows = m + ctx.shape[0]
    return pl.pallas_call(
        functools.partial(_ln_mod_kernel, n_x_blocks=n_x_blocks),
        grid=(n_x_blocks + n_c_blocks,),
        in_specs=[
            pl.BlockSpec(memory_space=pl.ANY),
            pl.BlockSpec((tr, d), lambda i: (jnp.maximum(i - n_x_blocks, 0), 0)),
            pl.BlockSpec((SUBLANES, 2 * d), lambda i: (0, 0)),
        ],
        out_specs=[pl.BlockSpec((tr, d), lambda i: (i, 0)), pl.BlockSpec((tr, LANES), lambda i: (i, 0))],
        out_shape=[jax.ShapeDtypeStruct((n_rows, d), BF16), jax.ShapeDtypeStruct((n_rows, LANES), F32)],
        scratch_shapes=[pltpu.VMEM((RING_SLOTS, tr, d), F32), pltpu.SemaphoreType.DMA((RING_SLOTS,))],
        compiler_params=_cparams(("arbitrary",)),
        name="ln_mod",
    )(x, ctx, mod)


CAST_ROWS = 256
SIDE_ROWS = 64
MM_CHUNK = 256


def _mm_cast_kernel(*refs, n_w, bn, n_blocks, m_blocks, col0, n_cols, epilogue, scale, has_side, head_major):
    it = iter(refs)
    x_ref = next(it)
    w_hbm = [next(it) for _ in range(n_w)]
    side_in = next(it) if has_side else None
    o_ref = next(it)
    side_out = next(it) if has_side else None
    stage = [next(it) for _ in range(n_w)]
    wb = [next(it) for _ in range(n_w)]
    sem = next(it)

    j = pl.program_id(0)
    i = pl.program_id(1)
    k = stage[0].shape[0]
    w_last = n_cols - (n_blocks - 1) * bn

    def slab_copies(jj, width):
        return [pltpu.make_async_copy(w_hbm[a].at[:, pl.ds(col0 + jj * bn, width)],
                                      stage[a].at[:, pl.ds(0, width)], sem.at[a]) for a in range(n_w)]

    def for_slab(jj, fn):
        if w_last == bn:
            for cp in slab_copies(jj, bn):
                fn(cp)
            return

        @pl.when(jj < n_blocks - 1)
        def _():
            for cp in slab_copies(jj, bn):
                fn(cp)

        @pl.when(jj == n_blocks - 1)
        def _():
            for cp in slab_copies(n_blocks - 1, w_last):
                fn(cp)

    @pl.when((j == 0) & (i == 0))
    def _():
        for_slab(j, lambda cp: cp.start())

    @pl.when(i == 0)
    def _():
        for_slab(j, lambda cp: cp.wait())

        def cast_rows(r, carry):
            r0 = pl.multiple_of(r * CAST_ROWS, CAST_ROWS)
            for a in range(n_w):
                wb[a][pl.ds(r0, CAST_ROWS), :] = stage[a][pl.ds(r0, CAST_ROWS), :].astype(BF16)
            return carry

        lax.fori_loop(0, k // CAST_ROWS, cast_rows, 0)

    @pl.when((i == min(1, m_blocks - 1)) & (j + 1 < n_blocks))
    def _():
        for_slab(j + 1, lambda cp: cp.start())

    def compute(n_chunks):
        x = x_ref[...]
        for c in range(n_chunks):
            cols = slice(c * MM_CHUNK, (c + 1) * MM_CHUNK)
            accs = [jnp.dot(x, wb[a][:, cols], preferred_element_type=F32) for a in range(n_w)]
            if epilogue == "plain":
                y = accs[0]
            elif epilogue == "gelu":
                y = _gelu(accs[0])
            elif epilogue == "scale":
                y = accs[0] * scale
            else:
                y = _silu(accs[0]) * accs[1]
            y = y.astype(o_ref.dtype)
            if head_major:
                for hh in range(MM_CHUNK // HEAD_DIM):
                    o_ref[c * (MM_CHUNK // HEAD_DIM) + hh] = y[:, hh * HEAD_DIM:(hh + 1) * HEAD_DIM]
            else:
                o_ref[:, cols] = y

    full_chunks = bn // MM_CHUNK
    last_chunks = -(-w_last // MM_CHUNK)
    if last_chunks == full_chunks:
        compute(full_chunks)
    else:
        @pl.when(j < n_blocks - 1)
        def _():
            compute(full_chunks)

        @pl.when(j == n_blocks - 1)
        def _():
            compute(last_chunks)

    if side_in is not None:
        side_out[...] = side_in[...].astype(side_out.dtype)


def _mm_cast(x, ws, *, bm, bn, n_cols, m_rows=None, col0=0, epilogue="plain", scale=1.0, side=None,
             head_major=False, name):
    m, k = x.shape
    m = m if m_rows is None else m_rows
    n_w = len(ws)
    assert m % bm == 0 and bn % MM_CHUNK == 0 and k % CAST_ROWS == 0
    n_blocks = -(-n_cols // bn)
    m_blocks = m // bm
    in_specs = [pl.BlockSpec((bm, k), lambda j, i: (i, 0))]
    in_specs += [pl.BlockSpec(memory_space=pl.ANY)] * n_w
    if head_major:
        assert n_cols % bn == 0
        out_specs = [pl.BlockSpec((bn // HEAD_DIM, bm, HEAD_DIM), lambda j, i: (j, i, 0))]
        out_shape = [jax.ShapeDtypeStruct((n_cols // HEAD_DIM, m, HEAD_DIM), BF16)]
    else:
        out_specs = [pl.BlockSpec((bm, bn), lambda j, i: (i, j))]
        out_shape = [jax.ShapeDtypeStruct((m, n_cols), BF16)]
    args = [x, *ws]
    if side is not None:
        rows, c = side.shape
        side_blocks = rows // SIDE_ROWS
        assert side_blocks * SIDE_ROWS == rows and side_blocks <= n_blocks * m_blocks
        side_spec = pl.BlockSpec(
            (SIDE_ROWS, c), lambda j, i: (jnp.minimum(j * m_blocks + i, side_blocks - 1), 0))
        in_specs.append(side_spec)
        out_specs.append(side_spec)
        out_shape.append(jax.ShapeDtypeStruct((rows, c), BF16))
        args.append(side)
    kern = functools.partial(
        _mm_cast_kernel, n_w=n_w, bn=bn, n_blocks=n_blocks, m_blocks=m_blocks, col0=col0, n_cols=n_cols,
        epilogue=epilogue, scale=scale, has_side=side is not None, head_major=head_major)
    out = pl.pallas_call(
        kern,
        grid=(n_blocks, m_blocks),
        in_specs=in_specs,
        out_specs=out_specs,
        out_shape=out_shape,
        scratch_shapes=([pltpu.VMEM((k, bn), F32)] * n_w + [pltpu.VMEM((k, bn), BF16)] * n_w
                        + [pltpu.SemaphoreType.DMA((n_w,))]),
        compiler_params=_cparams(("arbitrary", "arbitrary")),
        name=name,
    )(*args)
    return out if side is not None else out[0]


LN_ROWS = 64


def _down_ln_kernel(g_ref, w_ref, x_ref, gate_ref, lg_ref, lb_ref, o_ref, *, k_blocks, k_last):
    kk = pl.program_id(1)

    def part(kw):
        return jnp.dot(g_ref[:, :kw], w_ref[:kw, :], preferred_element_type=F32)

    bk = w_ref.shape[0]

    @pl.when(kk == 0)
    def _():
        o_ref[...] = part(bk)

    @pl.when((kk > 0) & (kk < k_blocks - 1))
    def _():
        o_ref[...] += part(bk)

    @pl.when(kk == k_blocks - 1)
    def _():
        o_ref[...] += part(k_last)
        gate = gate_ref[...]
        lg = lg_ref[...]
        lb = lb_ref[...]

        def ln_chunk(r, carry):
            rows = pl.ds(pl.multiple_of(r * LN_ROWS, LN_ROWS), LN_ROWS)
            z = DEEPNORM_ALPHA * x_ref[rows, :] + gate * o_ref[rows, :]
            o_ref[rows, :] = _ln_rows(z) * lg + lb
            return carry

        lax.fori_loop(0, o_ref.shape[0] // LN_ROWS, ln_chunk, 0)


def _down_ln(g, w, x, gate, lg, lb, *, bm, bk):
    m, k = g.shape
    n = w.shape[1]
    assert m % bm == 0 and w.shape[0] == k
    k_blocks = -(-k // bk)
    k_last = k - (k_blocks - 1) * bk
    assert k_blocks >= 2 and k_last % LANES == 0
    row = pl.BlockSpec((bm, n), lambda i, kk: (i, 0))
    vec = pl.BlockSpec((1, n), lambda i, kk: (0, 0))
    return pl.pallas_call(
        functools.partial(_down_ln_kernel, k_blocks=k_blocks, k_last=k_last),
        grid=(m // bm, k_blocks),
        in_specs=[
            pl.BlockSpec((bm, bk), lambda i, kk: (i, kk)),
            pl.BlockSpec((bk, n), lambda i, kk: (kk, 0)),
            row, vec, vec, vec,
        ],
        out_specs=row,
        out_shape=jax.ShapeDtypeStruct((m, n), F32),
        compiler_params=_cparams(("arbitrary", "arbitrary")),
        name="swiglu_down_ln",
    )(g, w, x, gate, lg, lb)


N_PAIR_ROWS = 2 * WIN_H - 2


def _bias_kernel(rpb_ref, o_ref):
    h = pl.program_id(0)
    n_dcol = 2 * WIN_W - 1
    q = lax.broadcasted_iota(jnp.int32, (GRID_W, 2 * GRID_W), 0)
    l = lax.broadcasted_iota(jnp.int32, (GRID_W, 2 * GRID_W), 1)
    kc = l & (GRID_W - 1)
    hi = l >= GRID_W
    dcol = jnp.clip(kc - q, -(WIN_W - 1), WIN_W - 1) + (WIN_W - 1)
    cs = jnp.clip(q - WIN_W // 2, 0, GRID_W - WIN_W)
    inside = (kc >= cs) & (kc < cs + WIN_W)
    base = h * ((2 * WIN_H - 1) * n_dcol)

    def body(dra, carry):
        acc = jnp.zeros((GRID_W, 2 * GRID_W), F32)
        for d in range(n_dcol):
            lo_v = rpb_ref[base + dra * n_dcol + d]
            hi_v = rpb_ref[base + (dra + 1) * n_dcol + d]
            acc = jnp.where(dcol == d, jnp.where(hi, hi_v, lo_v), acc)
        o_ref[dra] = jnp.where(inside, acc * LOG2_E, -jnp.inf)
        return carry

    lax.fori_loop(0, N_PAIR_ROWS, body, 0)


def _bias_table(rpb_flat, n_heads):
    return pl.pallas_call(
        _bias_kernel,
        grid=(n_heads,),
        in_specs=[pl.BlockSpec(memory_space=pltpu.SMEM)],
        out_specs=pl.BlockSpec((None, N_PAIR_ROWS, GRID_W, 2 * GRID_W), lambda h: (h, 0, 0, 0)),
        out_shape=jax.ShapeDtypeStruct((n_heads, N_PAIR_ROWS, GRID_W, 2 * GRID_W), F32),
        compiler_params=_cparams(("arbitrary",)),
        name="bias_table",
    )(rpb_flat)


ROWS_PER_STEP = 64


def _attn_kernel(q_ref, k_ref, v_ref, kc_ref, vc_ref, tp_ref, sb_ref, wa_hbm, ba_ref, o_ref, mod_ref,
                 kt_ref, wabuf, wasem, *, n_rows, mod_col0):
    jb = pl.program_id(1)
    s_keys = k_ref.shape[0]
    step = pl.program_id(0) * pl.num_programs(1) + jb
    n_steps = wa_hbm.shape[1] // wabuf.shape[2] - mod_col0 // wabuf.shape[2]
    wa_ref = wabuf.at[_ring_fetch(wa_hbm, wabuf, wasem, step, n_steps, axis=1, offset=mod_col0)]

    @pl.when(jb == 0)
    def _():
        kt_ref[0] = k_ref[...].T
        n_shift = s_keys - LANES
        kt_ref[1, :, :n_shift] = k_ref[GRID_W:GRID_W + n_shift, :].T

    _deferred_mod_tile(sb_ref, wa_ref, ba_ref, mod_ref)

    n_lat = WIN_H * GRID_W
    q = q_ref[...]
    kc = kc_ref[...]
    vc = vc_ref[...]
    dn_t = (((1,), (1,)), ((), ()))
    s_ctx = lax.dot_general(q, kc, dn_t, preferred_element_type=F32)
    starts, s_rows = [], []
    for t in range(ROWS_PER_STEP):
        r = jb * ROWS_PER_STEP + t
        ks = jnp.clip(r - WIN_H // 2, 0, n_rows - WIN_H)
        shift = r - ks
        start = pl.multiple_of(ks * GRID_W, GRID_W)
        starts.append(start)
        lane0 = pl.multiple_of((ks // 2) * LANES, LANES)
        kw_t = kt_ref[ks % 2, :, pl.ds(lane0, n_lat)]
        qt = q[t * GRID_W:(t + 1) * GRID_W]
        s = jnp.dot(qt, kw_t, preferred_element_type=F32)
        bias = jnp.concatenate(
            [tp_ref[2 * p - shift + (WIN_H - 1)] for p in range(WIN_H // 2)], axis=1)
        s_rows.append(s + bias)
    s_lat = jnp.concatenate(s_rows, axis=0)
    tiles = [s_lat[:, c * LANES:(c + 1) * LANES] for c in range(n_lat // LANES)]
    tiles += [s_ctx[:, c * LANES:(c + 1) * LANES] for c in range(s_ctx.shape[1] // LANES)]
    m = jnp.max(functools.reduce(jnp.maximum, tiles), axis=1, keepdims=True)
    e = jnp.exp2((s_lat - m).astype(BF16))
    ec = jnp.exp2((s_ctx - m).astype(BF16))
    ones = jnp.ones((vc.shape[0], HEAD_DIM), BF16)
    vc_aug = jnp.concatenate([vc, ones], axis=1)
    oc = jnp.dot(ec, vc_aug, preferred_element_type=F32)
    e32 = e.astype(F32)
    e_sum = functools.reduce(jnp.add, [e32[:, c * LANES:(c + 1) * LANES] for c in range(n_lat // LANES)])
    den = jnp.sum(e_sum, axis=1, keepdims=True) + oc[:, HEAD_DIM:]
    for t in range(ROWS_PER_STEP):
        rows = slice(t * GRID_W, (t + 1) * GRID_W)
        vw = v_ref[pl.ds(starts[t], n_lat), :]
        o = jnp.dot(e[rows], vw, preferred_element_type=F32) + oc[rows, :HEAD_DIM]
        o_ref[rows, :] = (o / den[rows]).astype(o_ref.dtype)


def _attention(q, kv, tp, sb0, w_ada, b_ada, mod_col0):
    n_heads, s, _ = q.shape
    n_ctx = kv.shape[1] - s
    assert s % n_ctx == 0
    ctx_blk = s // n_ctx
    n_rows = s // GRID_W
    tq = ROWS_PER_STEP * GRID_W
    bph = n_rows // ROWS_PER_STEP
    d, n_mod = w_ada.shape
    n_side = n_mod - mod_col0
    tn = n_side // (n_heads * bph)
    assert tn * n_heads * bph == n_side and tn % LANES == 0 and mod_col0 % tn == 0
    blk0 = mod_col0 // tn
    return pl.pallas_call(
        functools.partial(_attn_kernel, n_rows=n_rows, mod_col0=mod_col0),
        grid=(n_heads, bph),
        in_specs=[
            pl.BlockSpec((None, tq, HEAD_DIM), lambda h, j: (h, j, 0)),
            pl.BlockSpec((None, s, HEAD_DIM), lambda h, j: (h, 0, 0)),
            pl.BlockSpec((None, s, HEAD_DIM), lambda h, j: (n_heads + h, 0, 0)),
            pl.BlockSpec((None, n_ctx, HEAD_DIM), lambda h, j: (h, ctx_blk, 0)),
            pl.BlockSpec((None, n_ctx, HEAD_DIM), lambda h, j: (n_heads + h, ctx_blk, 0)),
            pl.BlockSpec((None, N_PAIR_ROWS, GRID_W, 2 * GRID_W), lambda h, j: (h, 0, 0, 0)),
            pl.BlockSpec((d, LANES), lambda h, j: (0, 0)),
            pl.BlockSpec(memory_space=pl.ANY),
            pl.BlockSpec((1, tn), lambda h, j: (0, blk0 + h * bph + j)),
        ],
        out_specs=[pl.BlockSpec((None, tq, HEAD_DIM), lambda h, j: (h, j, 0)),
                   pl.BlockSpec((SUBLANES, tn), lambda h, j: (0, h * bph + j))],
        out_shape=[jax.ShapeDtypeStruct((n_heads, s, HEAD_DIM), BF16),
                   jax.ShapeDtypeStruct((SUBLANES, n_side), F32)],
        scratch_shapes=[pltpu.VMEM((2, HEAD_DIM, s), BF16), pltpu.VMEM((RING_SLOTS, d, tn), F32),
                        pltpu.SemaphoreType.DMA((RING_SLOTS,))],
        compiler_params=_cparams(("arbitrary", "arbitrary")),
        name="nbr_attention",
    )(q, kv, kv, kv, kv, tp, sb0, w_ada, b_ada)


SGU_CHUNKS_PER_STEP = 4


def _sgu_merge_kernel(gu_ref, gv_ref, ob_ref, ws_ref, bst_ref, lg_ref, lb_ref, gn_ref, o_ref, oa_ref,
                      *, n_heads, w_sgu):
    v = _ln_rows(gv_ref[...].astype(F32)) * lg_ref[...] + lb_ref[...]
    v = v.astype(BF16)
    bst = bst_ref[...]
    for h in range(n_heads):
        cols = slice(h * HEAD_DIM, (h + 1) * HEAD_DIM)
        vh = jnp.concatenate(
            [v[c * CHUNK:(c + 1) * CHUNK, cols] for c in range(SGU_CHUNKS_PER_STEP)], axis=1)
        mixed = jnp.dot(ws_ref[h], vh, preferred_element_type=F32) + bst[:, h:h + 1]
        for c in range(SGU_CHUNKS_PER_STEP):
            rows = slice(c * CHUNK, (c + 1) * CHUNK)
            oa_ref[rows, cols] = gu_ref[rows, cols].astype(F32) * mixed[:, c * HEAD_DIM:(c + 1) * HEAD_DIM]
    gn = gn_ref[...]
    o_ref[:, :w_sgu] = (_rms_rows(oa_ref[...]) * gn[:, :w_sgu]).astype(o_ref.dtype)
    ob = jnp.concatenate([ob_ref[h] for h in range(ob_ref.shape[0])], axis=1).astype(F32)
    o_ref[:, w_sgu:] = (_rms_rows(ob) * gn[:, w_sgu:]).astype(o_ref.dtype)


def _sgu_merge(p, ob, ws, bst, lg, lb, gn, *, n_heads):
    s = p.shape[0]
    w_sgu = n_heads * HEAD_DIM
    n_na_heads = ob.shape[0]
    w_na = n_na_heads * HEAD_DIM
    tr = SGU_CHUNKS_PER_STEP * CHUNK
    return pl.pallas_call(
        functools.partial(_sgu_merge_kernel, n_heads=n_heads, w_sgu=w_sgu),
        grid=(s // tr,),
        in_specs=[
            pl.BlockSpec((tr, w_sgu), lambda i: (i, 0)),
            pl.BlockSpec((tr, w_sgu), lambda i: (i, 1)),
            pl.BlockSpec((n_na_heads, tr, HEAD_DIM), lambda i: (0, i, 0)),
            pl.BlockSpec((n_heads, CHUNK, CHUNK), lambda i: (0, 0, 0)),
            pl.BlockSpec((CHUNK, n_heads), lambda i: (0, 0)),
            pl.BlockSpec((1, w_sgu), lambda i: (0, 0)),
            pl.BlockSpec((1, w_sgu), lambda i: (0, 0)),
            pl.BlockSpec((1, w_sgu + w_na), lambda i: (0, 0)),
        ],
        out_specs=pl.BlockSpec((tr, w_sgu + w_na), lambda i: (i, 0)),
        out_shape=jax.ShapeDtypeStruct((s, w_sgu + w_na), BF16),
        scratch_shapes=[pltpu.VMEM((tr, w_sgu), F32)],
        compiler_params=_cparams(("parallel",)),
        name="sgu_merge",
    )(p, p, ob, ws, bst, lg, lb, gn)


RING_SLOTS = 3


def _ring_fetch(src_hbm, buf, sem, step, n_blocks, axis=0, offset=0):
    size = buf.shape[1 + axis]

    def copy(s):
        start = offset + s * size
        if not isinstance(s, int):
            start = pl.multiple_of(start, math.gcd(size, offset) if offset else size)
        slot = s % RING_SLOTS
        src = src_hbm.at[pl.ds(start, size)] if axis == 0 else src_hbm.at[:, pl.ds(start, size)]
        return pltpu.make_async_copy(src, buf.at[slot], sem.at[slot])

    @pl.when(step == 0)
    def _():
        for s in range(min(RING_SLOTS - 1, n_blocks)):
            copy(s).start()

    @pl.when(step + (RING_SLOTS - 1) < n_blocks)
    def _():
        copy(step + (RING_SLOTS - 1)).start()

    @pl.when(step < n_blocks)
    def _():
        copy(step).wait()

    return step % RING_SLOTS


def _res_ln_mod_kernel(x_hbm, st_ref, y_ref, g_ref, lg_ref, lb_ref, sc_ref, sh_ref, x1_ref, h_ref, xbuf, sem,
                       *, n_blocks):
    slot = _ring_fetch(x_hbm, xbuf, sem, pl.program_id(0), n_blocks)
    st = st_ref[...]
    xn = (xbuf[slot] - st[:, 0:1]) * st[:, 1:2]
    z = DEEPNORM_ALPHA * xn + g_ref[...] * y_ref[...].astype(F32)
    x1 = _ln_rows(z) * lg_ref[...] + lb_ref[...]
    x1_ref[...] = x1
    h_ref[...] = (x1 * (1.0 + sc_ref[...]) + sh_ref[...]).astype(h_ref.dtype)


def _res_ln_mod(x, stats, y, gate, lg, lb, sc, sh, tr):
    m, d = x.shape
    row = pl.BlockSpec((tr, d), lambda i: (i, 0))
    vec = pl.BlockSpec((1, d), lambda i: (0, 0))
    return pl.pallas_call(
        functools.partial(_res_ln_mod_kernel, n_blocks=m // tr),
        grid=(m // tr,),
        in_specs=[pl.BlockSpec(memory_space=pl.ANY), pl.BlockSpec((tr, LANES), lambda i: (i, 0)), row,
                  vec, vec, vec, vec, vec],
        out_specs=[row, row],
        out_shape=[jax.ShapeDtypeStruct((m, d), F32), jax.ShapeDtypeStruct((m, d), BF16)],
        scratch_shapes=[pltpu.VMEM((RING_SLOTS, tr, d), F32), pltpu.SemaphoreType.DMA((RING_SLOTS,))],
        compiler_params=_cparams(("arbitrary",)),
        name="res_ln_mod",
    )(x, stats, y, gate, lg, lb, sc, sh)


def _tiles(seq, n_ctx):
    return dict(row=256, proj_bm=1024, proj_bn=1024, kv_bm=(seq + n_ctx) // 8, ff_bn=512,
                down_bm=512, down_bk=1024)


def kernel(x, c, ctx, c_ctx, w_ada, b_ada, w_in, w_s, b_s, sgu_g, sgu_b, rpb, gn_g, w_o, ln_g, ln_b,
           w_gate, w_up, w_down):
    batch, seq, d = x.shape
    assert batch == 1 and w_ada.shape[0] == DEPTH
    t = _tiles(seq, ctx.shape[1])
    n_sgu_heads = w_s.shape[1]
    n_na_heads = rpb.shape[1]
    w_sgu = n_sgu_heads * HEAD_DIM
    w_na = n_na_heads * HEAD_DIM
    d_ff = w_gate.shape[2]
    assert w_in.shape[2] == 2 * w_sgu + 3 * w_na and w_sgu == w_na

    x2 = x.reshape(seq, d)
    ctx2 = ctx.reshape(ctx.shape[1], d)

    ccol = jnp.stack([c.reshape(d), c_ctx], axis=1)
    b_ada2 = b_ada[0].reshape(1, 6 * d)
    mod, sb0 = _modulation(ccol, w_ada[0], b_ada2, 2 * d)

    h, ln_stats = _ln_mod(x2, ctx2, mod, t["row"])

    bm, bn = t["proj_bm"], t["proj_bn"]
    guv = _mm_cast(h, [w_in[0]], m_rows=seq, bm=bm, bn=bn, n_cols=2 * w_sgu, epilogue="gelu",
                   name="in_proj_uv")
    q = _mm_cast(h, [w_in[0]], m_rows=seq, bm=bm, bn=bn, n_cols=w_na, col0=2 * w_sgu,
                 epilogue="scale", scale=HEAD_DIM ** -0.5 * LOG2_E, head_major=True, name="in_proj_q")
    kv = _mm_cast(h, [w_in[0]], bm=t["kv_bm"], bn=bn, n_cols=2 * w_na, col0=2 * w_sgu + w_na,
                  head_major=True, name="in_proj_kv")

    tp = _bias_table(rpb[0].reshape(-1), n_na_heads)
    ob, mod_b = _attention(q, kv, tp, sb0, w_ada[0], b_ada2, 2 * d)
    g1, sh2, sc2, g2 = [mod_b[0:1, i * d:(i + 1) * d] for i in range(4)]

    merged = _sgu_merge(guv, ob, w_s[0].astype(BF16), b_s[0].T, sgu_g, sgu_b, gn_g, n_heads=n_sgu_heads)

    y = _mm_cast(merged, [w_o[0]], bm=bm, bn=bn, n_cols=d, name="out_proj")
    x1, h2 = _res_ln_mod(x2, ln_stats, y, g1, ln_g[0, 0:1], ln_b[0, 0:1], sc2, sh2, t["row"])

    g, wd = _mm_cast(h2, [w_gate[0], w_up[0]], bm=bm, bn=t["ff_bn"], n_cols=d_ff, epilogue="swiglu",
                     side=w_down[0], name="swiglu_up")
    out = _down_ln(g, wd, x1, g2, ln_g[0, 1:2], ln_b[0, 1:2], bm=t["down_bm"], bk=t["down_bk"])
    return out.reshape(batch, seq, d)
```

```python
import functools
import math

import jax
import jax.numpy as jnp
from jax import lax
from jax.experimental import pallas as pl
from jax.experimental.pallas import tpu as pltpu

F32 = jnp.float32
BF16 = jnp.bfloat16

GRID_W = 64
HEAD_DIM = 128
CHUNK = 128
WIN_H = 8
WIN_W = 16
LN_EPS = 1e-5
DEPTH = 1
DEEPNORM_ALPHA = (2.0 * DEPTH) ** 0.25
LOG2_E = math.log2(math.e)

LANES = 128
SUBLANES = 8
VMEM_LIMIT = 58 * 1024 * 1024


def _cparams(sem):
    return pltpu.CompilerParams(dimension_semantics=sem, vmem_limit_bytes=VMEM_LIMIT)


def _gelu(x):
    return 0.5 * x * (1.0 + lax.erf(x * (1.0 / math.sqrt(2.0))))


def _silu(x):
    return x * jax.nn.sigmoid(x)


def _ln_rows(x):
    mu = jnp.mean(x, axis=-1, keepdims=True)
    xc = x - mu
    var = jnp.mean(xc * xc, axis=-1, keepdims=True)
    return xc * lax.rsqrt(var + LN_EPS)


def _rms_rows(x):
    return x * lax.rsqrt(jnp.mean(x * x, axis=-1, keepdims=True) + LN_EPS)


def _mod_kernel(ccol_ref, w_ref, b_ref, o_ref, sb0_ref, sb_ref, *, tn):
    d = w_ref.shape[0]

    @pl.when(pl.program_id(0) == 0)
    def _():
        s = _silu(ccol_ref[...])
        sb_ref[0] = jnp.broadcast_to(s[:, 0:1], (d, LANES))
        sb_ref[1] = jnp.broadcast_to(s[:, 1:2], (d, LANES))
        sb0_ref[...] = sb_ref[0]

    nl = tn // LANES

    def body(kc, accs):
        k0 = pl.multiple_of(kc * SUBLANES, SUBLANES)
        s0 = sb_ref[0, pl.ds(k0, SUBLANES), :]
        s1 = sb_ref[1, pl.ds(k0, SUBLANES), :]
        w = w_ref[pl.ds(k0, SUBLANES), :]
        out = []
        for j in range(nl):
            wj = w[:, j * LANES:(j + 1) * LANES]
            out.append(accs[2 * j] + wj * s0)
            out.append(accs[2 * j + 1] + wj * s1)
        return tuple(out)

    zero = jnp.zeros((SUBLANES, LANES), F32)
    accs = lax.fori_loop(0, d // SUBLANES, body, (zero,) * (2 * nl), unroll=8)
    r0 = jnp.concatenate([jnp.sum(accs[2 * j], axis=0, keepdims=True) for j in range(nl)], axis=1)
    r1 = jnp.concatenate([jnp.sum(accs[2 * j + 1], axis=0, keepdims=True) for j in range(nl)], axis=1)
    row = lax.broadcasted_iota(jnp.int32, (SUBLANES, tn), 0)
    b = b_ref[...]
    o_ref[...] = jnp.where(row == 0, r0 + b, jnp.where(row == 1, r1 + b, 0.0))


def _modulation(ccol, w_ada, b_ada, n):
    d = w_ada.shape[0]
    tn = 512
    return pl.pallas_call(
        functools.partial(_mod_kernel, tn=tn),
        grid=(n // tn,),
        in_specs=[
            pl.BlockSpec((d, 2), lambda j: (0, 0)),
            pl.BlockSpec((d, tn), lambda j: (0, j)),
            pl.BlockSpec((1, tn), lambda j: (0, j)),
        ],
        out_specs=[pl.BlockSpec((SUBLANES, tn), lambda j: (0, j)),
                   pl.BlockSpec((d, LANES), lambda j: (0, 0))],
        out_shape=[jax.ShapeDtypeStruct((SUBLANES, n), F32), jax.ShapeDtypeStruct((d, LANES), F32)],
        scratch_shapes=[pltpu.VMEM((2, d, LANES), F32)],
        compiler_params=_cparams(("arbitrary",)),
        name="modulation",
    )(ccol, w_ada, b_ada)


MOD_ACCS = 4


def _deferred_mod_tile(sb_ref, w_ref, b_ref, o_ref):
    d, tn = w_ref.shape
    nl = tn // LANES
    accs = [[None] * MOD_ACCS for _ in range(nl)]
    for kc in range(d // SUBLANES):
        rows = slice(kc * SUBLANES, (kc + 1) * SUBLANES)
        s = sb_ref[rows, :]
        for jj in range(nl):
            p = w_ref[rows, jj * LANES:(jj + 1) * LANES] * s
            a = kc % MOD_ACCS
            accs[jj][a] = p if accs[jj][a] is None else accs[jj][a] + p
    tot = [jnp.sum(functools.reduce(jnp.add, accs[jj]), axis=0, keepdims=True) for jj in range(nl)]
    o_ref[...] = jnp.broadcast_to(jnp.concatenate(tot, axis=1) + b_ref[...], o_ref.shape)


def _ln_mod_kernel(x_hbm, c_ref, mod_ref, o_ref, st_ref, xbuf, sem, *, n_x_blocks):
    d = c_ref.shape[1]
    i = pl.program_id(0)
    slot = _ring_fetch(x_hbm, xbuf, sem, i, n_x_blocks)

    def emit(x, row):
        sh = mod_ref[row:row + 1, 0:d]
        sc = mod_ref[row:row + 1, d:2 * d]
        mu = jnp.mean(x, axis=-1, keepdims=True)
        xc = x - mu
        rstd = lax.rsqrt(jnp.mean(xc * xc, axis=-1, keepdims=True) + LN_EPS)
        o_ref[...] = (xc * rstd * (1.0 + sc) + sh).astype(o_ref.dtype)
        lane = lax.broadcasted_iota(jnp.int32, st_ref.shape, 1)
        st_ref[...] = jnp.where(lane == 0, mu, rstd)

    @pl.when(i < n_x_blocks)
    def _():
        emit(xbuf[slot], 0)

    @pl.when(i >= n_x_blocks)
    def _():
        emit(c_ref[...], 1)


def _ln_mod(x, ctx, mod, tr):
    m, d = x.shape
    n_x_blocks = m // tr
    n_c_blocks = ctx.shape[0] // tr
    n_rows = m + ctx.shape[0]
    return pl.pallas_call(
        functools.partial(_ln_mod_kernel, n_x_blocks=n_x_blocks),
        grid=(n_x_blocks + n_c_blocks,),
        in_specs=[
            pl.BlockSpec(memory_space=pl.ANY),
            pl.BlockSpec((tr, d), lambda i: (jnp.maximum(i - n_x_blocks, 0), 0)),
            pl.BlockSpec((SUBLANES, 2 * d), lambda i: (0, 0)),
        ],
        out_specs=[pl.BlockSpec((tr, d), lambda i: (i, 0)), pl.BlockSpec((tr, LANES), lambda i: (i, 0))],
        out_shape=[jax.ShapeDtypeStruct((n_rows, d), BF16), jax.ShapeDtypeStruct((n_rows, LANES), F32)],
        scratch_shapes=[pltpu.VMEM((RING_SLOTS, tr, d), F32), pltpu.SemaphoreType.DMA((RING_SLOTS,))],
        compiler_params=_cparams(("arbitrary",)),
        name="ln_mod",
    )(x, ctx, mod)


CAST_ROWS = 256
SIDE_ROWS = 64
MM_CHUNK = 256


def _mm_cast_kernel(*refs, n_w, bn, n_blocks, m_blocks, col0, n_cols, epilogue, scale, has_side, head_major):
    it = iter(refs)
    x_ref = next(it)
    w_hbm = [next(it) for _ in range(n_w)]
    side_in = next(it) if has_side else None
    o_ref = next(it)
    side_out = next(it) if has_side else None
    stage = [next(it) for _ in range(n_w)]
    wb = [next(it) for _ in range(n_w)]
    sem = next(it)

    j = pl.program_id(0)
    i = pl.program_id(1)
    k = stage[0].shape[0]
    w_last = n_cols - (n_blocks - 1) * bn

    def slab_copies(jj, width):
        return [pltpu.make_async_copy(w_hbm[a].at[:, pl.ds(col0 + jj * bn, width)],
                                      stage[a].at[:, pl.ds(0, width)], sem.at[a]) for a in range(n_w)]

    def for_slab(jj, fn):
        if w_last == bn:
            for cp in slab_copies(jj, bn):
                fn(cp)
            return

        @pl.when(jj < n_blocks - 1)
        def _():
            for cp in slab_copies(jj, bn):
                fn(cp)

        @pl.when(jj == n_blocks - 1)
        def _():
            for cp in slab_copies(n_blocks - 1, w_last):
                fn(cp)

    @pl.when((j == 0) & (i == 0))
    def _():
        for_slab(j, lambda cp: cp.start())

    @pl.when(i == 0)
    def _():
        for_slab(j, lambda cp: cp.wait())

        def cast_rows(r, carry):
            r0 = pl.multiple_of(r * CAST_ROWS, CAST_ROWS)
            for a in range(n_w):
                wb[a][pl.ds(r0, CAST_ROWS), :] = stage[a][pl.ds(r0, CAST_ROWS), :].astype(BF16)
            return carry

        lax.fori_loop(0, k // CAST_ROWS, cast_rows, 0)

    @pl.when((i == min(1, m_blocks - 1)) & (j + 1 < n_blocks))
    def _():
        for_slab(j + 1, lambda cp: cp.start())

    def compute(n_chunks):
        x = x_ref[...]
        for c in range(n_chunks):
            cols = slice(c * MM_CHUNK, (c + 1) * MM_CHUNK)
            accs = [jnp.dot(x, wb[a][:, cols], preferred_element_type=F32) for a in range(n_w)]
            if epilogue == "plain":
                y = accs[0]
            elif epilogue == "gelu":
                y = _gelu(accs[0])
            elif epilogue == "scale":
                y = accs[0] * scale
            else:
                y = _silu(accs[0]) * accs[1]
            y = y.astype(o_ref.dtype)
            if head_major:
                for hh in range(MM_CHUNK // HEAD_DIM):
                    o_ref[c * (MM_CHUNK // HEAD_DIM) + hh] = y[:, hh * HEAD_DIM:(hh + 1) * HEAD_DIM]
            else:
                o_ref[:, cols] = y

    full_chunks = bn // MM_CHUNK
    last_chunks = -(-w_last // MM_CHUNK)
    if last_chunks == full_chunks:
        compute(full_chunks)
    else:
        @pl.when(j < n_blocks - 1)
        def _():
            compute(full_chunks)

        @pl.when(j == n_blocks - 1)
        def _():
            compute(last_chunks)

    if side_in is not None:
        side_out[...] = side_in[...].astype(side_out.dtype)


def _mm_cast(x, ws, *, bm, bn, n_cols, m_rows=None, col0=0, epilogue="plain", scale=1.0, side=None,
             head_major=False, name):
    m, k = x.shape
    m = m if m_rows is None else m_rows
    n_w = len(ws)
    assert m % bm == 0 and bn % MM_CHUNK == 0 and k % CAST_ROWS == 0
    n_blocks = -(-n_cols // bn)
    m_blocks = m // bm
    in_specs = [pl.BlockSpec((bm, k), lambda j, i: (i, 0))]
    in_specs += [pl.BlockSpec(memory_space=pl.ANY)] * n_w
    if head_major:
        assert n_cols % bn == 0
        out_specs = [pl.BlockSpec((bn // HEAD_DIM, bm, HEAD_DIM), lambda j, i: (j, i, 0))]
        out_shape = [jax.ShapeDtypeStruct((n_cols // HEAD_DIM, m, HEAD_DIM), BF16)]
    else:
        out_specs = [pl.BlockSpec((bm, bn), lambda j, i: (i, j))]
        out_shape = [jax.ShapeDtypeStruct((m, n_cols), BF16)]
    args = [x, *ws]
    if side is not None:
        rows, c = side.shape
        side_blocks = rows // SIDE_ROWS
        assert side_blocks * SIDE_ROWS == rows and side_blocks <= n_blocks * m_blocks
        side_spec = pl.BlockSpec(
            (SIDE_ROWS, c), lambda j, i: (jnp.minimum(j * m_blocks + i, side_blocks - 1), 0))
        in_specs.append(side_spec)
        out_specs.append(side_spec)
        out_shape.append(jax.ShapeDtypeStruct((rows, c), BF16))
        args.append(side)
    kern = functools.partial(
        _mm_cast_kernel, n_w=n_w, bn=bn, n_blocks=n_blocks, m_blocks=m_blocks, col0=col0, n_cols=n_cols,
        epilogue=epilogue, scale=scale, has_side=side is not None, head_major=head_major)
    out = pl.pallas_call(
        kern,
        grid=(n_blocks, m_blocks),
        in_specs=in_specs,
        out_specs=out_specs,
        out_shape=out_shape,
        scratch_shapes=([pltpu.VMEM((k, bn), F32)] * n_w + [pltpu.VMEM((k, bn), BF16)] * n_w
                        + [pltpu.SemaphoreType.DMA((n_w,))]),
        compiler_params=_cparams(("arbitrary", "arbitrary")),
        name=name,
    )(*args)
    return out if side is not None else out[0]


LN_ROWS = 64


def _down_ln_kernel(g_ref, w_ref, x_ref, gate_ref, lg_ref, lb_ref, o_ref, *, k_blocks, k_last):
    kk = pl.program_id(1)

    def part(kw):
        return jnp.dot(g_ref[:, :kw], w_ref[:kw, :], preferred_element_type=F32)

    bk = w_ref.shape[0]

    @pl.when(kk == 0)
    def _():
        o_ref[...] = part(bk)

    @pl.when((kk > 0) & (kk < k_blocks - 1))
    def _():
        o_ref[...] += part(bk)

    @pl.when(kk == k_blocks - 1)
    def _():
        o_ref[...] += part(k_last)
        gate = gate_ref[...]
        lg = lg_ref[...]
        lb = lb_ref[...]

        def ln_chunk(r, carry):
            rows = pl.ds(pl.multiple_of(r * LN_ROWS, LN_ROWS), LN_ROWS)
            z = DEEPNORM_ALPHA * x_ref[rows, :] + gate * o_ref[rows, :]
            o_ref[rows, :] = _ln_rows(z) * lg + lb
            return carry

        lax.fori_loop(0, o_ref.shape[0] // LN_ROWS, ln_chunk, 0)


def _down_ln(g, w, x, gate, lg, lb, *, bm, bk):
    m, k = g.shape
    n = w.shape[1]
    assert m % bm == 0 and w.shape[0] == k
    k_blocks = -(-k // bk)
    k_last = k - (k_blocks - 1) * bk
    assert k_blocks >= 2 and k_last % LANES == 0
    row = pl.BlockSpec((bm, n), lambda i, kk: (i, 0))
    vec = pl.BlockSpec((1, n), lambda i, kk: (0, 0))
    return pl.pallas_call(
        functools.partial(_down_ln_kernel, k_blocks=k_blocks, k_last=k_last),
        grid=(m // bm, k_blocks),
        in_specs=[
            pl.BlockSpec((bm, bk), lambda i, kk: (i, kk)),
            pl.BlockSpec((bk, n), lambda i, kk: (kk, 0)),
            row, vec, vec, vec,
        ],
        out_specs=row,
        out_shape=jax.ShapeDtypeStruct((m, n), F32),
        compiler_params=_cparams(("arbitrary", "arbitrary")),
        name="swiglu_down_ln",
    )(g, w, x, gate, lg, lb)


N_PAIR_ROWS = 2 * WIN_H - 2


def _bias_kernel(rpb_ref, o_ref):
    h = pl.program_id(0)
    n_dcol = 2 * WIN_W - 1
    q = lax.broadcasted_iota(jnp.int32, (GRID_W, 2 * GRID_W), 0)
    l = lax.broadcasted_iota(jnp.int32, (GRID_W, 2 * GRID_W), 1)
    kc = l & (GRID_W - 1)
    hi = l >= GRID_W
    dcol = jnp.clip(kc - q, -(WIN_W - 1), WIN_W - 1) + (WIN_W - 1)
    cs = jnp.clip(q - WIN_W // 2, 0, GRID_W - WIN_W)
    inside = (kc >= cs) & (kc < cs + WIN_W)
    base = h * ((2 * WIN_H - 1) * n_dcol)

    def body(dra, carry):
        acc = jnp.zeros((GRID_W, 2 * GRID_W), F32)
        for d in range(n_dcol):
            lo_v = rpb_ref[base + dra * n_dcol + d]
            hi_v = rpb_ref[base + (dra + 1) * n_dcol + d]
            acc = jnp.where(dcol == d, jnp.where(hi, hi_v, lo_v), acc)
        o_ref[dra] = jnp.where(inside, acc * LOG2_E, -jnp.inf)
        return carry

    lax.fori_loop(0, N_PAIR_ROWS, body, 0)


def _bias_table(rpb_flat, n_heads):
    return pl.pallas_call(
        _bias_kernel,
        grid=(n_heads,),
        in_specs=[pl.BlockSpec(memory_space=pltpu.SMEM)],
        out_specs=pl.BlockSpec((None, N_PAIR_ROWS, GRID_W, 2 * GRID_W), lambda h: (h, 0, 0, 0)),
        out_shape=jax.ShapeDtypeStruct((n_heads, N_PAIR_ROWS, GRID_W, 2 * GRID_W), F32),
        compiler_params=_cparams(("arbitrary",)),
        name="bias_table",
    )(rpb_flat)


ROWS_PER_STEP = 64


def _attn_kernel(q_ref, k_ref, v_ref, kc_ref, vc_ref, tp_ref, sb_ref, wa_ref, ba_ref, o_ref, mod_ref,
                 kt_ref, *, n_rows):
    jb = pl.program_id(1)
    s_keys = k_ref.shape[0]

    @pl.when(jb == 0)
    def _():
        kt_ref[0] = k_ref[...].T
        n_shift = s_keys - LANES
        kt_ref[1, :, :n_shift] = k_ref[GRID_W:GRID_W + n_shift, :].T

    _deferred_mod_tile(sb_ref, wa_ref, ba_ref, mod_ref)

    n_lat = WIN_H * GRID_W
    q = q_ref[...]
    kc = kc_ref[...]
    vc = vc_ref[...]
    dn_t = (((1,), (1,)), ((), ()))
    s_ctx = lax.dot_general(q, kc, dn_t, preferred_element_type=F32)
    starts, s_rows = [], []
    for t in range(ROWS_PER_STEP):
        r = jb * ROWS_PER_STEP + t
        ks = jnp.clip(r - WIN_H // 2, 0, n_rows - WIN_H)
        shift = r - ks
        start = pl.multiple_of(ks * GRID_W, GRID_W)
        starts.append(start)
        lane0 = pl.multiple_of((ks // 2) * LANES, LANES)
        kw_t = kt_ref[ks % 2, :, pl.ds(lane0, n_lat)]
        qt = q[t * GRID_W:(t + 1) * GRID_W]
        s = jnp.dot(qt, kw_t, preferred_element_type=F32)
        bias = jnp.concatenate(
            [tp_ref[2 * p - shift + (WIN_H - 1)] for p in range(WIN_H // 2)], axis=1)
        s_rows.append(s + bias)
    s_lat = jnp.concatenate(s_rows, axis=0)
    tiles = [s_lat[:, c * LANES:(c + 1) * LANES] for c in range(n_lat // LANES)]
    tiles += [s_ctx[:, c * LANES:(c + 1) * LANES] for c in range(s_ctx.shape[1] // LANES)]
    m = jnp.max(functools.reduce(jnp.maximum, tiles), axis=1, keepdims=True)
    e = jnp.exp2((s_lat - m).astype(BF16))
    ec = jnp.exp2((s_ctx - m).astype(BF16))
    ones = jnp.ones((vc.shape[0], HEAD_DIM), BF16)
    vc_aug = jnp.concatenate([vc, ones], axis=1)
    oc = jnp.dot(ec, vc_aug, preferred_element_type=F32)
    e32 = e.astype(F32)
    e_sum = functools.reduce(jnp.add, [e32[:, c * LANES:(c + 1) * LANES] for c in range(n_lat // LANES)])
    den = jnp.sum(e_sum, axis=1, keepdims=True) + oc[:, HEAD_DIM:]
    for t in range(ROWS_PER_STEP):
        rows = slice(t * GRID_W, (t + 1) * GRID_W)
        vw = v_ref[pl.ds(starts[t], n_lat), :]
        o = jnp.dot(e[rows], vw, preferred_element_type=F32) + oc[rows, :HEAD_DIM]
        o_ref[rows, :] = (o / den[rows]).astype(o_ref.dtype)


def _attention(q, kv, tp, sb0, w_ada, b_ada, mod_col0):
    n_heads, s, _ = q.shape
    n_ctx = kv.shape[1] - s
    assert s % n_ctx == 0
    ctx_blk = s // n_ctx
    n_rows = s // GRID_W
    tq = ROWS_PER_STEP * GRID_W
    bph = n_rows // ROWS_PER_STEP
    d, n_mod = w_ada.shape
    n_side = n_mod - mod_col0
    tn = n_side // (n_heads * bph)
    assert tn * n_heads * bph == n_side and tn % LANES == 0 and mod_col0 % tn == 0
    blk0 = mod_col0 // tn
    return pl.pallas_call(
        functools.partial(_attn_kernel, n_rows=n_rows),
        grid=(n_heads, bph),
        in_specs=[
            pl.BlockSpec((None, tq, HEAD_DIM), lambda h, j: (h, j, 0)),
            pl.BlockSpec((None, s, HEAD_DIM), lambda h, j: (h, 0, 0)),
            pl.BlockSpec((None, s, HEAD_DIM), lambda h, j: (n_heads + h, 0, 0)),
            pl.BlockSpec((None, n_ctx, HEAD_DIM), lambda h, j: (h, ctx_blk, 0)),
            pl.BlockSpec((None, n_ctx, HEAD_DIM), lambda h, j: (n_heads + h, ctx_blk, 0)),
            pl.BlockSpec((None, N_PAIR_ROWS, GRID_W, 2 * GRID_W), lambda h, j: (h, 0, 0, 0)),
            pl.BlockSpec((d, LANES), lambda h, j: (0, 0)),
            pl.BlockSpec((d, tn), lambda h, j: (0, blk0 + h * bph + j)),
            pl.BlockSpec((1, tn), lambda h, j: (0, blk0 + h * bph + j)),
        ],
        out_specs=[pl.BlockSpec((None, tq, HEAD_DIM), lambda h, j: (h, j, 0)),
                   pl.BlockSpec((SUBLANES, tn), lambda h, j: (0, h * bph + j))],
        out_shape=[jax.ShapeDtypeStruct((n_heads, s, HEAD_DIM), BF16),
                   jax.ShapeDtypeStruct((SUBLANES, n_side), F32)],
        scratch_shapes=[pltpu.VMEM((2, HEAD_DIM, s), BF16)],
        compiler_params=_cparams(("arbitrary", "arbitrary")),
        name="nbr_attention",
    )(q, kv, kv, kv, kv, tp, sb0, w_ada, b_ada)


SGU_CHUNKS_PER_STEP = 4


def _sgu_merge_kernel(gu_ref, gv_ref, ob_ref, ws_ref, bst_ref, lg_ref, lb_ref, gn_ref, o_ref, oa_ref,
                      *, n_heads, w_sgu):
    v = _ln_rows(gv_ref[...].astype(F32)) * lg_ref[...] + lb_ref[...]
    v = v.astype(BF16)
    bst = bst_ref[...]
    for h in range(n_heads):
        cols = slice(h * HEAD_DIM, (h + 1) * HEAD_DIM)
        vh = jnp.concatenate(
            [v[c * CHUNK:(c + 1) * CHUNK, cols] for c in range(SGU_CHUNKS_PER_STEP)], axis=1)
        mixed = jnp.dot(ws_ref[h], vh, preferred_element_type=F32) + bst[:, h:h + 1]
        for c in range(SGU_CHUNKS_PER_STEP):
            rows = slice(c * CHUNK, (c + 1) * CHUNK)
            oa_ref[rows, cols] = gu_ref[rows, cols].astype(F32) * mixed[:, c * HEAD_DIM:(c + 1) * HEAD_DIM]
    gn = gn_ref[...]
    o_ref[:, :w_sgu] = (_rms_rows(oa_ref[...]) * gn[:, :w_sgu]).astype(o_ref.dtype)
    ob = jnp.concatenate([ob_ref[h] for h in range(ob_ref.shape[0])], axis=1).astype(F32)
    o_ref[:, w_sgu:] = (_rms_rows(ob) * gn[:, w_sgu:]).astype(o_ref.dtype)


def _sgu_merge(p, ob, ws, bst, lg, lb, gn, *, n_heads):
    s = p.shape[0]
    w_sgu = n_heads * HEAD_DIM
    n_na_heads = ob.shape[0]
    w_na = n_na_heads * HEAD_DIM
    tr = SGU_CHUNKS_PER_STEP * CHUNK
    return pl.pallas_call(
        functools.partial(_sgu_merge_kernel, n_heads=n_heads, w_sgu=w_sgu),
        grid=(s // tr,),
        in_specs=[
            pl.BlockSpec((tr, w_sgu), lambda i: (i, 0)),
            pl.BlockSpec((tr, w_sgu), lambda i: (i, 1)),
            pl.BlockSpec((n_na_heads, tr, HEAD_DIM), lambda i: (0, i, 0)),
            pl.BlockSpec((n_heads, CHUNK, CHUNK), lambda i: (0, 0, 0)),
            pl.BlockSpec((CHUNK, n_heads), lambda i: (0, 0)),
            pl.BlockSpec((1, w_sgu), lambda i: (0, 0)),
            pl.BlockSpec((1, w_sgu), lambda i: (0, 0)),
            pl.BlockSpec((1, w_sgu + w_na), lambda i: (0, 0)),
        ],
        out_specs=pl.BlockSpec((tr, w_sgu + w_na), lambda i: (i, 0)),
        out_shape=jax.ShapeDtypeStruct((s, w_sgu + w_na), BF16),
        scratch_shapes=[pltpu.VMEM((tr, w_sgu), F32)],
        compiler_params=_cparams(("parallel",)),
        name="sgu_merge",
    )(p, p, ob, ws, bst, lg, lb, gn)


RING_SLOTS = 3


def _ring_fetch(src_hbm, buf, sem, step, n_blocks):
    rows = buf.shape[1]

    def copy(s):
        start = s * rows if isinstance(s, int) else pl.multiple_of(s * rows, rows)
        slot = s % RING_SLOTS
        return pltpu.make_async_copy(src_hbm.at[pl.ds(start, rows)], buf.at[slot], sem.at[slot])

    @pl.when(step == 0)
    def _():
        for s in range(min(RING_SLOTS - 1, n_blocks)):
            copy(s).start()

    @pl.when(step + (RING_SLOTS - 1) < n_blocks)
    def _():
        copy(step + (RING_SLOTS - 1)).start()

    @pl.when(step < n_blocks)
    def _():
        copy(step).wait()

    return step % RING_SLOTS


def _res_ln_mod_kernel(x_hbm, st_ref, y_ref, g_ref, lg_ref, lb_ref, sc_ref, sh_ref, x1_ref, h_ref, xbuf, sem,
                       *, n_blocks):
    slot = _ring_fetch(x_hbm, xbuf, sem, pl.program_id(0), n_blocks)
    st = st_ref[...]
    xn = (xbuf[slot] - st[:, 0:1]) * st[:, 1:2]
    z = DEEPNORM_ALPHA * xn + g_ref[...] * y_ref[...].astype(F32)
    x1 = _ln_rows(z) * lg_ref[...] + lb_ref[...]
    x1_ref[...] = x1
    h_ref[...] = (x1 * (1.0 + sc_ref[...]) + sh_ref[...]).astype(h_ref.dtype)


def _res_ln_mod(x, stats, y, gate, lg, lb, sc, sh, tr):
    m, d = x.shape
    row = pl.BlockSpec((tr, d), lambda i: (i, 0))
    vec = pl.BlockSpec((1, d), lambda i: (0, 0))
    return pl.pallas_call(
        functools.partial(_res_ln_mod_kernel, n_blocks=m // tr),
        grid=(m // tr,),
        in_specs=[pl.BlockSpec(memory_space=pl.ANY), pl.BlockSpec((tr, LANES), lambda i: (i, 0)), row,
                  vec, vec, vec, vec, vec],
        out_specs=[row, row],
        out_shape=[jax.ShapeDtypeStruct((m, d), F32), jax.ShapeDtypeStruct((m, d), BF16)],
        scratch_shapes=[pltpu.VMEM((RING_SLOTS, tr, d), F32), pltpu.SemaphoreType.DMA((RING_SLOTS,))],
        compiler_params=_cparams(("arbitrary",)),
        name="res_ln_mod",
    )(x, stats, y, gate, lg, lb, sc, sh)


def _tiles(seq, n_ctx):
    return dict(row=256, proj_bm=1024, proj_bn=1024, kv_bm=(seq + n_ctx) // 8, ff_bn=512,
                down_bm=512, down_bk=1280)


def kernel(x, c, ctx, c_ctx, w_ada, b_ada, w_in, w_s, b_s, sgu_g, sgu_b, rpb, gn_g, w_o, ln_g, ln_b,
           w_gate, w_up, w_down):
    batch, seq, d = x.shape
    assert batch == 1 and w_ada.shape[0] == DEPTH
    t = _tiles(seq, ctx.shape[1])
    n_sgu_heads = w_s.shape[1]
    n_na_heads = rpb.shape[1]
    w_sgu = n_sgu_heads * HEAD_DIM
    w_na = n_na_heads * HEAD_DIM
    d_ff = w_gate.shape[2]
    assert w_in.shape[2] == 2 * w_sgu + 3 * w_na and w_sgu == w_na

    x2 = x.reshape(seq, d)
    ctx2 = ctx.reshape(ctx.shape[1], d)

    ccol = jnp.stack([c.reshape(d), c_ctx], axis=1)
    b_ada2 = b_ada[0].reshape(1, 6 * d)
    mod, sb0 = _modulation(ccol, w_ada[0], b_ada2, 2 * d)

    h, ln_stats = _ln_mod(x2, ctx2, mod, t["row"])

    bm, bn = t["proj_bm"], t["proj_bn"]
    guv = _mm_cast(h, [w_in[0]], m_rows=seq, bm=bm, bn=bn, n_cols=2 * w_sgu, epilogue="gelu",
                   name="in_proj_uv")
    q = _mm_cast(h, [w_in[0]], m_rows=seq, bm=bm, bn=bn, n_cols=w_na, col0=2 * w_sgu,
                 epilogue="scale", scale=HEAD_DIM ** -0.5 * LOG2_E, head_major=True, name="in_proj_q")
    kv = _mm_cast(h, [w_in[0]], bm=t["kv_bm"], bn=bn, n_cols=2 * w_na, col0=2 * w_sgu + w_na,
                  head_major=True, name="in_proj_kv")

    tp = _bias_table(rpb[0].reshape(-1), n_na_heads)
    ob, mod_b = _attention(q, kv, tp, sb0, w_ada[0], b_ada2, 2 * d)
    g1, sh2, sc2, g2 = [mod_b[0:1, i * d:(i + 1) * d] for i in range(4)]

    merged = _sgu_merge(guv, ob, w_s[0].astype(BF16), b_s[0].T, sgu_g, sgu_b, gn_g, n_heads=n_sgu_heads)

    y = _mm_cast(merged, [w_o[0]], bm=bm, bn=bn, n_cols=d, name="out_proj")
    x1, h2 = _res_ln_mod(x2, ln_stats, y, g1, ln_g[0, 0:1], ln_b[0, 0:1], sc2, sh2, t["row"])

    g, wd = _mm_cast(h2, [w_gate[0], w_up[0]], bm=bm, bn=t["ff_bn"], n_cols=d_ff, epilogue="swiglu",
                     side=w_down[0], name="swiglu_up")
    out = _down_ln(g, wd, x1, g2, ln_g[0, 1:2], ln_b[0, 1:2], bm=t["down_bm"], bk=t["down_bk"])
    return out.reshape(batch, seq, d)
```

```python
import functools
import math

import jax
import jax.numpy as jnp
from jax import lax
from jax.experimental import pallas as pl
from jax.experimental.pallas import tpu as pltpu

F32 = jnp.float32
BF16 = jnp.bfloat16

GRID_W = 64
HEAD_DIM = 128
CHUNK = 128
WIN_H = 8
WIN_W = 16
LN_EPS = 1e-5
DEPTH = 1
DEEPNORM_ALPHA = (2.0 * DEPTH) ** 0.25
LOG2_E = math.log2(math.e)

LANES = 128
SUBLANES = 8
VMEM_LIMIT = 58 * 1024 * 1024


def _cparams(sem):
    return pltpu.CompilerParams(dimension_semantics=sem, vmem_limit_bytes=VMEM_LIMIT)


def _gelu(x):
    return 0.5 * x * (1.0 + lax.erf(x * (1.0 / math.sqrt(2.0))))


def _silu(x):
    return x * jax.nn.sigmoid(x)


def _ln_rows(x):
    mu = jnp.mean(x, axis=-1, keepdims=True)
    xc = x - mu
    var = jnp.mean(xc * xc, axis=-1, keepdims=True)
    return xc * lax.rsqrt(var + LN_EPS)


def _rms_rows(x):
    return x * lax.rsqrt(jnp.mean(x * x, axis=-1, keepdims=True) + LN_EPS)


def _mod_kernel(ccol_ref, w_ref, b_ref, o_ref, sb0_ref, sb_ref, *, tn):
    d = w_ref.shape[0]

    @pl.when(pl.program_id(0) == 0)
    def _():
        s = _silu(ccol_ref[...])
        sb_ref[0] = jnp.broadcast_to(s[:, 0:1], (d, LANES))
        sb_ref[1] = jnp.broadcast_to(s[:, 1:2], (d, LANES))
        sb0_ref[...] = sb_ref[0]

    nl = tn // LANES

    def body(kc, accs):
        k0 = pl.multiple_of(kc * SUBLANES, SUBLANES)
        s0 = sb_ref[0, pl.ds(k0, SUBLANES), :]
        s1 = sb_ref[1, pl.ds(k0, SUBLANES), :]
        w = w_ref[pl.ds(k0, SUBLANES), :]
        out = []
        for j in range(nl):
            wj = w[:, j * LANES:(j + 1) * LANES]
            out.append(accs[2 * j] + wj * s0)
            out.append(accs[2 * j + 1] + wj * s1)
        return tuple(out)

    zero = jnp.zeros((SUBLANES, LANES), F32)
    accs = lax.fori_loop(0, d // SUBLANES, body, (zero,) * (2 * nl), unroll=8)
    r0 = jnp.concatenate([jnp.sum(accs[2 * j], axis=0, keepdims=True) for j in range(nl)], axis=1)
    r1 = jnp.concatenate([jnp.sum(accs[2 * j + 1], axis=0, keepdims=True) for j in range(nl)], axis=1)
    row = lax.broadcasted_iota(jnp.int32, (SUBLANES, tn), 0)
    b = b_ref[...]
    o_ref[...] = jnp.where(row == 0, r0 + b, jnp.where(row == 1, r1 + b, 0.0))


def _modulation(ccol, w_ada, b_ada, n):
    d = w_ada.shape[0]
    tn = 512
    return pl.pallas_call(
        functools.partial(_mod_kernel, tn=tn),
        grid=(n // tn,),
        in_specs=[
            pl.BlockSpec((d, 2), lambda j: (0, 0)),
            pl.BlockSpec((d, tn), lambda j: (0, j)),
            pl.BlockSpec((1, tn), lambda j: (0, j)),
        ],
        out_specs=[pl.BlockSpec((SUBLANES, tn), lambda j: (0, j)),
                   pl.BlockSpec((d, LANES), lambda j: (0, 0))],
        out_shape=[jax.ShapeDtypeStruct((SUBLANES, n), F32), jax.ShapeDtypeStruct((d, LANES), F32)],
        scratch_shapes=[pltpu.VMEM((2, d, LANES), F32)],
        compiler_params=_cparams(("arbitrary",)),
        name="modulation",
    )(ccol, w_ada, b_ada)


MOD_ACCS = 4


def _deferred_mod_tile(sb_ref, w_ref, b_ref, o_ref):
    d, tn = w_ref.shape
    nl = tn // LANES
    accs = [[None] * MOD_ACCS for _ in range(nl)]
    for kc in range(d // SUBLANES):
        rows = slice(kc * SUBLANES, (kc + 1) * SUBLANES)
        s = sb_ref[rows, :]
        for jj in range(nl):
            p = w_ref[rows, jj * LANES:(jj + 1) * LANES] * s
            a = kc % MOD_ACCS
            accs[jj][a] = p if accs[jj][a] is None else accs[jj][a] + p
    tot = [jnp.sum(functools.reduce(jnp.add, accs[jj]), axis=0, keepdims=True) for jj in range(nl)]
    o_ref[...] = jnp.broadcast_to(jnp.concatenate(tot, axis=1) + b_ref[...], o_ref.shape)


def _ln_mod_kernel(x_hbm, c_ref, mod_ref, o_ref, st_ref, xbuf, sem, *, n_x_blocks):
    d = c_ref.shape[1]
    i = pl.program_id(0)
    slot = _ring_fetch(x_hbm, xbuf, sem, i, n_x_blocks)

    def emit(x, row):
        sh = mod_ref[row:row + 1, 0:d]
        sc = mod_ref[row:row + 1, d:2 * d]
        mu = jnp.mean(x, axis=-1, keepdims=True)
        xc = x - mu
        rstd = lax.rsqrt(jnp.mean(xc * xc, axis=-1, keepdims=True) + LN_EPS)
        o_ref[...] = (xc * rstd * (1.0 + sc) + sh).astype(o_ref.dtype)
        lane = lax.broadcasted_iota(jnp.int32, st_ref.shape, 1)
        st_ref[...] = jnp.where(lane == 0, mu, rstd)

    @pl.when(i < n_x_blocks)
    def _():
        emit(xbuf[slot], 0)

    @pl.when(i >= n_x_blocks)
    def _():
        emit(c_ref[...], 1)


def _ln_mod(x, ctx, mod, tr):
    m, d = x.shape
    n_x_blocks = m // tr
    n_c_blocks = ctx.shape[0] // tr
    n_rows = m + ctx.shape[0]
    return pl.pallas_call(
        functools.partial(_ln_mod_kernel, n_x_blocks=n_x_blocks),
        grid=(n_x_blocks + n_c_blocks,),
        in_specs=[
            pl.BlockSpec(memory_space=pl.ANY),
            pl.BlockSpec((tr, d), lambda i: (jnp.maximum(i - n_x_blocks, 0), 0)),
            pl.BlockSpec((SUBLANES, 2 * d), lambda i: (0, 0)),
        ],
        out_specs=[pl.BlockSpec((tr, d), lambda i: (i, 0)), pl.BlockSpec((tr, LANES), lambda i: (i, 0))],
        out_shape=[jax.ShapeDtypeStruct((n_rows, d), BF16), jax.ShapeDtypeStruct((n_rows, LANES), F32)],
        scratch_shapes=[pltpu.VMEM((RING_SLOTS, tr, d), F32), pltpu.SemaphoreType.DMA((RING_SLOTS,))],
        compiler_params=_cparams(("arbitrary",)),
        name="ln_mod",
    )(x, ctx, mod)


CAST_ROWS = 256
SIDE_ROWS = 64
MM_CHUNK = 256


def _mm_cast_kernel(*refs, n_w, bn, n_blocks, m_blocks, col0, n_cols, epilogue, scale, has_side, head_major,
                    n_scaled=0):
    it = iter(refs)
    x_ref = next(it)
    w_hbm = [next(it) for _ in range(n_w)]
    side_in = next(it) if has_side else None
    o_ref = next(it)
    side_out = next(it) if has_side else None
    stage = [next(it) for _ in range(n_w)]
    wb = [next(it) for _ in range(n_w)]
    sem = next(it)

    j = pl.program_id(0)
    i = pl.program_id(1)
    k = stage[0].shape[0]
    w_last = n_cols - (n_blocks - 1) * bn

    def slab_copies(jj, width):
        return [pltpu.make_async_copy(w_hbm[a].at[:, pl.ds(col0 + jj * bn, width)],
                                      stage[a].at[:, pl.ds(0, width)], sem.at[a]) for a in range(n_w)]

    def for_slab(jj, fn):
        if w_last == bn:
            for cp in slab_copies(jj, bn):
                fn(cp)
            return

        @pl.when(jj < n_blocks - 1)
        def _():
            for cp in slab_copies(jj, bn):
                fn(cp)

        @pl.when(jj == n_blocks - 1)
        def _():
            for cp in slab_copies(n_blocks - 1, w_last):
                fn(cp)

    @pl.when((j == 0) & (i == 0))
    def _():
        for_slab(j, lambda cp: cp.start())

    @pl.when(i == 0)
    def _():
        for_slab(j, lambda cp: cp.wait())

        def cast_rows(r, carry):
            r0 = pl.multiple_of(r * CAST_ROWS, CAST_ROWS)
            for a in range(n_w):
                wb[a][pl.ds(r0, CAST_ROWS), :] = stage[a][pl.ds(r0, CAST_ROWS), :].astype(BF16)
            return carry

        lax.fori_loop(0, k // CAST_ROWS, cast_rows, 0)

    @pl.when((i == min(1, m_blocks - 1)) & (j + 1 < n_blocks))
    def _():
        for_slab(j + 1, lambda cp: cp.start())

    def compute(n_chunks):
        x = x_ref[...]
        for c in range(n_chunks):
            cols = slice(c * MM_CHUNK, (c + 1) * MM_CHUNK)
            accs = [jnp.dot(x, wb[a][:, cols], preferred_element_type=F32) for a in range(n_w)]
            if epilogue == "plain":
                y = accs[0]
            elif epilogue == "gelu":
                y = _gelu(accs[0])
            elif epilogue == "scale":
                y = accs[0] * scale
            elif epilogue == "scale_first":
                y = accs[0] * jnp.where(j < n_scaled, scale, 1.0)
            else:
                y = _silu(accs[0]) * accs[1]
            y = y.astype(o_ref.dtype)
            if head_major:
                for hh in range(MM_CHUNK // HEAD_DIM):
                    o_ref[c * (MM_CHUNK // HEAD_DIM) + hh] = y[:, hh * HEAD_DIM:(hh + 1) * HEAD_DIM]
            else:
                o_ref[:, cols] = y

    full_chunks = bn // MM_CHUNK
    last_chunks = -(-w_last // MM_CHUNK)
    if last_chunks == full_chunks:
        compute(full_chunks)
    else:
        @pl.when(j < n_blocks - 1)
        def _():
            compute(full_chunks)

        @pl.when(j == n_blocks - 1)
        def _():
            compute(last_chunks)

    if side_in is not None:
        side_out[...] = side_in[...].astype(side_out.dtype)


def _mm_cast(x, ws, *, bm, bn, n_cols, m_rows=None, col0=0, epilogue="plain", scale=1.0, side=None,
             head_major=False, n_scaled=0, name):
    m, k = x.shape
    m = m if m_rows is None else m_rows
    n_w = len(ws)
    assert m % bm == 0 and bn % MM_CHUNK == 0 and k % CAST_ROWS == 0
    n_blocks = -(-n_cols // bn)
    m_blocks = m // bm
    in_specs = [pl.BlockSpec((bm, k), lambda j, i: (i, 0))]
    in_specs += [pl.BlockSpec(memory_space=pl.ANY)] * n_w
    if head_major:
        assert n_cols % bn == 0
        out_specs = [pl.BlockSpec((bn // HEAD_DIM, bm, HEAD_DIM), lambda j, i: (j, i, 0))]
        out_shape = [jax.ShapeDtypeStruct((n_cols // HEAD_DIM, m, HEAD_DIM), BF16)]
    else:
        out_specs = [pl.BlockSpec((bm, bn), lambda j, i: (i, j))]
        out_shape = [jax.ShapeDtypeStruct((m, n_cols), BF16)]
    args = [x, *ws]
    if side is not None:
        rows, c = side.shape
        side_blocks = rows // SIDE_ROWS
        assert side_blocks * SIDE_ROWS == rows and side_blocks <= n_blocks * m_blocks
        side_spec = pl.BlockSpec(
            (SIDE_ROWS, c), lambda j, i: (jnp.minimum(j * m_blocks + i, side_blocks - 1), 0))
        in_specs.append(side_spec)
        out_specs.append(side_spec)
        out_shape.append(jax.ShapeDtypeStruct((rows, c), BF16))
        args.append(side)
    kern = functools.partial(
        _mm_cast_kernel, n_w=n_w, bn=bn, n_blocks=n_blocks, m_blocks=m_blocks, col0=col0, n_cols=n_cols,
        epilogue=epilogue, scale=scale, has_side=side is not None, head_major=head_major, n_scaled=n_scaled)
    out = pl.pallas_call(
        kern,
        grid=(n_blocks, m_blocks),
        in_specs=in_specs,
        out_specs=out_specs,
        out_shape=out_shape,
        scratch_shapes=([pltpu.VMEM((k, bn), F32)] * n_w + [pltpu.VMEM((k, bn), BF16)] * n_w
                        + [pltpu.SemaphoreType.DMA((n_w,))]),
        compiler_params=_cparams(("arbitrary", "arbitrary")),
        name=name,
    )(*args)
    return out if side is not None else out[0]


LN_ROWS = 64


def _down_ln_kernel(g_ref, w_ref, x_ref, gate_ref, lg_ref, lb_ref, o_ref, *, k_blocks, k_last):
    kk = pl.program_id(1)

    def part(kw):
        return jnp.dot(g_ref[:, :kw], w_ref[:kw, :], preferred_element_type=F32)

    bk = w_ref.shape[0]

    @pl.when(kk == 0)
    def _():
        o_ref[...] = part(bk)

    @pl.when((kk > 0) & (kk < k_blocks - 1))
    def _():
        o_ref[...] += part(bk)

    @pl.when(kk == k_blocks - 1)
    def _():
        o_ref[...] += part(k_last)
        gate = gate_ref[...]
        lg = lg_ref[...]
        lb = lb_ref[...]

        def ln_chunk(r, carry):
            rows = pl.ds(pl.multiple_of(r * LN_ROWS, LN_ROWS), LN_ROWS)
            z = DEEPNORM_ALPHA * x_ref[rows, :] + gate * o_ref[rows, :]
            o_ref[rows, :] = _ln_rows(z) * lg + lb
            return carry

        lax.fori_loop(0, o_ref.shape[0] // LN_ROWS, ln_chunk, 0)


def _down_ln(g, w, x, gate, lg, lb, *, bm, bk):
    m, k = g.shape
    n = w.shape[1]
    assert m % bm == 0 and w.shape[0] == k
    k_blocks = -(-k // bk)
    k_last = k - (k_blocks - 1) * bk
    assert k_blocks >= 2 and k_last % LANES == 0
    row = pl.BlockSpec((bm, n), lambda i, kk: (i, 0))
    vec = pl.BlockSpec((1, n), lambda i, kk: (0, 0))
    return pl.pallas_call(
        functools.partial(_down_ln_kernel, k_blocks=k_blocks, k_last=k_last),
        grid=(m // bm, k_blocks),
        in_specs=[
            pl.BlockSpec((bm, bk), lambda i, kk: (i, kk)),
            pl.BlockSpec((bk, n), lambda i, kk: (kk, 0)),
            row, vec, vec, vec,
        ],
        out_specs=row,
        out_shape=jax.ShapeDtypeStruct((m, n), F32),
        compiler_params=_cparams(("arbitrary", "arbitrary")),
        name="swiglu_down_ln",
    )(g, w, x, gate, lg, lb)


N_PAIR_ROWS = 2 * WIN_H - 2


def _bias_kernel(rpb_ref, o_ref):
    h = pl.program_id(0)
    n_dcol = 2 * WIN_W - 1
    q = lax.broadcasted_iota(jnp.int32, (GRID_W, 2 * GRID_W), 0)
    l = lax.broadcasted_iota(jnp.int32, (GRID_W, 2 * GRID_W), 1)
    kc = l & (GRID_W - 1)
    hi = l >= GRID_W
    dcol = jnp.clip(kc - q, -(WIN_W - 1), WIN_W - 1) + (WIN_W - 1)
    cs = jnp.clip(q - WIN_W // 2, 0, GRID_W - WIN_W)
    inside = (kc >= cs) & (kc < cs + WIN_W)
    base = h * ((2 * WIN_H - 1) * n_dcol)

    def body(dra, carry):
        acc = jnp.zeros((GRID_W, 2 * GRID_W), F32)
        for d in range(n_dcol):
            lo_v = rpb_ref[base + dra * n_dcol + d]
            hi_v = rpb_ref[base + (dra + 1) * n_dcol + d]
            acc = jnp.where(dcol == d, jnp.where(hi, hi_v, lo_v), acc)
        o_ref[dra] = jnp.where(inside, acc * LOG2_E, -jnp.inf)
        return carry

    lax.fori_loop(0, N_PAIR_ROWS, body, 0)


def _bias_table(rpb_flat, n_heads):
    return pl.pallas_call(
        _bias_kernel,
        grid=(n_heads,),
        in_specs=[pl.BlockSpec(memory_space=pltpu.SMEM)],
        out_specs=pl.BlockSpec((None, N_PAIR_ROWS, GRID_W, 2 * GRID_W), lambda h: (h, 0, 0, 0)),
        out_shape=jax.ShapeDtypeStruct((n_heads, N_PAIR_ROWS, GRID_W, 2 * GRID_W), F32),
        compiler_params=_cparams(("arbitrary",)),
        name="bias_table",
    )(rpb_flat)


ROWS_PER_STEP = 64


def _attn_kernel(q_ref, k_ref, v_ref, kc_ref, vc_ref, tp_ref, sb_ref, wa_ref, ba_ref, o_ref, mod_ref,
                 kt_ref, *, n_rows):
    jb = pl.program_id(1)
    s_keys = k_ref.shape[0]

    @pl.when(jb == 0)
    def _():
        kt_ref[0] = k_ref[...].T
        n_shift = s_keys - LANES
        kt_ref[1, :, :n_shift] = k_ref[GRID_W:GRID_W + n_shift, :].T

    _deferred_mod_tile(sb_ref, wa_ref, ba_ref, mod_ref)

    n_lat = WIN_H * GRID_W
    q = q_ref[...]
    kc = kc_ref[...]
    vc = vc_ref[...]
    dn_t = (((1,), (1,)), ((), ()))
    s_ctx = lax.dot_general(q, kc, dn_t, preferred_element_type=F32)
    starts, s_rows = [], []
    for t in range(ROWS_PER_STEP):
        r = jb * ROWS_PER_STEP + t
        ks = jnp.clip(r - WIN_H // 2, 0, n_rows - WIN_H)
        shift = r - ks
        start = pl.multiple_of(ks * GRID_W, GRID_W)
        starts.append(start)
        lane0 = pl.multiple_of((ks // 2) * LANES, LANES)
        kw_t = kt_ref[ks % 2, :, pl.ds(lane0, n_lat)]
        qt = q[t * GRID_W:(t + 1) * GRID_W]
        s = jnp.dot(qt, kw_t, preferred_element_type=F32)
        bias = jnp.concatenate(
            [tp_ref[2 * p - shift + (WIN_H - 1)] for p in range(WIN_H // 2)], axis=1)
        s_rows.append(s + bias)
    s_lat = jnp.concatenate(s_rows, axis=0)
    tiles = [s_lat[:, c * LANES:(c + 1) * LANES] for c in range(n_lat // LANES)]
    tiles += [s_ctx[:, c * LANES:(c + 1) * LANES] for c in range(s_ctx.shape[1] // LANES)]
    m = jnp.max(functools.reduce(jnp.maximum, tiles), axis=1, keepdims=True)
    e = jnp.exp2((s_lat - m).astype(BF16))
    ec = jnp.exp2((s_ctx - m).astype(BF16))
    ones = jnp.ones((vc.shape[0], HEAD_DIM), BF16)
    vc_aug = jnp.concatenate([vc, ones], axis=1)
    oc = jnp.dot(ec, vc_aug, preferred_element_type=F32)
    e32 = e.astype(F32)
    e_sum = functools.reduce(jnp.add, [e32[:, c * LANES:(c + 1) * LANES] for c in range(n_lat // LANES)])
    den = jnp.sum(e_sum, axis=1, keepdims=True) + oc[:, HEAD_DIM:]
    for t in range(ROWS_PER_STEP):
        rows = slice(t * GRID_W, (t + 1) * GRID_W)
        vw = v_ref[pl.ds(starts[t], n_lat), :]
        o = jnp.dot(e[rows], vw, preferred_element_type=F32) + oc[rows, :HEAD_DIM]
        o_ref[rows, :] = (o / den[rows]).astype(o_ref.dtype)


def _attention(qkv, s, tp, sb0, w_ada, b_ada, mod_col0):
    n_heads = qkv.shape[0] // 3
    n_ctx = qkv.shape[1] - s
    assert s % n_ctx == 0
    ctx_blk = s // n_ctx
    n_rows = s // GRID_W
    tq = ROWS_PER_STEP * GRID_W
    bph = n_rows // ROWS_PER_STEP
    d, n_mod = w_ada.shape
    n_side = n_mod - mod_col0
    tn = n_side // (n_heads * bph)
    assert tn * n_heads * bph == n_side and tn % LANES == 0 and mod_col0 % tn == 0
    blk0 = mod_col0 // tn
    return pl.pallas_call(
        functools.partial(_attn_kernel, n_rows=n_rows),
        grid=(n_heads, bph),
        in_specs=[
            pl.BlockSpec((None, tq, HEAD_DIM), lambda h, j: (h, j, 0)),
            pl.BlockSpec((None, s, HEAD_DIM), lambda h, j: (n_heads + h, 0, 0)),
            pl.BlockSpec((None, s, HEAD_DIM), lambda h, j: (2 * n_heads + h, 0, 0)),
            pl.BlockSpec((None, n_ctx, HEAD_DIM), lambda h, j: (n_heads + h, ctx_blk, 0)),
            pl.BlockSpec((None, n_ctx, HEAD_DIM), lambda h, j: (2 * n_heads + h, ctx_blk, 0)),
            pl.BlockSpec((None, N_PAIR_ROWS, GRID_W, 2 * GRID_W), lambda h, j: (h, 0, 0, 0)),
            pl.BlockSpec((d, LANES), lambda h, j: (0, 0)),
            pl.BlockSpec((d, tn), lambda h, j: (0, blk0 + h * bph + j)),
            pl.BlockSpec((1, tn), lambda h, j: (0, blk0 + h * bph + j)),
        ],
        out_specs=[pl.BlockSpec((None, tq, HEAD_DIM), lambda h, j: (h, j, 0)),
                   pl.BlockSpec((SUBLANES, tn), lambda h, j: (0, h * bph + j))],
        out_shape=[jax.ShapeDtypeStruct((n_heads, s, HEAD_DIM), BF16),
                   jax.ShapeDtypeStruct((SUBLANES, n_side), F32)],
        scratch_shapes=[pltpu.VMEM((2, HEAD_DIM, s), BF16)],
        compiler_params=_cparams(("arbitrary", "arbitrary")),
        name="nbr_attention",
    )(qkv, qkv, qkv, qkv, qkv, tp, sb0, w_ada, b_ada)


SGU_CHUNKS_PER_STEP = 4


def _sgu_merge_kernel(gu_ref, gv_ref, ob_ref, ws_ref, bst_ref, lg_ref, lb_ref, gn_ref, o_ref, oa_ref,
                      *, n_heads, w_sgu):
    v = _ln_rows(gv_ref[...].astype(F32)) * lg_ref[...] + lb_ref[...]
    v = v.astype(BF16)
    bst = bst_ref[...]
    for h in range(n_heads):
        cols = slice(h * HEAD_DIM, (h + 1) * HEAD_DIM)
        vh = jnp.concatenate(
            [v[c * CHUNK:(c + 1) * CHUNK, cols] for c in range(SGU_CHUNKS_PER_STEP)], axis=1)
        mixed = jnp.dot(ws_ref[h], vh, preferred_element_type=F32) + bst[:, h:h + 1]
        for c in range(SGU_CHUNKS_PER_STEP):
            rows = slice(c * CHUNK, (c + 1) * CHUNK)
            oa_ref[rows, cols] = gu_ref[rows, cols].astype(F32) * mixed[:, c * HEAD_DIM:(c + 1) * HEAD_DIM]
    gn = gn_ref[...]
    o_ref[:, :w_sgu] = (_rms_rows(oa_ref[...]) * gn[:, :w_sgu]).astype(o_ref.dtype)
    ob = jnp.concatenate([ob_ref[h] for h in range(ob_ref.shape[0])], axis=1).astype(F32)
    o_ref[:, w_sgu:] = (_rms_rows(ob) * gn[:, w_sgu:]).astype(o_ref.dtype)


def _sgu_merge(p, ob, ws, bst, lg, lb, gn, *, n_heads):
    s = p.shape[0]
    w_sgu = n_heads * HEAD_DIM
    n_na_heads = ob.shape[0]
    w_na = n_na_heads * HEAD_DIM
    tr = SGU_CHUNKS_PER_STEP * CHUNK
    return pl.pallas_call(
        functools.partial(_sgu_merge_kernel, n_heads=n_heads, w_sgu=w_sgu),
        grid=(s // tr,),
        in_specs=[
            pl.BlockSpec((tr, w_sgu), lambda i: (i, 0)),
            pl.BlockSpec((tr, w_sgu), lambda i: (i, 1)),
            pl.BlockSpec((n_na_heads, tr, HEAD_DIM), lambda i: (0, i, 0)),
            pl.BlockSpec((n_heads, CHUNK, CHUNK), lambda i: (0, 0, 0)),
            pl.BlockSpec((CHUNK, n_heads), lambda i: (0, 0)),
            pl.BlockSpec((1, w_sgu), lambda i: (0, 0)),
            pl.BlockSpec((1, w_sgu), lambda i: (0, 0)),
            pl.BlockSpec((1, w_sgu + w_na), lambda i: (0, 0)),
        ],
        out_specs=pl.BlockSpec((tr, w_sgu + w_na), lambda i: (i, 0)),
        out_shape=jax.ShapeDtypeStruct((s, w_sgu + w_na), BF16),
        scratch_shapes=[pltpu.VMEM((tr, w_sgu), F32)],
        compiler_params=_cparams(("parallel",)),
        name="sgu_merge",
    )(p, p, ob, ws, bst, lg, lb, gn)


RING_SLOTS = 3


def _ring_fetch(src_hbm, buf, sem, step, n_blocks):
    rows = buf.shape[1]

    def copy(s):
        start = s * rows if isinstance(s, int) else pl.multiple_of(s * rows, rows)
        slot = s % RING_SLOTS
        return pltpu.make_async_copy(src_hbm.at[pl.ds(start, rows)], buf.at[slot], sem.at[slot])

    @pl.when(step == 0)
    def _():
        for s in range(min(RING_SLOTS - 1, n_blocks)):
            copy(s).start()

    @pl.when(step + (RING_SLOTS - 1) < n_blocks)
    def _():
        copy(step + (RING_SLOTS - 1)).start()

    @pl.when(step < n_blocks)
    def _():
        copy(step).wait()

    return step % RING_SLOTS


def _res_ln_mod_kernel(x_hbm, st_ref, y_ref, g_ref, lg_ref, lb_ref, sc_ref, sh_ref, x1_ref, h_ref, xbuf, sem,
                       *, n_blocks):
    slot = _ring_fetch(x_hbm, xbuf, sem, pl.program_id(0), n_blocks)
    st = st_ref[...]
    xn = (xbuf[slot] - st[:, 0:1]) * st[:, 1:2]
    z = DEEPNORM_ALPHA * xn + g_ref[...] * y_ref[...].astype(F32)
    x1 = _ln_rows(z) * lg_ref[...] + lb_ref[...]
    x1_ref[...] = x1
    h_ref[...] = (x1 * (1.0 + sc_ref[...]) + sh_ref[...]).astype(h_ref.dtype)


def _res_ln_mod(x, stats, y, gate, lg, lb, sc, sh, tr):
    m, d = x.shape
    row = pl.BlockSpec((tr, d), lambda i: (i, 0))
    vec = pl.BlockSpec((1, d), lambda i: (0, 0))
    return pl.pallas_call(
        functools.partial(_res_ln_mod_kernel, n_blocks=m // tr),
        grid=(m // tr,),
        in_specs=[pl.BlockSpec(memory_space=pl.ANY), pl.BlockSpec((tr, LANES), lambda i: (i, 0)), row,
                  vec, vec, vec, vec, vec],
        out_specs=[row, row],
        out_shape=[jax.ShapeDtypeStruct((m, d), F32), jax.ShapeDtypeStruct((m, d), BF16)],
        scratch_shapes=[pltpu.VMEM((RING_SLOTS, tr, d), F32), pltpu.SemaphoreType.DMA((RING_SLOTS,))],
        compiler_params=_cparams(("arbitrary",)),
        name="res_ln_mod",
    )(x, stats, y, gate, lg, lb, sc, sh)


def _tiles(seq, n_ctx):
    return dict(row=256, proj_bm=1024, proj_bn=1024, kv_bm=(seq + n_ctx) // 8, ff_bn=512,
                down_bm=512, down_bk=1280)


def kernel(x, c, ctx, c_ctx, w_ada, b_ada, w_in, w_s, b_s, sgu_g, sgu_b, rpb, gn_g, w_o, ln_g, ln_b,
           w_gate, w_up, w_down):
    batch, seq, d = x.shape
    assert batch == 1 and w_ada.shape[0] == DEPTH
    t = _tiles(seq, ctx.shape[1])
    n_sgu_heads = w_s.shape[1]
    n_na_heads = rpb.shape[1]
    w_sgu = n_sgu_heads * HEAD_DIM
    w_na = n_na_heads * HEAD_DIM
    d_ff = w_gate.shape[2]
    assert w_in.shape[2] == 2 * w_sgu + 3 * w_na and w_sgu == w_na

    x2 = x.reshape(seq, d)
    ctx2 = ctx.reshape(ctx.shape[1], d)

    ccol = jnp.stack([c.reshape(d), c_ctx], axis=1)
    b_ada2 = b_ada[0].reshape(1, 6 * d)
    mod, sb0 = _modulation(ccol, w_ada[0], b_ada2, 2 * d)

    h, ln_stats = _ln_mod(x2, ctx2, mod, t["row"])

    bm, bn = t["proj_bm"], t["proj_bn"]
    guv = _mm_cast(h, [w_in[0]], m_rows=seq, bm=bm, bn=bn, n_cols=2 * w_sgu, epilogue="gelu",
                   name="in_proj_uv")
    qkv = _mm_cast(h, [w_in[0]], bm=t["kv_bm"], bn=bn, n_cols=3 * w_na, col0=2 * w_sgu,
                   epilogue="scale_first", scale=HEAD_DIM ** -0.5 * LOG2_E, n_scaled=w_na // bn,
                   head_major=True, name="in_proj_qkv")

    tp = _bias_table(rpb[0].reshape(-1), n_na_heads)
    ob, mod_b = _attention(qkv, seq, tp, sb0, w_ada[0], b_ada2, 2 * d)
    g1, sh2, sc2, g2 = [mod_b[0:1, i * d:(i + 1) * d] for i in range(4)]

    merged = _sgu_merge(guv, ob, w_s[0].astype(BF16), b_s[0].T, sgu_g, sgu_b, gn_g, n_heads=n_sgu_heads)

    y = _mm_cast(merged, [w_o[0]], bm=bm, bn=bn, n_cols=d, name="out_proj")
    x1, h2 = _res_ln_mod(x2, ln_stats, y, g1, ln_g[0, 0:1], ln_b[0, 0:1], sc2, sh2, t["row"])

    g, wd = _mm_cast(h2, [w_gate[0], w_up[0]], bm=bm, bn=t["ff_bn"], n_cols=d_ff, epilogue="swiglu",
                     side=w_down[0], name="swiglu_up")
    out = _down_ln(g, wd, x1, g2, ln_g[0, 1:2], ln_b[0, 1:2], bm=t["down_bm"], bk=t["down_bk"])
    return out.reshape(batch, seq, d)
```
